```python
import math
import jax, jax.numpy as jnp
from jax import lax
import numpy as np

D_MODEL = 1024
BATCH = 8
SEQ = 8192
DEPTH = 2
DEC_BATCH = 8
DEC_SEQ = 16
PAST_LEN = 2048

CHUNK = 64
N_A = max(1, DEPTH // 2)
N_B = DEPTH - N_A
N_DENSE = (DEPTH + 1) // 2
N_MOE = DEPTH // 2
A_HEADS = 8
A_DQK = 64
A_DV = 2 * A_DQK
A_QK_COLS = A_HEADS * 2 * A_DQK
A_QKV_COLS = 2 * A_QK_COLS + A_HEADS * A_DV
B_HEADS = 16
B_KV_HEADS = 2
B_GROUPS = B_HEADS // B_KV_HEADS
B_HD = 64
WINDOW = 128
WIN_CHUNKS = WINDOW // CHUNK
BAND = (WIN_CHUNKS + 1) * CHUNK
ROT_DIM = 16
ROPE_THETA = 500000.0
EPS = 1e-6
Q_BLOCK = 128
FF_DENSE = 2816
N_EXPERTS = 8
TOP_K = 2
FF_EXPERT = 3584

kernel_name = "yoco_diffattn_swa_sink_moe_stream_step"


def rmsnorm(x, g):
    xf = x.astype(jnp.float32)
    y = xf * lax.rsqrt(jnp.mean(xf * xf, axis=-1, keepdims=True) + EPS)
    return (y * g.astype(jnp.float32)).astype(x.dtype)


def modulate(x, g, shift, scale):
    return rmsnorm(x, g) * (1 + scale[:, None]) + shift[:, None]


def rope(x, pos):
    half = ROT_DIM // 2
    inv = ROPE_THETA ** (-(jnp.arange(half, dtype=jnp.float32) * 2.0 / ROT_DIM))
    ang = pos.astype(jnp.float32)[:, None] * inv[None, :]
    shape = (ang.shape[0],) + (1,) * (x.ndim - 3) + (half,)
    cos = jnp.cos(ang).reshape(shape)
    sin = jnp.sin(ang).reshape(shape)
    xr = x[..., :ROT_DIM].astype(jnp.float32)
    x1, x2 = xr[..., :half], xr[..., half:]
    rot = jnp.concatenate([x1 * cos - x2 * sin, x2 * cos + x1 * sin], axis=-1).astype(x.dtype)
    return jnp.concatenate([rot, x[..., ROT_DIM:]], axis=-1)


def diff_core(q, k, v, q_pos, k_pos, lam):
    s = jnp.einsum('bqhcd,bkhcd->bhcqk', q, k, preferred_element_type=jnp.float32) * (A_DQK ** -0.5)
    mask = (k_pos[None, :] // CHUNK) <= (q_pos[:, None] // CHUNK)
    s = jnp.where(mask, s, -jnp.inf)
    p = jax.nn.softmax(s, axis=-1)
    a = p[:, :, 0] - lam * p[:, :, 1]
    return jnp.einsum('bhqk,bkhd->bqhd', a.astype(v.dtype), v)


def diff_blocked(q, k, v, lam):
    B, T = q.shape[:2]
    nb = T // Q_BLOCK
    qb = q.reshape((B, nb, Q_BLOCK) + q.shape[2:]).swapaxes(0, 1)
    pos = jnp.arange(T, dtype=jnp.int32)
    pb = pos.reshape(nb, Q_BLOCK)
    ob = lax.map(lambda a: diff_core(a[0], k, v, a[1], pos, lam), (qb, pb))
    return ob.swapaxes(0, 1).reshape((B, T) + ob.shape[3:])


def diff_mixer(u, w_qkv, q_g, k_g, lam_vec, subln_g, w_o, layer_idx, pos, past_k, past_v):
    B, T, _ = u.shape
    qkv = u @ w_qkv
    q = qkv[..., :A_QK_COLS].reshape(B, T, A_HEADS, 2, A_DQK)
    k = qkv[..., A_QK_COLS:2 * A_QK_COLS].reshape(B, T, A_HEADS, 2, A_DQK)
    v = qkv[..., 2 * A_QK_COLS:].reshape(B, T, A_HEADS, A_DV)
    q = rope(rmsnorm(q, q_g), pos)
    k = rope(rmsnorm(k, k_g), pos)
    lam_init = 0.8 - 0.6 * math.exp(-0.3 * layer_idx)
    lv = lam_vec.astype(jnp.float32)
    lam = jnp.exp(jnp.sum(lv[0] * lv[1])) - jnp.exp(jnp.sum(lv[2] * lv[3])) + lam_init
    if past_k is None:
        o = diff_blocked(q, k, v, lam)
    else:
        kk = jnp.concatenate([past_k, k], axis=1)
        vv = jnp.concatenate([past_v, v], axis=1)
        k_pos = jnp.arange(kk.shape[1], dtype=jnp.int32)
        o = diff_core(q, kk, vv, pos, k_pos, lam)
    o = rmsnorm(o, subln_g) * (1 - lam_init)
    return o.reshape(B, T, A_HEADS * A_DV) @ w_o, k, v


def shared_kv(h, g, shift, scale, w_kv, k_g, pos):
    B, T, _ = h.shape
    kv = modulate(h, g, shift, scale) @ w_kv
    k = kv[..., :B_KV_HEADS * B_HD].reshape(B, T, B_KV_HEADS, B_HD)
    v = kv[..., B_KV_HEADS * B_HD:].reshape(B, T, B_KV_HEADS, B_HD)
    return rope(rmsnorm(k, k_g), pos), v


def band(k):
    B, T = k.shape[:2]
    nc = T // CHUNK
    kp = jnp.pad(k, ((0, 0), (WIN_CHUNKS * CHUNK, 0), (0, 0), (0, 0)))
    kb = kp.reshape((B, nc + WIN_CHUNKS, CHUNK) + k.shape[2:])
    return jnp.concatenate([kb[:, j:j + nc] for j in range(WIN_CHUNKS + 1)], axis=2)


def sink_core(q, k, v, q_pos, k_pos, sinks):
    s = jnp.einsum('bcqngd,bcknd->bcngqk', q, k, preferred_element_type=jnp.float32) * (B_HD ** -0.5)
    qc = q_pos[:, :, None] // CHUNK
    kc = k_pos[:, None, :] // CHUNK
    mask = (kc <= qc) & (kc >= qc - WIN_CHUNKS) & (k_pos[:, None, :] >= 0)
    s = jnp.where(mask[None, :, None, None], s, -jnp.inf)
    sk = sinks.astype(jnp.float32).reshape(1, 1, B_KV_HEADS, B_GROUPS, 1, 1)
    m = jnp.maximum(jnp.max(s, axis=-1, keepdims=True), sk)
    e = jnp.exp(s - m)
    p = e / (jnp.sum(e, axis=-1, keepdims=True) + jnp.exp(sk - m))
    return jnp.einsum('bcngqk,bcknd->bcqngd', p.astype(v.dtype), v)


def window_mixer(u, w_q, q_g, sinks, w_o, kb, vb, q_pos, k_pos):
    B, T, _ = u.shape
    q = (u @ w_q).reshape(B, T, B_KV_HEADS, B_GROUPS, B_HD)
    q = rope(rmsnorm(q, q_g), q_pos.reshape(-1))
    nc = q_pos.shape[0]
    q = q.reshape(B, nc, T // nc, B_KV_HEADS, B_GROUPS, B_HD)
    o = sink_core(q, kb, vb, q_pos, k_pos, sinks)
    return o.reshape(B, T, B_HEADS * B_HD) @ w_o


def swiglu(u, w_gu, w_d):
    g, up = jnp.split(u @ w_gu, 2, axis=-1)
    return (jax.nn.silu(g) * up) @ w_d


def moe(u, router, w_gu, w_d):
    B, T, D = u.shape
    x = u.reshape(B * T, D)
    logits = (x @ router).astype(jnp.float32)
    top_v, top_i = lax.top_k(logits, TOP_K)
    gates = jax.nn.softmax(top_v, axis=-1)
    dense_g = jnp.sum(jax.nn.one_hot(top_i, N_EXPERTS, dtype=jnp.float32) * gates[..., None], axis=1)
    y = jnp.zeros_like(x)
    for e in range(N_EXPERTS):
        y = y + dense_g[:, e:e + 1].astype(x.dtype) * swiglu(x, w_gu[e], w_d[e])
    return y.reshape(B, T, D)


def trunk(x, c, p, past):
    B, T, _ = x.shape
    past_len = 0 if past is None else past[0].shape[2]
    pos = past_len + jnp.arange(T, dtype=jnp.int32)
    sc = jax.nn.silu(c)
    h = x
    a_ks, a_vs = [], []
    k_sh = v_sh = None
    kb = vb = q_pos = k_pos = None
    for l in range(DEPTH):
        mod = (sc @ p['w_mod'][l] + p['b_mod'][l]).reshape(B, 6, D_MODEL)
        u = modulate(h, p['norm_g'][l, 0], mod[:, 0], mod[:, 1])
        if l < N_A:
            y, k, v = diff_mixer(u, p['a_w_qkv'][l], p['a_q_norm'][l], p['a_k_norm'][l],
                                 p['a_lambda'][l], p['a_subln'][l], p['a_w_o'][l], l, pos,
                                 None if past is None else past[0][l],
                                 None if past is None else past[1][l])
            a_ks.append(k)
            a_vs.append(v)
        else:
            if k_sh is None:
                kv_mod = (sc @ p['kv_w_mod'] + p['kv_b_mod']).reshape(B, 2, D_MODEL)
                k_sh, v_sh = shared_kv(h, p['kv_norm_g'], kv_mod[:, 0], kv_mod[:, 1],
                                       p['kv_w'], p['kv_k_norm'], pos)
                if past is None:
                    nc = T // CHUNK
                    kb, vb = band(k_sh), band(v_sh)
                    q_pos = pos.reshape(nc, CHUNK)
                    k_pos = (jnp.arange(nc, dtype=jnp.int32)[:, None] * CHUNK - WIN_CHUNKS * CHUNK
                             + jnp.arange(BAND, dtype=jnp.int32)[None, :])
                else:
                    win = past[2].shape[1]
                    kb = jnp.concatenate([past[2], k_sh], axis=1)[:, None]
                    vb = jnp.concatenate([past[3], v_sh], axis=1)[:, None]
                    q_pos = pos[None]
                    k_pos = (past_len - win + jnp.arange(win + T, dtype=jnp.int32))[None]
            j = l - N_A
            y = window_mixer(u, p['b_w_q'][j], p['b_q_norm'][j], p['b_sinks'][j], p['b_w_o'][j],
                             kb, vb, q_pos, k_pos)
        h = h + mod[:, 2, None] * y
        u = modulate(h, p['norm_g'][l, 1], mod[:, 3], mod[:, 4])
        if l % 2 == 0:
            f = swiglu(u, p['f_w_gu'][l // 2], p['f_w_d'][l // 2])
        else:
            f = moe(u, p['m_router'][l // 2], p['m_w_gu'][l // 2], p['m_w_d'][l // 2])
        h = h + mod[:, 5, None] * f
    return h, jnp.stack(a_ks), jnp.stack(a_vs), k_sh, v_sh


def _nrm(k, shape, scale):
    return jax.random.normal(k, shape, jnp.float32) * scale


def setup_inputs(seed: int = 0) -> dict:
    key = jax.random.key(seed)
    ks = jax.random.split(key, 40)
    D = D_MODEL
    win = min(WINDOW, PAST_LEN)
    return {
        'x_prompt': _nrm(ks[0], (BATCH, SEQ, D), 1.0),
        'x_sample': _nrm(ks[1], (DEC_BATCH, DEC_SEQ, D), 1.0),
        'c_prompt': _nrm(ks[2], (BATCH, D), 1.0),
        'c_sample': _nrm(ks[3], (DEC_BATCH, D), 1.0),
        'cache_a_k': _nrm(ks[4], (N_A, DEC_BATCH, PAST_LEN, A_HEADS, 2, A_DQK), 1.0),
        'cache_a_v': _nrm(ks[5], (N_A, DEC_BATCH, PAST_LEN, A_HEADS, A_DV), 1.0),
        'cache_b_k': _nrm(ks[6], (DEC_BATCH, win, B_KV_HEADS, B_HD), 1.0),
        'cache_b_v': _nrm(ks[7], (DEC_BATCH, win, B_KV_HEADS, B_HD), 1.0),
        'w_mod': _nrm(ks[8], (DEPTH, D, 6 * D), 0.5 * D ** -0.5),
        'b_mod': _nrm(ks[9], (DEPTH, 6 * D), 0.01),
        'norm_g': 1.0 + _nrm(ks[10], (DEPTH, 2, D), 0.02),
        'a_w_qkv': _nrm(ks[11], (N_A, D, A_QKV_COLS), D ** -0.5),
        'a_q_norm': 1.0 + _nrm(ks[12], (N_A, 2, A_DQK), 0.02),
        'a_k_norm': 1.0 + _nrm(ks[13], (N_A, 2, A_DQK), 0.02),
        'a_lambda': _nrm(ks[14], (N_A, 4, A_DQK), 0.1),
        'a_subln': 1.0 + _nrm(ks[15], (N_A, A_DV), 0.02),
        'a_w_o': _nrm(ks[16], (N_A, A_HEADS * A_DV, D), (A_HEADS * A_DV) ** -0.5),
        'kv_norm_g': 1.0 + _nrm(ks[17], (D,), 0.02),
        'kv_w_mod': _nrm(ks[18], (D, 2 * D), 0.5 * D ** -0.5),
        'kv_b_mod': _nrm(ks[19], (2 * D,), 0.01),
        'kv_w': _nrm(ks[20], (D, 2 * B_KV_HEADS * B_HD), D ** -0.5),
        'kv_k_norm': 1.0 + _nrm(ks[21], (B_HD,), 0.02),
        'b_w_q': _nrm(ks[22], (N_B, D, B_HEADS * B_HD), D ** -0.5),
        'b_q_norm': 1.0 + _nrm(ks[23], (N_B, B_HD), 0.02),
        'b_sinks': _nrm(ks[24], (N_B, B_HEADS), 0.5),
        'b_w_o': _nrm(ks[25], (N_B, B_HEADS * B_HD, D), (B_HEADS * B_HD) ** -0.5),
        'f_w_gu': _nrm(ks[26], (N_DENSE, D, 2 * FF_DENSE), D ** -0.5),
        'f_w_d': _nrm(ks[27], (N_DENSE, FF_DENSE, D), FF_DENSE ** -0.5),
        'm_router': _nrm(ks[28], (N_MOE, D, N_EXPERTS), D ** -0.5),
        'm_w_gu': _nrm(ks[29], (N_MOE, N_EXPERTS, D, 2 * FF_EXPERT), D ** -0.5),
        'm_w_d': _nrm(ks[30], (N_MOE, N_EXPERTS, FF_EXPERT, D), FF_EXPERT ** -0.5),
    }


def reference(x_prompt, x_sample, c_prompt, c_sample, cache_a_k, cache_a_v, cache_b_k, cache_b_v,
              w_mod, b_mod, norm_g, a_w_qkv, a_q_norm, a_k_norm, a_lambda, a_subln, a_w_o,
              kv_norm_g, kv_w_mod, kv_b_mod, kv_w, kv_k_norm, b_w_q, b_q_norm, b_sinks, b_w_o,
              f_w_gu, f_w_d, m_router, m_w_gu, m_w_d):
    p = dict(w_mod=w_mod, b_mod=b_mod, norm_g=norm_g, a_w_qkv=a_w_qkv, a_q_norm=a_q_norm,
             a_k_norm=a_k_norm, a_lambda=a_lambda, a_subln=a_subln, a_w_o=a_w_o,
             kv_norm_g=kv_norm_g, kv_w_mod=kv_w_mod, kv_b_mod=kv_b_mod, kv_w=kv_w,
             kv_k_norm=kv_k_norm, b_w_q=b_w_q, b_q_norm=b_q_norm, b_sinks=b_sinks, b_w_o=b_w_o,
             f_w_gu=f_w_gu, f_w_d=f_w_d, m_router=m_router, m_w_gu=m_w_gu, m_w_d=m_w_d)
    y_prompt, a_k_p, a_v_p, b_k_p, b_v_p = trunk(x_prompt, c_prompt, p, None)
    y_sample, a_k_s, a_v_s, b_k_s, b_v_s = trunk(
        x_sample, c_sample, p, (cache_a_k, cache_a_v, cache_b_k, cache_b_v))
    w = min(WINDOW, x_prompt.shape[1])
    return (y_prompt, y_sample, a_k_p, a_v_p, b_k_p[:, -w:], b_v_p[:, -w:], a_k_s, a_v_s, b_k_s, b_v_s)
```

```python
import functools
import math

import jax
import jax.numpy as jnp
from jax import lax
from jax.experimental import pallas as pl
from jax.experimental.pallas import tpu as pltpu

CHUNK = 64
WIN_CHUNKS = 2
ROT_DIM = 16
ROPE_THETA = 500000.0
EPS = 1e-6
HEAD_DIM = 64
N_ROUTED = 2

LANES = 128
MXU_DIM = 256
VMEM_LIMIT_BYTES = 56 * 1024 * 1024

F32 = jnp.float32
BF16 = jnp.bfloat16
NEG_INF = float("-inf")


def _cparams(*sem):
    return pltpu.CompilerParams(dimension_semantics=sem, vmem_limit_bytes=VMEM_LIMIT_BYTES)


def _row_tiling(batch, seq, rows):
    if seq >= rows:
        assert seq % rows == 0
        return 1, rows
    nb = max(1, min(batch, rows // seq))
    while batch % nb:
        nb -= 1
    return nb, seq


def _const_spec(shape):
    zeros = (0,) * len(shape)
    return pl.BlockSpec(shape, lambda *_: zeros, pipeline_mode=pl.Buffered(1))


def _sigmoid(x):
    return 1.0 / (1.0 + jnp.exp(-x))


def _rms_scale(x):
    return x * lax.rsqrt(jnp.mean(x * x, axis=-1, keepdims=True) + EPS)


def _modulate(x, g, shift, scale):
    return (_rms_scale(x) * g) * (1.0 + scale) + shift


def _group_norm_rope(x, gsum_ref, gain, cos, s1, s2):
    width = x.shape[1]
    slab = MXU_DIM if width % MXU_DIM == 0 else LANES
    gsum = gsum_ref[:slab, :slab]
    pieces = []
    for j in range(width // slab):
        xs = x[:, j * slab:(j + 1) * slab]
        ss = jnp.dot((xs * xs).astype(BF16), gsum, preferred_element_type=F32)
        y = xs * lax.rsqrt(ss * (1.0 / HEAD_DIM) + EPS) * gain[:, j * slab:(j + 1) * slab]
        for t in range(slab // LANES):
            yt = y[:, t * LANES:(t + 1) * LANES]
            pieces.append(yt * cos + pltpu.roll(yt, LANES - ROT_DIM // 2, 1) * s1
                          + pltpu.roll(yt, ROT_DIM // 2, 1) * s2)
    return pieces


def _tile_rows(t, nb):
    return t if nb == 1 else jnp.concatenate([t] * nb, axis=0)


def _mod_kernel(c_ref, w_ref, b_ref, o_ref):
    c = c_ref[...]
    sc = (c * _sigmoid(c)).astype(BF16)
    o_ref[...] = jnp.dot(sc, w_ref[...].astype(BF16), preferred_element_type=F32) + b_ref[...]


def _mod_call(c, w, b):
    rows, d = c.shape
    n = w.shape[1]
    tn = 2048
    assert n % tn == 0
    return pl.pallas_call(
        _mod_kernel,
        out_shape=jax.ShapeDtypeStruct((rows, n), F32),
        grid=(n // tn,),
        in_specs=[pl.BlockSpec((rows, d), lambda j: (0, 0)),
                  pl.BlockSpec((d, tn), lambda j: (0, j)),
                  pl.BlockSpec((1, tn), lambda j: (0, j))],
        out_specs=pl.BlockSpec((rows, tn), lambda j: (0, j)),
        compiler_params=_cparams("parallel"),
        name="adaln_mod",
    )(c, w, b.reshape(1, n))


def _qkv_a_kernel(x_ref, sh_ref, sc_ref, g_ref, w_ref, gsum_ref, qg_ref, kg_ref,
                  cos_ref, s1_ref, s2_ref, qh_ref, kh_ref, vh_ref, ak_ref, av_ref,
                  *, nb, tb, heads):
    x = x_ref[...]
    d = x.shape[-1]
    u = _modulate(x, g_ref[...], sh_ref[...], sc_ref[...]).reshape(nb * tb, d).astype(BF16)
    qkv = jnp.dot(u, w_ref[...], preferred_element_type=F32)
    nqk = heads * 2 * HEAD_DIM
    cos, s1, s2 = (_tile_rows(r[...], nb) for r in (cos_ref, s1_ref, s2_ref))
    q_p = _group_norm_rope(qkv[:, :nqk], gsum_ref, qg_ref[...], cos, s1, s2)
    k_p = _group_norm_rope(qkv[:, nqk:2 * nqk], gsum_ref, kg_ref[...], cos, s1, s2)
    qk_scale = HEAD_DIM ** -0.5
    for h in range(heads):
        cols = slice(h * LANES, (h + 1) * LANES)
        v_h = qkv[:, 2 * nqk + h * LANES:2 * nqk + (h + 1) * LANES].reshape(nb, tb, LANES)
        k_h = k_p[h].reshape(nb, tb, LANES)
        qh_ref[:, h] = (q_p[h] * qk_scale).reshape(nb, tb, LANES).astype(BF16)
        kh_ref[:, h] = k_h.astype(BF16)
        vh_ref[:, h] = v_h.astype(BF16)
        ak_ref[:, :, cols] = k_h
        av_ref[:, :, cols] = v_h


def _qkv_a_call(x, shift, scale, g, w, gsum, qg, kg, tables, heads, rows):
    b, t, d = x.shape
    nb, tb = _row_tiling(b, t, rows)
    nt = t // tb
    n = w.shape[1]
    xmap = lambda i: (i // nt, i % nt, 0)
    mmap = lambda i: (i // nt, 0, 0)
    tmap = lambda i: (i % nt, 0)
    hmap = lambda i: (i // nt, 0, i % nt, 0)
    head_shape = jax.ShapeDtypeStruct((b, heads, t, LANES), BF16)
    flat_shape = jax.ShapeDtypeStruct((b, t, heads * LANES), F32)
    return pl.pallas_call(
        functools.partial(_qkv_a_kernel, nb=nb, tb=tb, heads=heads),
        out_shape=(head_shape, head_shape, head_shape, flat_shape, flat_shape),
        grid=(b // nb * nt,),
        in_specs=[pl.BlockSpec((nb, tb, d), xmap),
                  pl.BlockSpec((nb, 1, d), mmap),
                  pl.BlockSpec((nb, 1, d), mmap),
                  _const_spec((1, d)),
                  _const_spec((d, n)),
                  _const_spec((MXU_DIM, MXU_DIM)),
                  _const_spec((1, heads * LANES)),
                  _const_spec((1, heads * LANES)),
                  pl.BlockSpec((tb, LANES), tmap),
                  pl.BlockSpec((tb, LANES), tmap),
                  pl.BlockSpec((tb, LANES), tmap)],
        out_specs=(pl.BlockSpec((nb, heads, tb, LANES), hmap),) * 3
                  + (pl.BlockSpec((nb, tb, heads * LANES), xmap),) * 2,
        compiler_params=_cparams("parallel"),
        name="mixer_a_qkv",
    )(x, shift, scale, g, w, gsum, qg, kg, *tables)


def _split_halves(q):
    lane = lax.broadcasted_iota(jnp.int32, q.shape, 1)
    zero = jnp.zeros_like(q)
    return jnp.concatenate([jnp.where(lane < HEAD_DIM, q, zero),
                            jnp.where(lane >= HEAD_DIM, q, zero)], axis=0)


def _diff_lambda(lam_ref, lam_init):
    lv = lam_ref[...]
    t1 = jnp.sum(lv[0:1] * lv[1:2], axis=1, keepdims=True)
    t2 = jnp.sum(lv[2:3] * lv[3:4], axis=1, keepdims=True)
    return jnp.exp(t1) - jnp.exp(t2) + lam_init


def _diff_finish(o0, o1, lam_ref, g_ref, lam_init):
    o = o0 - _diff_lambda(lam_ref, lam_init) * o1
    return _rms_scale(o) * g_ref[...] * (1.0 - lam_init)


def _diff_attn_kernel(q_ref, k_ref, v_ref, lam_ref, g_ref, o_ref, m_ref, l_ref, acc_ref,
                      *, tq, lam_init):
    qi = pl.program_id(2)
    qq = _split_halves(q_ref[0, 0])
    m_ref[...] = jnp.full(m_ref.shape, NEG_INF, F32)
    l_ref[...] = jnp.zeros(l_ref.shape, F32)
    acc_ref[...] = jnp.zeros(acc_ref.shape, F32)

    def step(start, mask):
        kj = k_ref[0, 0, pl.ds(start, tq), :]
        vj = v_ref[0, 0, pl.ds(start, tq), :]
        s = lax.dot_general(qq, kj, (((1,), (1,)), ((), ())), preferred_element_type=F32)
        if mask is not None:
            s = jnp.where(mask, s, NEG_INF)
        m_prev = m_ref[...]
        m_new = jnp.maximum(m_prev, jnp.max(s, axis=1, keepdims=True))
        alpha = jnp.exp(m_prev - m_new)
        p = jnp.concatenate([jnp.exp(s[:, c * LANES:(c + 1) * LANES] - m_new)
                             for c in range(tq // LANES)], axis=1)
        l_ref[...] = alpha * l_ref[...] + jnp.sum(p, axis=1, keepdims=True)
        acc_ref[...] = alpha * acc_ref[...] + jnp.dot(p.astype(BF16), vj, preferred_element_type=F32)
        m_ref[...] = m_new

    def body(j, carry):
        step(pl.multiple_of(j * tq, tq), None)
        return carry

    lax.fori_loop(0, qi, body, 0)
    row = lax.broadcasted_iota(jnp.int32, (2 * tq, tq), 0)
    col = lax.broadcasted_iota(jnp.int32, (2 * tq, tq), 1)
    step(pl.multiple_of(qi * tq, tq), (col // CHUNK) <= ((row % tq) // CHUNK))

    o = acc_ref[...] / l_ref[...]
    o_ref[0] = _diff_finish(o[:tq], o[tq:], lam_ref, g_ref, lam_init).astype(BF16)


def _diff_attn_call(qh, kh, vh, lam_vec, subln, lam_init, tq):
    b, heads, t, _ = qh.shape
    assert t % tq == 0 and tq % CHUNK == 0
    return pl.pallas_call(
        functools.partial(_diff_attn_kernel, tq=tq, lam_init=lam_init),
        out_shape=jax.ShapeDtypeStruct((b, t, heads * LANES), BF16),
        grid=(b, heads, t // tq),
        in_specs=[pl.BlockSpec((1, 1, tq, LANES), lambda bi, h, i: (bi, h, i, 0)),
                  pl.BlockSpec((1, 1, t, LANES), lambda bi, h, i: (bi, h, 0, 0)),
                  pl.BlockSpec((1, 1, t, LANES), lambda bi, h, i: (bi, h, 0, 0)),
                  pl.BlockSpec(lam_vec.shape, lambda bi, h, i: (0, 0)),
                  pl.BlockSpec((1, LANES), lambda bi, h, i: (0, 0))],
        out_specs=pl.BlockSpec((1, tq, LANES), lambda bi, h, i: (bi, i, h)),
        scratch_shapes=[pltpu.VMEM((2 * tq, LANES), F32)] * 3,
        compiler_params=_cparams("parallel", "parallel", "arbitrary"),
        name="mixer_a_attention",
    )(qh, kh, vh, lam_vec, subln)


def _pad_rows(a, rows):
    return jnp.concatenate([a, jnp.zeros((rows - a.shape[0], a.shape[1]), a.dtype)], axis=0)


def _diff_decode_kernel(q_ref, kp_ref, vp_ref, kn_ref, vn_ref, lam_ref, g_ref, o_ref,
                        *, past, lam_init):
    tn = q_ref.shape[2]
    qq = _split_halves(q_ref[0, 0])
    kp = kp_ref[0].astype(BF16)
    vp = vp_ref[0].astype(BF16)
    kn = _pad_rows(kn_ref[0, 0], LANES)
    vn = _pad_rows(vn_ref[0, 0], LANES)
    dims = (((1,), (1,)), ((), ()))
    s_p = lax.dot_general(qq, kp, dims, preferred_element_type=F32)
    s_n = lax.dot_general(qq, kn, dims, preferred_element_type=F32)

    def visible(shape, k_off, k_count):
        q_pos = past + lax.broadcasted_iota(jnp.int32, shape, 0) % tn
        col = lax.broadcasted_iota(jnp.int32, shape, 1)
        return ((k_off + col) // CHUNK <= q_pos // CHUNK) & (col < k_count)

    s_p = jnp.where(visible(s_p.shape, 0, past), s_p, NEG_INF)
    s_n = jnp.where(visible(s_n.shape, past, tn), s_n, NEG_INF)
    m = jnp.maximum(jnp.max(s_p, axis=1, keepdims=True), jnp.max(s_n, axis=1, keepdims=True))
    e_p = jnp.exp(s_p - m)
    e_n = jnp.exp(s_n - m)
    l = jnp.sum(e_p, axis=1, keepdims=True) + jnp.sum(e_n, axis=1, keepdims=True)
    acc = (jnp.dot(e_p.astype(BF16), vp, preferred_element_type=F32)
           + jnp.dot(e_n.astype(BF16), vn, preferred_element_type=F32))
    o = acc / l
    o_ref[0] = _diff_finish(o[:tn], o[tn:], lam_ref, g_ref, lam_init).astype(BF16)


def _diff_decode_call(qh, kh, vh, past_k, past_v, lam_vec, subln, lam_init):
    b, heads, tn, _ = qh.shape
    past = past_k.shape[1]
    new_spec = pl.BlockSpec((1, 1, tn, LANES), lambda bi, h: (bi, h, 0, 0))
    past_spec = pl.BlockSpec((1, past, LANES), lambda bi, h: (bi, 0, h))
    return pl.pallas_call(
        functools.partial(_diff_decode_kernel, past=past, lam_init=lam_init),
        out_shape=jax.ShapeDtypeStruct((b, tn, heads * LANES), BF16),
        grid=(b, heads),
        in_specs=[new_spec, past_spec, past_spec, new_spec, new_spec,
                  pl.BlockSpec(lam_vec.shape, lambda bi, h: (0, 0)),
                  pl.BlockSpec((1, LANES), lambda bi, h: (0, 0))],
        out_specs=pl.BlockSpec((1, tn, LANES), lambda bi, h: (bi, 0, h)),
        compiler_params=_cparams("parallel", "parallel"),
        name="mixer_a_decode_attention",
    )(qh, past_k, past_v, kh, vh, lam_vec, subln)


def _proj_res_kernel(o_ref, h_ref, gate_ref, w_ref, out_ref, *, nb, tb):
    k = o_ref.shape[-1]
    y = jnp.dot(o_ref[...].reshape(nb * tb, k), w_ref[...], preferred_element_type=F32)
    out_ref[...] = h_ref[...] + gate_ref[...] * y.reshape(nb, tb, y.shape[-1])


def _proj_res_call(o, h, gate, w, rows):
    b, t, d = h.shape
    k = o.shape[-1]
    nb, tb = _row_tiling(b, t, rows)
    nt = t // tb
    xmap = lambda i: (i // nt, i % nt, 0)
    return pl.pallas_call(
        functools.partial(_proj_res_kernel, nb=nb, tb=tb),
        out_shape=jax.ShapeDtypeStruct(h.shape, F32),
        grid=(b // nb * nt,),
        in_specs=[pl.BlockSpec((nb, tb, k), xmap),
                  pl.BlockSpec((nb, tb, d), xmap),
                  pl.BlockSpec((nb, 1, d), lambda i: (i // nt, 0, 0)),
                  _const_spec((k, d))],
        out_specs=pl.BlockSpec((nb, tb, d), xmap),
        compiler_params=_cparams("parallel"),
        name="out_proj_residual",
    )(o, h, gate, w)


def _ffn_kernel(h_ref, sh_ref, sc_ref, gate_ref, g_ref, wgu_ref, wd_ref, out_ref,
                *, nb, tb, ff, fc):
    h = h_ref[...]
    d = h.shape[-1]
    u = _modulate(h, g_ref[...], sh_ref[...], sc_ref[...]).reshape(nb * tb, d).astype(BF16)
    acc = jnp.zeros((nb * tb, d), F32)
    for c in range(ff // fc):
        g = jnp.dot(u, wgu_ref[:, c * fc:(c + 1) * fc], preferred_element_type=F32)
        up = jnp.dot(u, wgu_ref[:, ff + c * fc:ff + (c + 1) * fc], preferred_element_type=F32)
        a = ((g * _sigmoid(g)) * up).astype(BF16)
        acc = acc + jnp.dot(a, wd_ref[c * fc:(c + 1) * fc, :], preferred_element_type=F32)
    out_ref[...] = h + gate_ref[...] * acc.reshape(nb, tb, d)


def _ffn_call(h, shift, scale, gate, g, w_gu, w_d, rows):
    b, t, d = h.shape
    ff = w_d.shape[0]
    fc = ff // 2 if (ff // 2) % LANES == 0 else ff
    nb, tb = _row_tiling(b, t, rows)
    nt = t // tb
    xmap = lambda i: (i // nt, i % nt, 0)
    mmap = lambda i: (i // nt, 0, 0)
    return pl.pallas_call(
        functools.partial(_ffn_kernel, nb=nb, tb=tb, ff=ff, fc=fc),
        out_shape=jax.ShapeDtypeStruct(h.shape, F32),
        grid=(b // nb * nt,),
        in_specs=[pl.BlockSpec((nb, tb, d), xmap),
                  pl.BlockSpec((nb, 1, d), mmap),
                  pl.BlockSpec((nb, 1, d), mmap),
                  pl.BlockSpec((nb, 1, d), mmap),
                  _const_spec((1, d)),
                  _const_spec((d, 2 * ff)),
                  _const_spec((ff, d))],
        out_specs=pl.BlockSpec((nb, tb, d), xmap),
        compiler_params=_cparams("parallel"),
        name="dense_swiglu",
    )(h, shift, scale, gate, g, w_gu, w_d)


def _qkv_b_kernel(x_ref, shq_ref, scq_ref, gq_ref, shkv_ref, sckv_ref, gkv_ref, wq_ref, wkv_ref,
                  gsum_ref, qn_ref, kn_ref, cos_ref, s1_ref, s2_ref, q_ref, k_ref, v_ref,
                  *, nb, tb):
    x = x_ref[...]
    d = x.shape[-1]
    y = _rms_scale(x)
    uq = ((y * gq_ref[...]) * (1.0 + scq_ref[...]) + shq_ref[...]).reshape(nb * tb, d)
    ukv = ((y * gkv_ref[...]) * (1.0 + sckv_ref[...]) + shkv_ref[...]).reshape(nb * tb, d)
    q = jnp.dot(uq.astype(BF16), wq_ref[...], preferred_element_type=F32)
    kv = jnp.dot(ukv.astype(BF16), wkv_ref[...], preferred_element_type=F32)
    cos, s1, s2 = (_tile_rows(r[...], nb) for r in (cos_ref, s1_ref, s2_ref))
    q_p = _group_norm_rope(q, gsum_ref, qn_ref[...], cos, s1, s2)
    k_p = _group_norm_rope(kv[:, :LANES], gsum_ref, kn_ref[...], cos, s1, s2)
    qk_scale = HEAD_DIM ** -0.5
    for j, piece in enumerate(q_p):
        q_ref[:, :, j * LANES:(j + 1) * LANES] = (piece * qk_scale).reshape(nb, tb, LANES).astype(BF16)
    k_ref[...] = k_p[0].reshape(nb, tb, LANES)
    v_ref[...] = kv[:, LANES:].reshape(nb, tb, LANES)


def _qkv_b_call(x, mod_q, mod_kv, wq, wkv, gsum, qn, kn, tables, rows):
    b, t, d = x.shape
    nq = wq.shape[1]
    assert wkv.shape[1] == 2 * LANES
    nb, tb = _row_tiling(b, t, rows)
    nt = t // tb
    xmap = lambda i: (i // nt, i % nt, 0)
    mmap = lambda i: (i // nt, 0, 0)
    tmap = lambda i: (i % nt, 0)
    mspec = pl.BlockSpec((nb, 1, d), mmap)
    return pl.pallas_call(
        functools.partial(_qkv_b_kernel, nb=nb, tb=tb),
        out_shape=(jax.ShapeDtypeStruct((b, t, nq), BF16),
                   jax.ShapeDtypeStruct((b, t, LANES), F32),
                   jax.ShapeDtypeStruct((b, t, LANES), F32)),
        grid=(b // nb * nt,),
        in_specs=[pl.BlockSpec((nb, tb, d), xmap),
                  mspec, mspec, _const_spec((1, d)),
                  mspec, mspec, _const_spec((1, d)),
                  _const_spec((d, nq)),
                  _const_spec((d, 2 * LANES)),
                  _const_spec((MXU_DIM, MXU_DIM)),
                  _const_spec((1, nq)),
                  _const_spec((1, LANES)),
                  pl.BlockSpec((tb, LANES), tmap),
                  pl.BlockSpec((tb, LANES), tmap),
                  pl.BlockSpec((tb, LANES), tmap)],
        out_specs=(pl.BlockSpec((nb, tb, nq), xmap),
                   pl.BlockSpec((nb, tb, LANES), xmap),
                   pl.BlockSpec((nb, tb, LANES), xmap)),
        compiler_params=_cparams("parallel"),
        name="mixer_b_qkv",
    )(x, mod_q[0], mod_q[1], mod_q[2], mod_kv[0], mod_kv[1], mod_kv[2], wq, wkv, gsum, qn, kn,
      *tables)


def _tile_kv_head(a, n, groups):
    lane = lax.broadcasted_iota(jnp.int32, a.shape, 1)
    swapped = pltpu.roll(a, HEAD_DIM, 1)
    keep = (lane < HEAD_DIM) if n == 0 else (lane >= HEAD_DIM)
    pair = jnp.where(keep, a, swapped).astype(BF16)
    return jnp.concatenate([pair] * (groups // 2), axis=1)


def _sink_group_attention(qs, kt, vt, mask, sink_col, groups):
    rows, width = qs.shape
    head_of_lane = lax.broadcasted_iota(jnp.int32, (rows, width), 1) // HEAD_DIM
    zero = jnp.zeros_like(qs)
    lhs = jnp.concatenate([jnp.where(head_of_lane == h, qs, zero) for h in range(groups)], axis=0)
    s = lax.dot_general(lhs, kt, (((1,), (1,)), ((), ())), preferred_element_type=F32)
    s = jnp.where(mask, s, NEG_INF)
    m = jnp.maximum(jnp.max(s, axis=1, keepdims=True), sink_col)
    e = jnp.exp(s - m)
    den = jnp.sum(e, axis=1, keepdims=True) + jnp.exp(sink_col - m)
    ob = jnp.dot((e / den).astype(BF16), vt, preferred_element_type=F32)
    out = jnp.zeros((rows, width), F32)
    for h in range(groups):
        out = out + jnp.where(head_of_lane == h, ob[h * rows:(h + 1) * rows], 0.0)
    return out


def _window_attn_kernel(q_ref, kprev_ref, kcur_ref, vprev_ref, vcur_ref, sink_ref, o_ref,
                        *, tq, groups, kv_heads):
    band = (WIN_CHUNKS + 2) * CHUNK
    q = q_ref[0]
    kc = jnp.concatenate([kprev_ref[0], kcur_ref[0]], axis=0)
    vc = jnp.concatenate([vprev_ref[0], vcur_ref[0]], axis=0)
    prev = kprev_ref.shape[1]
    first_chunk = pl.program_id(1) * (tq // CHUNK)
    width = groups * HEAD_DIM
    rows = groups * CHUNK
    col_chunk = lax.broadcasted_iota(jnp.int32, (rows, band), 1) // CHUNK
    outs = [[] for _ in range(tq // CHUNK)]
    for n in range(kv_heads):
        kt = _tile_kv_head(kc, n, groups)
        vt = _tile_kv_head(vc, n, groups)
        for c in range(tq // CHUNK):
            lo = prev + (c + 1) * CHUNK - band
            key_chunk = col_chunk + (first_chunk + c - (WIN_CHUNKS + 1))
            mask = (col_chunk >= 1) & (key_chunk >= 0)
            qs = q[c * CHUNK:(c + 1) * CHUNK, n * width:(n + 1) * width]
            outs[c].append(_sink_group_attention(qs, kt[lo:lo + band], vt[lo:lo + band], mask,
                                                 sink_ref[n], groups))
    for c in range(tq // CHUNK):
        o_ref[0, c * CHUNK:(c + 1) * CHUNK, :] = jnp.concatenate(outs[c], axis=1).astype(BF16)


def _window_attn_call(q, k, v, sink_cols, tq):
    b, t, nq = q.shape
    kv_heads = sink_cols.shape[0]
    groups = nq // (kv_heads * HEAD_DIM)
    prev = (WIN_CHUNKS + 2) * CHUNK
    assert t % tq == 0 and tq % prev == 0
    ratio = tq // prev
    cur_spec = pl.BlockSpec((1, tq, LANES), lambda bi, i: (bi, i, 0))
    prev_spec = pl.BlockSpec((1, prev, LANES), lambda bi, i: (bi, jnp.maximum(i * ratio - 1, 0), 0))
    return pl.pallas_call(
        functools.partial(_window_attn_kernel, tq=tq, groups=groups, kv_heads=kv_heads),
        out_shape=jax.ShapeDtypeStruct((b, t, nq), BF16),
        grid=(b, t // tq),
        in_specs=[pl.BlockSpec((1, tq, nq), lambda bi, i: (bi, i, 0)),
                  prev_spec, cur_spec, prev_spec, cur_spec,
                  pl.BlockSpec(sink_cols.shape, lambda bi, i: (0, 0, 0))],
        out_specs=pl.BlockSpec((1, tq, nq), lambda bi, i: (bi, i, 0)),
        compiler_params=_cparams("parallel", "parallel"),
        name="mixer_b_window_attention",
    )(q, k, k, v, v, sink_cols)


def _window_decode_kernel(q_ref, kp_ref, kn_ref, vp_ref, vn_ref, sink_ref, o_ref,
                          *, past_len, groups, kv_heads):
    q = q_ref[0]
    tn = q.shape[0]
    win = kp_ref.shape[1]
    keys = 2 * LANES
    kc = _pad_rows(jnp.concatenate([kp_ref[0], kn_ref[0]], axis=0), keys)
    vc = _pad_rows(jnp.concatenate([vp_ref[0], vn_ref[0]], axis=0), keys)
    width = groups * HEAD_DIM
    rows = groups * tn
    shape = (rows, keys)
    col = lax.broadcasted_iota(jnp.int32, shape, 1)
    q_chunk = (past_len + lax.broadcasted_iota(jnp.int32, shape, 0) % tn) // CHUNK
    k_pos = past_len - win + col
    k_chunk = k_pos // CHUNK
    mask = ((k_chunk <= q_chunk) & (k_chunk >= q_chunk - WIN_CHUNKS) & (k_pos >= 0)
            & (col < win + tn))
    outs = []
    for n in range(kv_heads):
        outs.append(_sink_group_attention(q[:, n * width:(n + 1) * width],
                                          _tile_kv_head(kc, n, groups), _tile_kv_head(vc, n, groups),
                                          mask, sink_ref[n], groups))
    o_ref[0] = jnp.concatenate(outs, axis=1).astype(BF16)


def _window_decode_call(q, k_new, v_new, past_k, past_v, sink_cols, past_len):
    b, tn, nq = q.shape
    win = past_k.shape[1]
    kv_heads = sink_cols.shape[0]
    groups = nq // (kv_heads * HEAD_DIM)
    assert win + tn <= 2 * LANES
    new_spec = pl.BlockSpec((1, tn, LANES), lambda bi: (bi, 0, 0))
    past_spec = pl.BlockSpec((1, win, LANES), lambda bi: (bi, 0, 0))
    return pl.pallas_call(
        functools.partial(_window_decode_kernel, past_len=past_len, groups=groups, kv_heads=kv_heads),
        out_shape=jax.ShapeDtypeStruct((b, tn, nq), BF16),
        grid=(b,),
        in_specs=[pl.BlockSpec((1, tn, nq), lambda bi: (bi, 0, 0)),
                  past_spec, new_spec, past_spec, new_spec,
                  pl.BlockSpec(sink_cols.shape, lambda bi: (0, 0, 0))],
        out_specs=pl.BlockSpec((1, tn, nq), lambda bi: (bi, 0, 0)),
        compiler_params=_cparams("parallel"),
        name="mixer_b_decode_attention",
    )(q, past_k, k_new, past_v, v_new, sink_cols)


def _route_top2(logits, experts):
    lane = lax.broadcasted_iota(jnp.int32, logits.shape, 1).astype(F32)
    big = float(LANES)
    lg = jnp.where(lane < experts, logits, NEG_INF)
    m1 = jnp.max(lg, axis=1, keepdims=True)
    i1 = jnp.min(jnp.where(lg == m1, lane, big), axis=1, keepdims=True)
    lg2 = jnp.where(lane == i1, NEG_INF, lg)
    m2 = jnp.max(lg2, axis=1, keepdims=True)
    i2 = jnp.min(jnp.where(lg2 == m2, lane, big), axis=1, keepdims=True)
    e2 = jnp.exp(m2 - m1)
    den = 1.0 + e2
    return jnp.where(lane == i1, 1.0 / den, 0.0) + jnp.where(lane == i2, e2 / den, 0.0)


def _moe_kernel(h_ref, sh_ref, sc_ref, gate_ref, g_ref, rhi_ref, rlo_ref, wg_ref, wu_ref, wd_ref,
                out_ref, u_s, dg_s, acc_s, *, nb, tb, experts):
    e = pl.program_id(1)
    k = pl.program_id(2)
    d = h_ref.shape[-1]

    @pl.when((e == 0) & (k == 0))
    def _():
        u = _modulate(h_ref[...], g_ref[...], sh_ref[...], sc_ref[...]).reshape(nb * tb, d)
        ub = u.astype(BF16)
        ulo = (u - ub.astype(F32)).astype(BF16)
        rhi = rhi_ref[...]
        logits = (jnp.dot(ub, rhi, preferred_element_type=F32)
                  + jnp.dot(ub, rlo_ref[...], preferred_element_type=F32)
                  + jnp.dot(ulo, rhi, preferred_element_type=F32))
        u_s[...] = ub
        dg_s[...] = _route_top2(logits, experts)
        acc_s[...] = jnp.zeros(acc_s.shape, F32)

    u = u_s[...]
    g = jnp.dot(u, wg_ref[0], preferred_element_type=F32)
    up = jnp.dot(u, wu_ref[0], preferred_element_type=F32)
    dg = dg_s[...]
    lane = lax.broadcasted_iota(jnp.int32, dg.shape, 1)
    gate_e = jnp.sum(jnp.where(lane == e, dg, 0.0), axis=1, keepdims=True)
    a = ((g * _sigmoid(g)) * up * gate_e).astype(BF16)
    acc_s[...] += jnp.dot(a, wd_ref[0], preferred_element_type=F32)

    @pl.when((e == pl.num_programs(1) - 1) & (k == pl.num_programs(2) - 1))
    def _():
        out_ref[...] = h_ref[...] + gate_ref[...] * acc_s[...].reshape(nb, tb, d)


def _moe_call(h, shift, scale, gate, g, r_hi, r_lo, w_gu, w_d, experts, rows, fc):
    b, t, d = h.shape
    ff = w_d.shape[1]
    assert ff % fc == 0
    nk = ff // fc
    nb, tb = _row_tiling(b, t, rows)
    nt = t // tb
    tm = nb * tb
    xmap = lambda i, e, k: (i // nt, i % nt, 0)
    mmap = lambda i, e, k: (i // nt, 0, 0)
    mspec = pl.BlockSpec((nb, 1, d), mmap)
    return pl.pallas_call(
        functools.partial(_moe_kernel, nb=nb, tb=tb, experts=experts),
        out_shape=jax.ShapeDtypeStruct(h.shape, F32),
        grid=(b // nb * nt, experts, nk),
        in_specs=[pl.BlockSpec((nb, tb, d), xmap),
                  mspec, mspec, mspec,
                  _const_spec((1, d)),
                  _const_spec((d, LANES)),
                  _const_spec((d, LANES)),
                  pl.BlockSpec((1, d, fc), lambda i, e, k: (e, 0, k)),
                  pl.BlockSpec((1, d, fc), lambda i, e, k: (e, 0, k + nk)),
                  pl.BlockSpec((1, fc, d), lambda i, e, k: (e, k, 0))],
        out_specs=pl.BlockSpec((nb, tb, d), xmap),
        scratch_shapes=[pltpu.VMEM((tm, d), BF16),
                        pltpu.VMEM((tm, LANES), F32),
                        pltpu.VMEM((tm, d), F32)],
        compiler_params=_cparams("parallel", "arbitrary", "arbitrary"),
        name="expert_swiglu",
    )(h, shift, scale, gate, g, r_hi, r_lo, w_gu, w_gu, w_d)


def _rope_tables(pos):
    half = ROT_DIM // 2
    inv = ROPE_THETA ** (-(jnp.arange(half, dtype=F32) * 2.0 / ROT_DIM))
    ang = pos.astype(F32)[:, None] * inv[None, :]
    cos, sin = jnp.cos(ang), jnp.sin(ang)
    t = pos.shape[0]
    zeros = lambda n: jnp.zeros((t, n), F32)
    c = jnp.concatenate([cos, cos, jnp.ones((t, HEAD_DIM - ROT_DIM), F32)], axis=1)
    s1 = jnp.concatenate([-sin, zeros(HEAD_DIM - half)], axis=1)
    s2 = jnp.concatenate([zeros(half), sin, zeros(HEAD_DIM - ROT_DIM)], axis=1)
    rep = LANES // HEAD_DIM
    return tuple(jnp.tile(a, (1, rep)) for a in (c, s1, s2))


def _mod_rows(mod, lo, hi):
    d = mod.shape[1]
    return mod[lo:hi].reshape(hi - lo, 1, d)


def _trunk(x, mods, kv_mods, w, past, rows):
    b, t, d = x.shape
    past_len = 0 if past is None else past[0].shape[2]
    pos = past_len + jnp.arange(t, dtype=jnp.int32)
    tables = _rope_tables(pos)
    heads_a = w["a_heads"]
    big = t >= rows

    m0 = mods[0]
    lam_init = 0.8 - 0.6 * math.exp(-0.3 * 0)
    qh, kh, vh, a_k, a_v = _qkv_a_call(x, m0[0], m0[1], w["norm_g"][0][0], w["a_w_qkv"], w["gsum"],
                                       w["a_q_gain"], w["a_k_gain"], tables, heads_a, rows)
    if past is None:
        o = _diff_attn_call(qh, kh, vh, w["a_lambda"], w["a_subln"], lam_init, 256)
    else:
        pk = past[0][0].reshape(b, past_len, heads_a * LANES)
        pv = past[1][0].reshape(b, past_len, heads_a * LANES)
        o = _diff_decode_call(qh, kh, vh, pk, pv, w["a_lambda"], w["a_subln"], lam_init)
    h = _proj_res_call(o, x, m0[2], w["a_w_o"], rows)
    h = _ffn_call(h, m0[3], m0[4], m0[5], w["norm_g"][0][1], w["f_w_gu"], w["f_w_d"], rows)

    m1 = mods[1]
    q, k_sh, v_sh = _qkv_b_call(h, (m1[0], m1[1], w["norm_g"][1][0]),
                                (kv_mods[0], kv_mods[1], w["kv_norm_g"]),
                                w["b_w_q"], w["kv_w"], w["gsum"], w["b_q_gain"], w["kv_k_gain"],
                                tables, rows)
    if past is None:
        o = _window_attn_call(q, k_sh, v_sh, w["sink_cols_prompt"], 256)
    else:
        win = past[2].shape[1]
        o = _window_decode_call(q, k_sh, v_sh, past[2].reshape(b, win, LANES),
                                past[3].reshape(b, win, LANES), w["sink_cols_decode"], past_len)
    h = _proj_res_call(o, h, m1[2], w["b_w_o"], rows)
    h = _moe_call(h, m1[3], m1[4], m1[5], w["norm_g"][1][1], w["r_hi"], w["r_lo"],
                  w["m_w_gu"], w["m_w_d"], w["experts"], 1024 if big else rows, 512)
    return h, a_k, a_v, k_sh, v_sh


def kernel(x_prompt, x_sample, c_prompt, c_sample, cache_a_k, cache_a_v, cache_b_k, cache_b_v,
           w_mod, b_mod, norm_g, a_w_qkv, a_q_norm, a_k_norm, a_lambda, a_subln, a_w_o,
           kv_norm_g, kv_w_mod, kv_b_mod, kv_w, kv_k_norm, b_w_q, b_q_norm, b_sinks, b_w_o,
           f_w_gu, f_w_d, m_router, m_w_gu, m_w_d):
    bp, t, d = x_prompt.shape
    bs, tn, _ = x_sample.shape
    assert w_mod.shape[0] == 2 and a_w_qkv.shape[0] == 1 and b_w_q.shape[0] == 1
    heads_a = cache_a_k.shape[3]
    kv_heads = cache_b_k.shape[2]
    heads_b = b_sinks.shape[1]
    groups = heads_b // kv_heads
    experts = m_router.shape[2]

    c_all = jnp.concatenate([c_prompt, c_sample], axis=0)
    mods = [_mod_call(c_all, w_mod[l], b_mod[l]) for l in range(2)]
    kv_mod = _mod_call(c_all, kv_w_mod, kv_b_mod)

    def split(mod, n, lo, hi):
        m = mod[lo:hi].reshape(hi - lo, n, 1, d)
        return [m[:, j] for j in range(n)]

    group = jnp.arange(MXU_DIM) // HEAD_DIM
    router = jnp.pad(m_router[0], ((0, 0), (0, LANES - experts)))
    r_hi = router.astype(BF16)

    def sink_cols(rows_per_head):
        s = b_sinks[0].astype(F32).reshape(kv_heads, groups, 1)
        return jnp.broadcast_to(s, (kv_heads, groups, rows_per_head)).reshape(
            kv_heads, groups * rows_per_head, 1)

    w = dict(
        a_heads=heads_a, experts=experts,
        norm_g=[[norm_g[l, j].reshape(1, d) for j in range(2)] for l in range(2)],
        gsum=(group[:, None] == group[None, :]).astype(BF16),
        a_w_qkv=a_w_qkv[0].astype(BF16),
        a_q_gain=jnp.tile(a_q_norm[0].reshape(1, LANES), (1, heads_a)),
        a_k_gain=jnp.tile(a_k_norm[0].reshape(1, LANES), (1, heads_a)),
        a_lambda=a_lambda[0].astype(F32),
        a_subln=a_subln[0].reshape(1, LANES),
        a_w_o=a_w_o[0].astype(BF16),
        f_w_gu=f_w_gu[0].astype(BF16), f_w_d=f_w_d[0].astype(BF16),
        kv_norm_g=kv_norm_g.reshape(1, d),
        kv_w=kv_w.astype(BF16),
        kv_k_gain=jnp.tile(kv_k_norm.reshape(1, HEAD_DIM), (1, LANES // HEAD_DIM)),
        b_w_q=b_w_q[0].astype(BF16),
        b_q_gain=jnp.tile(b_q_norm[0].reshape(1, HEAD_DIM), (1, heads_b)),
        b_w_o=b_w_o[0].astype(BF16),
        sink_cols_prompt=sink_cols(CHUNK), sink_cols_decode=sink_cols(tn),
        r_hi=r_hi, r_lo=(router - r_hi.astype(F32)).astype(BF16),
        m_w_gu=m_w_gu[0].astype(BF16), m_w_d=m_w_d[0].astype(BF16),
    )

    rows = 512
    y_p, ak_p, av_p, bk_p, bv_p = _trunk(
        x_prompt, [split(m, 6, 0, bp) for m in mods], split(kv_mod, 2, 0, bp), w, None, rows)
    y_s, ak_s, av_s, bk_s, bv_s = _trunk(
        x_sample, [split(m, 6, bp, bp + bs) for m in mods], split(kv_mod, 2, bp, bp + bs), w,
        (cache_a_k, cache_a_v, cache_b_k, cache_b_v), rows)

    win = min(WIN_CHUNKS * CHUNK, t)
    a_k_shape = lambda b_, t_: (1, b_, t_, heads_a, 2, HEAD_DIM)
    a_v_shape = lambda b_, t_: (1, b_, t_, heads_a, 2 * HEAD_DIM)
    b_shape = lambda b_, t_: (b_, t_, kv_heads, HEAD_DIM)
    return (y_p, y_s,
            ak_p.reshape(a_k_shape(bp, t)), av_p.reshape(a_v_shape(bp, t)),
            bk_p[:, -win:].reshape(b_shape(bp, win)), bv_p[:, -win:].reshape(b_shape(bp, win)),
            ak_s.reshape(a_k_shape(bs, tn)), av_s.reshape(a_v_shape(bs, tn)),
            bk_s.reshape(b_shape(bs, tn)), bv_s.reshape(b_shape(bs, tn)))
```

```python
import functools
import math

import jax
import jax.numpy as jnp
from jax import lax
from jax.experimental import pallas as pl
from jax.experimental.pallas import tpu as pltpu

CHUNK = 64
WIN_CHUNKS = 2
ROT_DIM = 16
ROPE_THETA = 500000.0
EPS = 1e-6
HEAD_DIM = 64
N_ROUTED = 2

LANES = 128
MXU_DIM = 256
VMEM_LIMIT_BYTES = 56 * 1024 * 1024

F32 = jnp.float32
BF16 = jnp.bfloat16
NEG_INF = float("-inf")


def _cparams(*sem):
    return pltpu.CompilerParams(dimension_semantics=sem, vmem_limit_bytes=VMEM_LIMIT_BYTES)


def _row_tiling(batch, seq, rows):
    if seq >= rows:
        assert seq % rows == 0
        return 1, rows
    nb = max(1, min(batch, rows // seq))
    while batch % nb:
        nb -= 1
    return nb, seq


def _const_spec(shape):
    zeros = (0,) * len(shape)
    return pl.BlockSpec(shape, lambda *_: zeros, pipeline_mode=pl.Buffered(1))


def _sigmoid(x):
    return 1.0 / (1.0 + jnp.exp(-x))


def _rms_scale(x):
    return x * lax.rsqrt(jnp.mean(x * x, axis=-1, keepdims=True) + EPS)


def _modulate(x, g, shift, scale):
    return (_rms_scale(x) * g) * (1.0 + scale) + shift


def _group_norm_rope(x, gsum_ref, gain, cos, s1, s2):
    width = x.shape[1]
    slab = MXU_DIM if width % MXU_DIM == 0 else LANES
    gsum = gsum_ref[:slab, :slab]
    pieces = []
    for j in range(width // slab):
        xs = x[:, j * slab:(j + 1) * slab]
        ss = jnp.dot((xs * xs).astype(BF16), gsum, preferred_element_type=F32)
        y = xs * lax.rsqrt(ss * (1.0 / HEAD_DIM) + EPS) * gain[:, j * slab:(j + 1) * slab]
        for t in range(slab // LANES):
            yt = y[:, t * LANES:(t + 1) * LANES]
            pieces.append(yt * cos + pltpu.roll(yt, LANES - ROT_DIM // 2, 1) * s1
                          + pltpu.roll(yt, ROT_DIM // 2, 1) * s2)
    return pieces


def _tile_rows(t, nb):
    return t if nb == 1 else jnp.concatenate([t] * nb, axis=0)


def _mod_kernel(c_ref, w_ref, b_ref, o_ref):
    c = c_ref[...]
    sc = (c * _sigmoid(c)).astype(BF16)
    o_ref[...] = jnp.dot(sc, w_ref[...].astype(BF16), preferred_element_type=F32) + b_ref[...]


def _mod_call(c, w, b):
    rows, d = c.shape
    n = w.shape[1]
    tn = 2048
    assert n % tn == 0
    return pl.pallas_call(
        _mod_kernel,
        out_shape=jax.ShapeDtypeStruct((rows, n), F32),
        grid=(n // tn,),
        in_specs=[pl.BlockSpec((rows, d), lambda j: (0, 0)),
                  pl.BlockSpec((d, tn), lambda j: (0, j)),
                  pl.BlockSpec((1, tn), lambda j: (0, j))],
        out_specs=pl.BlockSpec((rows, tn), lambda j: (0, j)),
        compiler_params=_cparams("parallel"),
        name="adaln_mod",
    )(c, w, b.reshape(1, n))


def _qkv_a_kernel(x_ref, sh_ref, sc_ref, g_ref, w_ref, gsum_ref, qg_ref, kg_ref,
                  cos_ref, s1_ref, s2_ref, qh_ref, kh_ref, vh_ref, ak_ref, av_ref,
                  *, nb, tb, heads):
    x = x_ref[...]
    d = x.shape[-1]
    u = _modulate(x, g_ref[...], sh_ref[...], sc_ref[...]).reshape(nb * tb, d).astype(BF16)
    qkv = jnp.dot(u, w_ref[...], preferred_element_type=F32)
    nqk = heads * 2 * HEAD_DIM
    cos, s1, s2 = (_tile_rows(r[...], nb) for r in (cos_ref, s1_ref, s2_ref))
    q_p = _group_norm_rope(qkv[:, :nqk], gsum_ref, qg_ref[...], cos, s1, s2)
    k_p = _group_norm_rope(qkv[:, nqk:2 * nqk], gsum_ref, kg_ref[...], cos, s1, s2)
    qk_scale = HEAD_DIM ** -0.5 * math.log2(math.e)
    for h in range(heads):
        cols = slice(h * LANES, (h + 1) * LANES)
        v_h = qkv[:, 2 * nqk + h * LANES:2 * nqk + (h + 1) * LANES].reshape(nb, tb, LANES)
        k_h = k_p[h].reshape(nb, tb, LANES)
        qh_ref[:, h] = (q_p[h] * qk_scale).reshape(nb, tb, LANES).astype(BF16)
        kh_ref[:, h] = k_h.astype(BF16)
        vh_ref[:, h] = v_h.astype(BF16)
        ak_ref[:, :, cols] = k_h
        av_ref[:, :, cols] = v_h


def _qkv_a_call(x, shift, scale, g, w, gsum, qg, kg, tables, heads, rows):
    b, t, d = x.shape
    nb, tb = _row_tiling(b, t, rows)
    nt = t // tb
    n = w.shape[1]
    xmap = lambda i: (i // nt, i % nt, 0)
    mmap = lambda i: (i // nt, 0, 0)
    tmap = lambda i: (i % nt, 0)
    hmap = lambda i: (i // nt, 0, i % nt, 0)
    head_shape = jax.ShapeDtypeStruct((b, heads, t, LANES), BF16)
    flat_shape = jax.ShapeDtypeStruct((b, t, heads * LANES), F32)
    return pl.pallas_call(
        functools.partial(_qkv_a_kernel, nb=nb, tb=tb, heads=heads),
        out_shape=(head_shape, head_shape, head_shape, flat_shape, flat_shape),
        grid=(b // nb * nt,),
        in_specs=[pl.BlockSpec((nb, tb, d), xmap),
                  pl.BlockSpec((nb, 1, d), mmap),
                  pl.BlockSpec((nb, 1, d), mmap),
                  _const_spec((1, d)),
                  _const_spec((d, n)),
                  _const_spec((MXU_DIM, MXU_DIM)),
                  _const_spec((1, heads * LANES)),
                  _const_spec((1, heads * LANES)),
                  pl.BlockSpec((tb, LANES), tmap),
                  pl.BlockSpec((tb, LANES), tmap),
                  pl.BlockSpec((tb, LANES), tmap)],
        out_specs=(pl.BlockSpec((nb, heads, tb, LANES), hmap),) * 3
                  + (pl.BlockSpec((nb, tb, heads * LANES), xmap),) * 2,
        compiler_params=_cparams("parallel"),
        name="mixer_a_qkv",
    )(x, shift, scale, g, w, gsum, qg, kg, *tables)


def _split_halves(q):
    lane = lax.broadcasted_iota(jnp.int32, q.shape, 1)
    zero = jnp.zeros_like(q)
    return jnp.concatenate([jnp.where(lane < HEAD_DIM, q, zero),
                            jnp.where(lane >= HEAD_DIM, q, zero)], axis=0)


def _diff_lambda(lam_ref, lam_init):
    lv = lam_ref[...]
    t1 = jnp.sum(lv[0:1] * lv[1:2], axis=1, keepdims=True)
    t2 = jnp.sum(lv[2:3] * lv[3:4], axis=1, keepdims=True)
    return jnp.exp(t1) - jnp.exp(t2) + lam_init


def _diff_finish(o0, o1, lam_ref, g_ref, lam_init):
    o = o0 - _diff_lambda(lam_ref, lam_init) * o1
    return _rms_scale(o) * g_ref[...] * (1.0 - lam_init)


def _diff_attn_kernel(q_ref, k_ref, v_ref, lam_ref, g_ref, o_ref, s_ref, m_ref, l_ref, acc_ref,
                      *, tq, tk, hp, lam_init):
    qi = pl.program_id(2)
    n_full = (qi * tq) // tk
    qqs = [_split_halves(q_ref[0, h]) for h in range(hp)]
    m_ref[...] = jnp.full(m_ref.shape, NEG_INF, F32)
    l_ref[...] = jnp.zeros(l_ref.shape, F32)
    acc_ref[...] = jnp.zeros(acc_ref.shape, F32)

    def scores(h, j):
        kj = k_ref[0, h, pl.ds(pl.multiple_of(j * tk, tk), tk), :]
        return lax.dot_general(qqs[h], kj, (((1,), (1,)), ((), ())), preferred_element_type=F32)

    def absorb(h, j, s):
        vj = v_ref[0, h, pl.ds(pl.multiple_of(j * tk, tk), tk), :]
        m_prev = m_ref[h]
        m_new = jnp.maximum(m_prev, jnp.max(s, axis=1, keepdims=True))
        alpha = jnp.exp2(m_prev - m_new)
        chunks = [jnp.exp2(s[:, c * LANES:(c + 1) * LANES] - m_new) for c in range(tk // LANES)]
        l_ref[h] = alpha * l_ref[h] + functools.reduce(lambda a, b: a + b, chunks)
        p = jnp.concatenate(chunks, axis=1).astype(BF16)
        acc_ref[h] = alpha * acc_ref[h] + jnp.dot(p, vj, preferred_element_type=F32)
        m_ref[h] = m_new

    for h in range(hp):
        s_ref[h] = scores(h, 0)

    def body(j, carry):
        for h in range(hp):
            s = s_ref[h]
            s_ref[h] = scores(h, j + 1)
            absorb(h, j, s)
        return carry

    lax.fori_loop(0, n_full, body, 0)
    q_chunk = (qi * tq + lax.broadcasted_iota(jnp.int32, (2 * tq, tk), 0) % tq) // CHUNK
    k_chunk = (n_full * tk + lax.broadcasted_iota(jnp.int32, (2 * tq, tk), 1)) // CHUNK
    visible = k_chunk <= q_chunk
    for h in range(hp):
        absorb(h, n_full, jnp.where(visible, s_ref[h], NEG_INF))

    for h in range(hp):
        o = acc_ref[h] / jnp.sum(l_ref[h], axis=1, keepdims=True)
        o_ref[0, :, h * LANES:(h + 1) * LANES] = _diff_finish(
            o[:tq], o[tq:], lam_ref, g_ref, lam_init).astype(BF16)


def _diff_attn_call(qh, kh, vh, lam_vec, subln, lam_init, tq, tk, hp):
    b, heads, t, _ = qh.shape
    assert t % tk == 0 and tk % tq == 0 and tq % CHUNK == 0 and heads % hp == 0
    return pl.pallas_call(
        functools.partial(_diff_attn_kernel, tq=tq, tk=tk, hp=hp, lam_init=lam_init),
        out_shape=jax.ShapeDtypeStruct((b, t, heads * LANES), BF16),
        grid=(b, heads // hp, t // tq),
        in_specs=[pl.BlockSpec((1, hp, tq, LANES), lambda bi, h, i: (bi, h, i, 0)),
                  pl.BlockSpec((1, hp, t, LANES), lambda bi, h, i: (bi, h, 0, 0)),
                  pl.BlockSpec((1, hp, t, LANES), lambda bi, h, i: (bi, h, 0, 0)),
                  pl.BlockSpec(lam_vec.shape, lambda bi, h, i: (0, 0)),
                  pl.BlockSpec((1, LANES), lambda bi, h, i: (0, 0))],
        out_specs=pl.BlockSpec((1, tq, hp * LANES), lambda bi, h, i: (bi, i, h)),
        scratch_shapes=[pltpu.VMEM((hp, 2 * tq, tk), F32)] + [pltpu.VMEM((hp, 2 * tq, LANES), F32)] * 3,
        compiler_params=_cparams("parallel", "parallel", "arbitrary"),
        name="mixer_a_attention",
    )(qh, kh, vh, lam_vec, subln)


def _pad_rows(a, rows):
    return jnp.concatenate([a, jnp.zeros((rows - a.shape[0], a.shape[1]), a.dtype)], axis=0)


def _diff_decode_kernel(q_ref, kp_ref, vp_ref, kn_ref, vn_ref, lam_ref, g_ref, o_ref,
                        *, past, lam_init):
    tn = q_ref.shape[2]
    qq = _split_halves(q_ref[0, 0])
    kp = kp_ref[0].astype(BF16)
    vp = vp_ref[0].astype(BF16)
    kn = _pad_rows(kn_ref[0, 0], LANES)
    vn = _pad_rows(vn_ref[0, 0], LANES)
    dims = (((1,), (1,)), ((), ()))
    s_p = lax.dot_general(qq, kp, dims, preferred_element_type=F32)
    s_n = lax.dot_general(qq, kn, dims, preferred_element_type=F32)

    def visible(shape, k_off, k_count):
        q_pos = past + lax.broadcasted_iota(jnp.int32, shape, 0) % tn
        col = lax.broadcasted_iota(jnp.int32, shape, 1)
        return ((k_off + col) // CHUNK <= q_pos // CHUNK) & (col < k_count)

    s_p = jnp.where(visible(s_p.shape, 0, past), s_p, NEG_INF)
    s_n = jnp.where(visible(s_n.shape, past, tn), s_n, NEG_INF)
    m = jnp.maximum(jnp.max(s_p, axis=1, keepdims=True), jnp.max(s_n, axis=1, keepdims=True))
    e_p = jnp.exp2(s_p - m)
    e_n = jnp.exp2(s_n - m)
    l = jnp.sum(e_p, axis=1, keepdims=True) + jnp.sum(e_n, axis=1, keepdims=True)
    acc = (jnp.dot(e_p.astype(BF16), vp, preferred_element_type=F32)
           + jnp.dot(e_n.astype(BF16), vn, preferred_element_type=F32))
    o = acc / l
    o_ref[0] = _diff_finish(o[:tn], o[tn:], lam_ref, g_ref, lam_init).astype(BF16)


def _diff_decode_call(qh, kh, vh, past_k, past_v, lam_vec, subln, lam_init):
    b, heads, tn, _ = qh.shape
    past = past_k.shape[1]
    new_spec = pl.BlockSpec((1, 1, tn, LANES), lambda bi, h: (bi, h, 0, 0))
    past_spec = pl.BlockSpec((1, past, LANES), lambda bi, h: (bi, 0, h))
    return pl.pallas_call(
        functools.partial(_diff_decode_kernel, past=past, lam_init=lam_init),
        out_shape=jax.ShapeDtypeStruct((b, tn, heads * LANES), BF16),
        grid=(b, heads),
        in_specs=[new_spec, past_spec, past_spec, new_spec, new_spec,
                  pl.BlockSpec(lam_vec.shape, lambda bi, h: (0, 0)),
                  pl.BlockSpec((1, LANES), lambda bi, h: (0, 0))],
        out_specs=pl.BlockSpec((1, tn, LANES), lambda bi, h: (bi, 0, h)),
        compiler_params=_cparams("parallel", "parallel"),
        name="mixer_a_decode_attention",
    )(qh, past_k, past_v, kh, vh, lam_vec, subln)


def _proj_res_kernel(o_ref, h_ref, gate_ref, w_ref, out_ref, *, nb, tb):
    k = o_ref.shape[-1]
    y = jnp.dot(o_ref[...].reshape(nb * tb, k), w_ref[...], preferred_element_type=F32)
    out_ref[...] = h_ref[...] + gate_ref[...] * y.reshape(nb, tb, y.shape[-1])


def _proj_res_call(o, h, gate, w, rows):
    b, t, d = h.shape
    k = o.shape[-1]
    nb, tb = _row_tiling(b, t, rows)
    nt = t // tb
    xmap = lambda i: (i // nt, i % nt, 0)
    return pl.pallas_call(
        functools.partial(_proj_res_kernel, nb=nb, tb=tb),
        out_shape=jax.ShapeDtypeStruct(h.shape, F32),
        grid=(b // nb * nt,),
        in_specs=[pl.BlockSpec((nb, tb, k), xmap),
                  pl.BlockSpec((nb, tb, d), xmap),
                  pl.BlockSpec((nb, 1, d), lambda i: (i // nt, 0, 0)),
                  _const_spec((k, d))],
        out_specs=pl.BlockSpec((nb, tb, d), xmap),
        compiler_params=_cparams("parallel"),
        name="out_proj_residual",
    )(o, h, gate, w)


def _ffn_kernel(h_ref, sh_ref, sc_ref, gate_ref, g_ref, wgu_ref, wd_ref, out_ref,
                *, nb, tb, ff, fc):
    h = h_ref[...]
    d = h.shape[-1]
    u = _modulate(h, g_ref[...], sh_ref[...], sc_ref[...]).reshape(nb * tb, d).astype(BF16)
    acc = jnp.zeros((nb * tb, d), F32)
    for c in range(ff // fc):
        g = jnp.dot(u, wgu_ref[:, c * fc:(c + 1) * fc], preferred_element_type=F32)
        up = jnp.dot(u, wgu_ref[:, ff + c * fc:ff + (c + 1) * fc], preferred_element_type=F32)
        a = ((g * _sigmoid(g)) * up).astype(BF16)
        acc = acc + jnp.dot(a, wd_ref[c * fc:(c + 1) * fc, :], preferred_element_type=F32)
    out_ref[...] = h + gate_ref[...] * acc.reshape(nb, tb, d)


def _ffn_call(h, shift, scale, gate, g, w_gu, w_d, rows):
    b, t, d = h.shape
    ff = w_d.shape[0]
    fc = ff // 2 if (ff // 2) % LANES == 0 else ff
    nb, tb = _row_tiling(b, t, rows)
    nt = t // tb
    xmap = lambda i: (i // nt, i % nt, 0)
    mmap = lambda i: (i // nt, 0, 0)
    return pl.pallas_call(
        functools.partial(_ffn_kernel, nb=nb, tb=tb, ff=ff, fc=fc),
        out_shape=jax.ShapeDtypeStruct(h.shape, F32),
        grid=(b // nb * nt,),
        in_specs=[pl.BlockSpec((nb, tb, d), xmap),
                  pl.BlockSpec((nb, 1, d), mmap),
                  pl.BlockSpec((nb, 1, d), mmap),
                  pl.BlockSpec((nb, 1, d), mmap),
                  _const_spec((1, d)),
                  _const_spec((d, 2 * ff)),
                  _const_spec((ff, d))],
        out_specs=pl.BlockSpec((nb, tb, d), xmap),
        compiler_params=_cparams("parallel"),
        name="dense_swiglu",
    )(h, shift, scale, gate, g, w_gu, w_d)


def _qkv_b_kernel(x_ref, shq_ref, scq_ref, gq_ref, shkv_ref, sckv_ref, gkv_ref, wq_ref, wkv_ref,
                  gsum_ref, qn_ref, kn_ref, cos_ref, s1_ref, s2_ref, q_ref, k_ref, v_ref,
                  *, nb, tb):
    x = x_ref[...]
    d = x.shape[-1]
    y = _rms_scale(x)
    uq = ((y * gq_ref[...]) * (1.0 + scq_ref[...]) + shq_ref[...]).reshape(nb * tb, d)
    ukv = ((y * gkv_ref[...]) * (1.0 + sckv_ref[...]) + shkv_ref[...]).reshape(nb * tb, d)
    q = jnp.dot(uq.astype(BF16), wq_ref[...], preferred_element_type=F32)
    kv = jnp.dot(ukv.astype(BF16), wkv_ref[...], preferred_element_type=F32)
    cos, s1, s2 = (_tile_rows(r[...], nb) for r in (cos_ref, s1_ref, s2_ref))
    q_p = _group_norm_rope(q, gsum_ref, qn_ref[...], cos, s1, s2)
    k_p = _group_norm_rope(kv[:, :LANES], gsum_ref, kn_ref[...], cos, s1, s2)
    qk_scale = HEAD_DIM ** -0.5
    for j, piece in enumerate(q_p):
        q_ref[:, :, j * LANES:(j + 1) * LANES] = (piece * qk_scale).reshape(nb, tb, LANES).astype(BF16)
    k_ref[...] = k_p[0].reshape(nb, tb, LANES)
    v_ref[...] = kv[:, LANES:].reshape(nb, tb, LANES)


def _qkv_b_call(x, mod_q, mod_kv, wq, wkv, gsum, qn, kn, tables, rows):
    b, t, d = x.shape
    nq = wq.shape[1]
    assert wkv.shape[1] == 2 * LANES
    nb, tb = _row_tiling(b, t, rows)
    nt = t // tb
    xmap = lambda i: (i // nt, i % nt, 0)
    mmap = lambda i: (i // nt, 0, 0)
    tmap = lambda i: (i % nt, 0)
    mspec = pl.BlockSpec((nb, 1, d), mmap)
    return pl.pallas_call(
        functools.partial(_qkv_b_kernel, nb=nb, tb=tb),
        out_shape=(jax.ShapeDtypeStruct((b, t, nq), BF16),
                   jax.ShapeDtypeStruct((b, t, LANES), F32),
                   jax.ShapeDtypeStruct((b, t, LANES), F32)),
        grid=(b // nb * nt,),
        in_specs=[pl.BlockSpec((nb, tb, d), xmap),
                  mspec, mspec, _const_spec((1, d)),
                  mspec, mspec, _const_spec((1, d)),
                  _const_spec((d, nq)),
                  _const_spec((d, 2 * LANES)),
                  _const_spec((MXU_DIM, MXU_DIM)),
                  _const_spec((1, nq)),
                  _const_spec((1, LANES)),
                  pl.BlockSpec((tb, LANES), tmap),
                  pl.BlockSpec((tb, LANES), tmap),
                  pl.BlockSpec((tb, LANES), tmap)],
        out_specs=(pl.BlockSpec((nb, tb, nq), xmap),
                   pl.BlockSpec((nb, tb, LANES), xmap),
                   pl.BlockSpec((nb, tb, LANES), xmap)),
        compiler_params=_cparams("parallel"),
        name="mixer_b_qkv",
    )(x, mod_q[0], mod_q[1], mod_q[2], mod_kv[0], mod_kv[1], mod_kv[2], wq, wkv, gsum, qn, kn,
      *tables)


def _tile_kv_head(a, n, groups):
    lane = lax.broadcasted_iota(jnp.int32, a.shape, 1)
    swapped = pltpu.roll(a, HEAD_DIM, 1)
    keep = (lane < HEAD_DIM) if n == 0 else (lane >= HEAD_DIM)
    pair = jnp.where(keep, a, swapped).astype(BF16)
    return jnp.concatenate([pair] * (groups // 2), axis=1)


def _sink_group_attention(qs, kt, vt, mask, sink_col, groups):
    rows, width = qs.shape
    head_of_lane = lax.broadcasted_iota(jnp.int32, (rows, width), 1) // HEAD_DIM
    zero = jnp.zeros_like(qs)
    lhs = jnp.concatenate([jnp.where(head_of_lane == h, qs, zero) for h in range(groups)], axis=0)
    s = lax.dot_general(lhs, kt, (((1,), (1,)), ((), ())), preferred_element_type=F32)
    s = jnp.where(mask, s, NEG_INF)
    m = jnp.maximum(jnp.max(s, axis=1, keepdims=True), sink_col)
    e = jnp.exp(s - m)
    den = jnp.sum(e, axis=1, keepdims=True) + jnp.exp(sink_col - m)
    ob = jnp.dot((e / den).astype(BF16), vt, preferred_element_type=F32)
    out = jnp.zeros((rows, width), F32)
    for h in range(groups):
        out = out + jnp.where(head_of_lane == h, ob[h * rows:(h + 1) * rows], 0.0)
    return out


def _window_attn_kernel(q_ref, kprev_ref, kcur_ref, vprev_ref, vcur_ref, sink_ref, o_ref,
                        *, tq, groups, kv_heads):
    band = (WIN_CHUNKS + 2) * CHUNK
    q = q_ref[0]
    kc = jnp.concatenate([kprev_ref[0], kcur_ref[0]], axis=0)
    vc = jnp.concatenate([vprev_ref[0], vcur_ref[0]], axis=0)
    prev = kprev_ref.shape[1]
    first_chunk = pl.program_id(1) * (tq // CHUNK)
    width = groups * HEAD_DIM
    rows = groups * CHUNK
    col_chunk = lax.broadcasted_iota(jnp.int32, (rows, band), 1) // CHUNK
    outs = [[] for _ in range(tq // CHUNK)]
    for n in range(kv_heads):
        kt = _tile_kv_head(kc, n, groups)
        vt = _tile_kv_head(vc, n, groups)
        for c in range(tq // CHUNK):
            lo = prev + (c + 1) * CHUNK - band
            key_chunk = col_chunk + (first_chunk + c - (WIN_CHUNKS + 1))
            mask = (col_chunk >= 1) & (key_chunk >= 0)
            qs = q[c * CHUNK:(c + 1) * CHUNK, n * width:(n + 1) * width]
            outs[c].append(_sink_group_attention(qs, kt[lo:lo + band], vt[lo:lo + band], mask,
                                                 sink_ref[n], groups))
    for c in range(tq // CHUNK):
        o_ref[0, c * CHUNK:(c + 1) * CHUNK, :] = jnp.concatenate(outs[c], axis=1).astype(BF16)


def _window_attn_call(q, k, v, sink_cols, tq):
    b, t, nq = q.shape
    kv_heads = sink_cols.shape[0]
    groups = nq // (kv_heads * HEAD_DIM)
    prev = (WIN_CHUNKS + 2) * CHUNK
    assert t % tq == 0 and tq % prev == 0
    ratio = tq // prev
    cur_spec = pl.BlockSpec((1, tq, LANES), lambda bi, i: (bi, i, 0))
    prev_spec = pl.BlockSpec((1, prev, LANES), lambda bi, i: (bi, jnp.maximum(i * ratio - 1, 0), 0))
    return pl.pallas_call(
        functools.partial(_window_attn_kernel, tq=tq, groups=groups, kv_heads=kv_heads),
        out_shape=jax.ShapeDtypeStruct((b, t, nq), BF16),
        grid=(b, t // tq),
        in_specs=[pl.BlockSpec((1, tq, nq), lambda bi, i: (bi, i, 0)),
                  prev_spec, cur_spec, prev_spec, cur_spec,
                  pl.BlockSpec(sink_cols.shape, lambda bi, i: (0, 0, 0))],
        out_specs=pl.BlockSpec((1, tq, nq), lambda bi, i: (bi, i, 0)),
        compiler_params=_cparams("parallel", "parallel"),
        name="mixer_b_window_attention",
    )(q, k, k, v, v, sink_cols)


def _window_decode_kernel(q_ref, kp_ref, kn_ref, vp_ref, vn_ref, sink_ref, o_ref,
                          *, past_len, groups, kv_heads):
    q = q_ref[0]
    tn = q.shape[0]
    win = kp_ref.shape[1]
    keys = 2 * LANES
    kc = _pad_rows(jnp.concatenate([kp_ref[0], kn_ref[0]], axis=0), keys)
    vc = _pad_rows(jnp.concatenate([vp_ref[0], vn_ref[0]], axis=0), keys)
    width = groups * HEAD_DIM
    rows = groups * tn
    shape = (rows, keys)
    col = lax.broadcasted_iota(jnp.int32, shape, 1)
    q_chunk = (past_len + lax.broadcasted_iota(jnp.int32, shape, 0) % tn) // CHUNK
    k_pos = past_len - win + col
    k_chunk = k_pos // CHUNK
    mask = ((k_chunk <= q_chunk) & (k_chunk >= q_chunk - WIN_CHUNKS) & (k_pos >= 0)
            & (col < win + tn))
    outs = []
    for n in range(kv_heads):
        outs.append(_sink_group_attention(q[:, n * width:(n + 1) * width],
                                          _tile_kv_head(kc, n, groups), _tile_kv_head(vc, n, groups),
                                          mask, sink_ref[n], groups))
    o_ref[0] = jnp.concatenate(outs, axis=1).astype(BF16)


def _window_decode_call(q, k_new, v_new, past_k, past_v, sink_cols, past_len):
    b, tn, nq = q.shape
    win = past_k.shape[1]
    kv_heads = sink_cols.shape[0]
    groups = nq // (kv_heads * HEAD_DIM)
    assert win + tn <= 2 * LANES
    new_spec = pl.BlockSpec((1, tn, LANES), lambda bi: (bi, 0, 0))
    past_spec = pl.BlockSpec((1, win, LANES), lambda bi: (bi, 0, 0))
    return pl.pallas_call(
        functools.partial(_window_decode_kernel, past_len=past_len, groups=groups, kv_heads=kv_heads),
        out_shape=jax.ShapeDtypeStruct((b, tn, nq), BF16),
        grid=(b,),
        in_specs=[pl.BlockSpec((1, tn, nq), lambda bi: (bi, 0, 0)),
                  past_spec, new_spec, past_spec, new_spec,
                  pl.BlockSpec(sink_cols.shape, lambda bi: (0, 0, 0))],
        out_specs=pl.BlockSpec((1, tn, nq), lambda bi: (bi, 0, 0)),
        compiler_params=_cparams("parallel"),
        name="mixer_b_decode_attention",
    )(q, past_k, k_new, past_v, v_new, sink_cols)


def _route_top2(logits, experts):
    lane = lax.broadcasted_iota(jnp.int32, logits.shape, 1).astype(F32)
    big = float(LANES)
    lg = jnp.where(lane < experts, logits, NEG_INF)
    m1 = jnp.max(lg, axis=1, keepdims=True)
    i1 = jnp.min(jnp.where(lg == m1, lane, big), axis=1, keepdims=True)
    lg2 = jnp.where(lane == i1, NEG_INF, lg)
    m2 = jnp.max(lg2, axis=1, keepdims=True)
    i2 = jnp.min(jnp.where(lg2 == m2, lane, big), axis=1, keepdims=True)
    e2 = jnp.exp(m2 - m1)
    den = 1.0 + e2
    return jnp.where(lane == i1, 1.0 / den, 0.0) + jnp.where(lane == i2, e2 / den, 0.0)


def _moe_kernel(h_ref, sh_ref, sc_ref, gate_ref, g_ref, rhi_ref, rlo_ref, wg_ref, wu_ref, wd_ref,
                out_ref, u_s, dg_s, acc_s, *, nb, tb, experts):
    e = pl.program_id(1)
    k = pl.program_id(2)
    d = h_ref.shape[-1]

    @pl.when((e == 0) & (k == 0))
    def _():
        u = _modulate(h_ref[...], g_ref[...], sh_ref[...], sc_ref[...]).reshape(nb * tb, d)
        ub = u.astype(BF16)
        ulo = (u - ub.astype(F32)).astype(BF16)
        rhi = rhi_ref[...]
        logits = (jnp.dot(ub, rhi, preferred_element_type=F32)
                  + jnp.dot(ub, rlo_ref[...], preferred_element_type=F32)
                  + jnp.dot(ulo, rhi, preferred_element_type=F32))
        u_s[...] = ub
        dg_s[...] = _route_top2(logits, experts)
        acc_s[...] = jnp.zeros(acc_s.shape, F32)

    u = u_s[...]
    g = jnp.dot(u, wg_ref[0], preferred_element_type=F32)
    up = jnp.dot(u, wu_ref[0], preferred_element_type=F32)
    dg = dg_s[...]
    lane = lax.broadcasted_iota(jnp.int32, dg.shape, 1)
    gate_e = jnp.sum(jnp.where(lane == e, dg, 0.0), axis=1, keepdims=True)
    a = ((g * _sigmoid(g)) * up * gate_e).astype(BF16)
    acc_s[...] += jnp.dot(a, wd_ref[0], preferred_element_type=F32)

    @pl.when((e == pl.num_programs(1) - 1) & (k == pl.num_programs(2) - 1))
    def _():
        out_ref[...] = h_ref[...] + gate_ref[...] * acc_s[...].reshape(nb, tb, d)


def _moe_call(h, shift, scale, gate, g, r_hi, r_lo, w_gu, w_d, experts, rows, fc):
    b, t, d = h.shape
    ff = w_d.shape[1]
    assert ff % fc == 0
    nk = ff // fc
    nb, tb = _row_tiling(b, t, rows)
    nt = t // tb
    tm = nb * tb
    xmap = lambda i, e, k: (i // nt, i % nt, 0)
    mmap = lambda i, e, k: (i // nt, 0, 0)
    mspec = pl.BlockSpec((nb, 1, d), mmap)
    return pl.pallas_call(
        functools.partial(_moe_kernel, nb=nb, tb=tb, experts=experts),
        out_shape=jax.ShapeDtypeStruct(h.shape, F32),
        grid=(b // nb * nt, experts, nk),
        in_specs=[pl.BlockSpec((nb, tb, d), xmap),
                  mspec, mspec, mspec,
                  _const_spec((1, d)),
                  _const_spec((d, LANES)),
                  _const_spec((d, LANES)),
                  pl.BlockSpec((1, d, fc), lambda i, e, k: (e, 0, k)),
                  pl.BlockSpec((1, d, fc), lambda i, e, k: (e, 0, k + nk)),
                  pl.BlockSpec((1, fc, d), lambda i, e, k: (e, k, 0))],
        out_specs=pl.BlockSpec((nb, tb, d), xmap),
        scratch_shapes=[pltpu.VMEM((tm, d), BF16),
                        pltpu.VMEM((tm, LANES), F32),
                        pltpu.VMEM((tm, d), F32)],
        compiler_params=_cparams("parallel", "arbitrary", "arbitrary"),
        name="expert_swiglu",
    )(h, shift, scale, gate, g, r_hi, r_lo, w_gu, w_gu, w_d)


def _rope_tables(pos):
    half = ROT_DIM // 2
    inv = ROPE_THETA ** (-(jnp.arange(half, dtype=F32) * 2.0 / ROT_DIM))
    ang = pos.astype(F32)[:, None] * inv[None, :]
    cos, sin = jnp.cos(ang), jnp.sin(ang)
    t = pos.shape[0]
    zeros = lambda n: jnp.zeros((t, n), F32)
    c = jnp.concatenate([cos, cos, jnp.ones((t, HEAD_DIM - ROT_DIM), F32)], axis=1)
    s1 = jnp.concatenate([-sin, zeros(HEAD_DIM - half)], axis=1)
    s2 = jnp.concatenate([zeros(half), sin, zeros(HEAD_DIM - ROT_DIM)], axis=1)
    rep = LANES // HEAD_DIM
    return tuple(jnp.tile(a, (1, rep)) for a in (c, s1, s2))


def _mod_rows(mod, lo, hi):
    d = mod.shape[1]
    return mod[lo:hi].reshape(hi - lo, 1, d)


def _trunk(x, mods, kv_mods, w, past, rows):
    b, t, d = x.shape
    past_len = 0 if past is None else past[0].shape[2]
    pos = past_len + jnp.arange(t, dtype=jnp.int32)
    tables = _rope_tables(pos)
    heads_a = w["a_heads"]
    big = t >= rows

    m0 = mods[0]
    lam_init = 0.8 - 0.6 * math.exp(-0.3 * 0)
    qh, kh, vh, a_k, a_v = _qkv_a_call(x, m0[0], m0[1], w["norm_g"][0][0], w["a_w_qkv"], w["gsum"],
                                       w["a_q_gain"], w["a_k_gain"], tables, heads_a, rows)
    if past is None:
        o = _diff_attn_call(qh, kh, vh, w["a_lambda"], w["a_subln"], lam_init, 256, 512, 4)
    else:
        pk = past[0][0].reshape(b, past_len, heads_a * LANES)
        pv = past[1][0].reshape(b, past_len, heads_a * LANES)
        o = _diff_decode_call(qh, kh, vh, pk, pv, w["a_lambda"], w["a_subln"], lam_init)
    h = _proj_res_call(o, x, m0[2], w["a_w_o"], rows)
    h = _ffn_call(h, m0[3], m0[4], m0[5], w["norm_g"][0][1], w["f_w_gu"], w["f_w_d"], rows)

    m1 = mods[1]
    q, k_sh, v_sh = _qkv_b_call(h, (m1[0], m1[1], w["norm_g"][1][0]),
                                (kv_mods[0], kv_mods[1], w["kv_norm_g"]),
                                w["b_w_q"], w["kv_w"], w["gsum"], w["b_q_gain"], w["kv_k_gain"],
                                tables, rows)
    if past is None:
        o = _window_attn_call(q, k_sh, v_sh, w["sink_cols_prompt"], 256)
    else:
        win = past[2].shape[1]
        o = _window_decode_call(q, k_sh, v_sh, past[2].reshape(b, win, LANES),
                                past[3].reshape(b, win, LANES), w["sink_cols_decode"], past_len)
    h = _proj_res_call(o, h, m1[2], w["b_w_o"], rows)
    h = _moe_call(h, m1[3], m1[4], m1[5], w["norm_g"][1][1], w["r_hi"], w["r_lo"],
                  w["m_w_gu"], w["m_w_d"], w["experts"], 1024 if big else rows, 512)
    return h, a_k, a_v, k_sh, v_sh


def kernel(x_prompt, x_sample, c_prompt, c_sample, cache_a_k, cache_a_v, cache_b_k, cache_b_v,
           w_mod, b_mod, norm_g, a_w_qkv, a_q_norm, a_k_norm, a_lambda, a_subln, a_w_o,
           kv_norm_g, kv_w_mod, kv_b_mod, kv_w, kv_k_norm, b_w_q, b_q_norm, b_sinks, b_w_o,
           f_w_gu, f_w_d, m_router, m_w_gu, m_w_d):
    bp, t, d = x_prompt.shape
    bs, tn, _ = x_sample.shape
    assert w_mod.shape[0] == 2 and a_w_qkv.shape[0] == 1 and b_w_q.shape[0] == 1
    heads_a = cache_a_k.shape[3]
    kv_heads = cache_b_k.shape[2]
    heads_b = b_sinks.shape[1]
    groups = heads_b // kv_heads
    experts = m_router.shape[2]

    c_all = jnp.concatenate([c_prompt, c_sample], axis=0)
    mods = [_mod_call(c_all, w_mod[l], b_mod[l]) for l in range(2)]
    kv_mod = _mod_call(c_all, kv_w_mod, kv_b_mod)

    def split(mod, n, lo, hi):
        m = mod[lo:hi].reshape(hi - lo, n, 1, d)
        return [m[:, j] for j in range(n)]

    group = jnp.arange(MXU_DIM) // HEAD_DIM
    router = jnp.pad(m_router[0], ((0, 0), (0, LANES - experts)))
    r_hi = router.astype(BF16)

    def sink_cols(rows_per_head):
        s = b_sinks[0].astype(F32).reshape(kv_heads, groups, 1)
        return jnp.broadcast_to(s, (kv_heads, groups, rows_per_head)).reshape(
            kv_heads, groups * rows_per_head, 1)

    w = dict(
        a_heads=heads_a, experts=experts,
        norm_g=[[norm_g[l, j].reshape(1, d) for j in range(2)] for l in range(2)],
        gsum=(group[:, None] == group[None, :]).astype(BF16),
        a_w_qkv=a_w_qkv[0].astype(BF16),
        a_q_gain=jnp.tile(a_q_norm[0].reshape(1, LANES), (1, heads_a)),
        a_k_gain=jnp.tile(a_k_norm[0].reshape(1, LANES), (1, heads_a)),
        a_lambda=a_lambda[0].astype(F32),
        a_subln=a_subln[0].reshape(1, LANES),
        a_w_o=a_w_o[0].astype(BF16),
        f_w_gu=f_w_gu[0].astype(BF16), f_w_d=f_w_d[0].astype(BF16),
        kv_norm_g=kv_norm_g.reshape(1, d),
        kv_w=kv_w.astype(BF16),
        kv_k_gain=jnp.tile(kv_k_norm.reshape(1, HEAD_DIM), (1, LANES // HEAD_DIM)),
        b_w_q=b_w_q[0].astype(BF16),
        b_q_gain=jnp.tile(b_q_norm[0].reshape(1, HEAD_DIM), (1, heads_b)),
        b_w_o=b_w_o[0].astype(BF16),
        sink_cols_prompt=sink_cols(CHUNK), sink_cols_decode=sink_cols(tn),
        r_hi=r_hi, r_lo=(router - r_hi.astype(F32)).astype(BF16),
        m_w_gu=m_w_gu[0].astype(BF16), m_w_d=m_w_d[0].astype(BF16),
    )

    rows = 512
    y_p, ak_p, av_p, bk_p, bv_p = _trunk(
        x_prompt, [split(m, 6, 0, bp) for m in mods], split(kv_mod, 2, 0, bp), w, None, rows)
    y_s, ak_s, av_s, bk_s, bv_s = _trunk(
        x_sample, [split(m, 6, bp, bp + bs) for m in mods], split(kv_mod, 2, bp, bp + bs), w,
        (cache_a_k, cache_a_v, cache_b_k, cache_b_v), rows)

    win = min(WIN_CHUNKS * CHUNK, t)
    a_k_shape = lambda b_, t_: (1, b_, t_, heads_a, 2, HEAD_DIM)
    a_v_shape = lambda b_, t_: (1, b_, t_, heads_a, 2 * HEAD_DIM)
    b_shape = lambda b_, t_: (b_, t_, kv_heads, HEAD_DIM)
    return (y_p, y_s,
            ak_p.reshape(a_k_shape(bp, t)), av_p.reshape(a_v_shape(bp, t)),
            bk_p[:, -win:].reshape(b_shape(bp, win)), bv_p[:, -win:].reshape(b_shape(bp, win)),
            ak_s.reshape(a_k_shape(bs, tn)), av_s.reshape(a_v_shape(bs, tn)),
            bk_s.reshape(b_shape(bs, tn)), bv_s.reshape(b_shape(bs, tn)))
```

```python
import functools
import math

import jax
import jax.numpy as jnp
from jax import lax
from jax.experimental import pallas as pl
from jax.experimental.pallas import tpu as pltpu

CHUNK = 64
WIN_CHUNKS = 2
ROT_DIM = 16
ROPE_THETA = 500000.0
EPS = 1e-6
HEAD_DIM = 64
N_ROUTED = 2

LANES = 128
MXU_DIM = 256
VMEM_LIMIT_BYTES = 56 * 1024 * 1024

F32 = jnp.float32
BF16 = jnp.bfloat16
NEG_INF = float("-inf")


def _cparams(*sem):
    return pltpu.CompilerParams(dimension_semantics=sem, vmem_limit_bytes=VMEM_LIMIT_BYTES)


def _row_tiling(batch, seq, rows):
    if seq >= rows:
        assert seq % rows == 0
        return 1, rows
    nb = max(1, min(batch, rows // seq))
    while batch % nb:
        nb -= 1
    return nb, seq


def _const_spec(shape):
    zeros = (0,) * len(shape)
    return pl.BlockSpec(shape, lambda *_: zeros, pipeline_mode=pl.Buffered(1))


def _sigmoid(x):
    return 1.0 / (1.0 + jnp.exp(-x))


def _rms_scale(x):
    return x * lax.rsqrt(jnp.mean(x * x, axis=-1, keepdims=True) + EPS)


def _modulate(x, g, shift, scale):
    return (_rms_scale(x) * g) * (1.0 + scale) + shift


def _group_norm_rope(x, gsum_ref, gain, cos, s1, s2):
    width = x.shape[1]
    slab = MXU_DIM if width % MXU_DIM == 0 else LANES
    gsum = gsum_ref[:slab, :slab]
    pieces = []
    for j in range(width // slab):
        xs = x[:, j * slab:(j + 1) * slab]
        ss = jnp.dot((xs * xs).astype(BF16), gsum, preferred_element_type=F32)
        y = xs * lax.rsqrt(ss * (1.0 / HEAD_DIM) + EPS) * gain[:, j * slab:(j + 1) * slab]
        for t in range(slab // LANES):
            yt = y[:, t * LANES:(t + 1) * LANES]
            pieces.append(yt * cos + pltpu.roll(yt, LANES - ROT_DIM // 2, 1) * s1
                          + pltpu.roll(yt, ROT_DIM // 2, 1) * s2)
    return pieces


def _tile_rows(t, nb):
    return t if nb == 1 else jnp.concatenate([t] * nb, axis=0)


def _mod_kernel(c_ref, w_ref, b_ref, o_ref):
    c = c_ref[...]
    sc = (c * _sigmoid(c)).astype(BF16)
    o_ref[...] = jnp.dot(sc, w_ref[...].astype(BF16), preferred_element_type=F32) + b_ref[...]


def _mod_call(c, w, b):
    rows, d = c.shape
    n = w.shape[1]
    tn = 2048
    assert n % tn == 0
    return pl.pallas_call(
        _mod_kernel,
        out_shape=jax.ShapeDtypeStruct((rows, n), F32),
        grid=(n // tn,),
        in_specs=[pl.BlockSpec((rows, d), lambda j: (0, 0)),
                  pl.BlockSpec((d, tn), lambda j: (0, j)),
                  pl.BlockSpec((1, tn), lambda j: (0, j))],
        out_specs=pl.BlockSpec((rows, tn), lambda j: (0, j)),
        compiler_params=_cparams("parallel"),
        name="adaln_mod",
    )(c, w, b.reshape(1, n))


def _qkv_a_kernel(x_ref, sh_ref, sc_ref, g_ref, w_ref, gsum_ref, qg_ref, kg_ref,
                  cos_ref, s1_ref, s2_ref, qh_ref, kh_ref, vh_ref, ak_ref, av_ref,
                  *, nb, tb, heads):
    x = x_ref[...]
    d = x.shape[-1]
    u = _modulate(x, g_ref[...], sh_ref[...], sc_ref[...]).reshape(nb * tb, d).astype(BF16)
    qkv = jnp.dot(u, w_ref[...], preferred_element_type=F32)
    nqk = heads * 2 * HEAD_DIM
    cos, s1, s2 = (_tile_rows(r[...], nb) for r in (cos_ref, s1_ref, s2_ref))
    q_p = _group_norm_rope(qkv[:, :nqk], gsum_ref, qg_ref[...], cos, s1, s2)
    k_p = _group_norm_rope(qkv[:, nqk:2 * nqk], gsum_ref, kg_ref[...], cos, s1, s2)
    qk_scale = HEAD_DIM ** -0.5 * math.log2(math.e)
    for h in range(heads):
        cols = slice(h * LANES, (h + 1) * LANES)
        v_h = qkv[:, 2 * nqk + h * LANES:2 * nqk + (h + 1) * LANES].reshape(nb, tb, LANES)
        k_h = k_p[h].reshape(nb, tb, LANES)
        qh_ref[:, h] = (q_p[h] * qk_scale).reshape(nb, tb, LANES).astype(BF16)
        kh_ref[:, h] = k_h.astype(BF16)
        vh_ref[:, h] = v_h.astype(BF16)
        ak_ref[:, :, cols] = k_h
        av_ref[:, :, cols] = v_h


def _qkv_a_call(x, shift, scale, g, w, gsum, qg, kg, tables, heads, rows):
    b, t, d = x.shape
    nb, tb = _row_tiling(b, t, rows)
    nt = t // tb
    n = w.shape[1]
    xmap = lambda i: (i // nt, i % nt, 0)
    mmap = lambda i: (i // nt, 0, 0)
    tmap = lambda i: (i % nt, 0)
    hmap = lambda i: (i // nt, 0, i % nt, 0)
    head_shape = jax.ShapeDtypeStruct((b, heads, t, LANES), BF16)
    flat_shape = jax.ShapeDtypeStruct((b, t, heads * LANES), F32)
    return pl.pallas_call(
        functools.partial(_qkv_a_kernel, nb=nb, tb=tb, heads=heads),
        out_shape=(head_shape, head_shape, head_shape, flat_shape, flat_shape),
        grid=(b // nb * nt,),
        in_specs=[pl.BlockSpec((nb, tb, d), xmap),
                  pl.BlockSpec((nb, 1, d), mmap),
                  pl.BlockSpec((nb, 1, d), mmap),
                  _const_spec((1, d)),
                  _const_spec((d, n)),
                  _const_spec((MXU_DIM, MXU_DIM)),
                  _const_spec((1, heads * LANES)),
                  _const_spec((1, heads * LANES)),
                  pl.BlockSpec((tb, LANES), tmap),
                  pl.BlockSpec((tb, LANES), tmap),
                  pl.BlockSpec((tb, LANES), tmap)],
        out_specs=(pl.BlockSpec((nb, heads, tb, LANES), hmap),) * 3
                  + (pl.BlockSpec((nb, tb, heads * LANES), xmap),) * 2,
        compiler_params=_cparams("parallel"),
        name="mixer_a_qkv",
    )(x, shift, scale, g, w, gsum, qg, kg, *tables)


def _split_halves(q):
    lane = lax.broadcasted_iota(jnp.int32, q.shape, 1)
    zero = jnp.zeros_like(q)
    return jnp.concatenate([jnp.where(lane < HEAD_DIM, q, zero),
                            jnp.where(lane >= HEAD_DIM, q, zero)], axis=0)


def _diff_lambda(lam_ref, lam_init):
    lv = lam_ref[...]
    t1 = jnp.sum(lv[0:1] * lv[1:2], axis=1, keepdims=True)
    t2 = jnp.sum(lv[2:3] * lv[3:4], axis=1, keepdims=True)
    return jnp.exp(t1) - jnp.exp(t2) + lam_init


def _diff_finish(o0, o1, lam_ref, g_ref, lam_init):
    o = o0 - _diff_lambda(lam_ref, lam_init) * o1
    return _rms_scale(o) * g_ref[...] * (1.0 - lam_init)


def _diff_attn_kernel(q_ref, k_ref, v_ref, lam_ref, g_ref, o_ref, s_ref, m_ref, l_ref, acc_ref,
                      *, tq, tk, hp, lam_init):
    qi = pl.program_id(2)
    n_full = (qi * tq) // tk
    qqs = [_split_halves(q_ref[0, h]) for h in range(hp)]
    m_ref[...] = jnp.full(m_ref.shape, NEG_INF, F32)
    l_ref[...] = jnp.zeros(l_ref.shape, F32)
    acc_ref[...] = jnp.zeros(acc_ref.shape, F32)

    def scores(h, j):
        kj = k_ref[0, h, pl.ds(pl.multiple_of(j * tk, tk), tk), :]
        return lax.dot_general(qqs[h], kj, (((1,), (1,)), ((), ())), preferred_element_type=F32)

    def absorb(h, j, s):
        vj = v_ref[0, h, pl.ds(pl.multiple_of(j * tk, tk), tk), :]
        m_prev = m_ref[h]
        m_new = jnp.maximum(m_prev, jnp.max(s, axis=1, keepdims=True))
        alpha = jnp.exp2(m_prev - m_new)
        chunks = [jnp.exp2(s[:, c * LANES:(c + 1) * LANES] - m_new) for c in range(tk // LANES)]
        l_ref[h] = alpha * l_ref[h] + functools.reduce(lambda a, b: a + b, chunks)
        p = jnp.concatenate(chunks, axis=1).astype(BF16)
        acc_ref[h] = alpha * acc_ref[h] + jnp.dot(p, vj, preferred_element_type=F32)
        m_ref[h] = m_new

    for h in range(hp):
        s_ref[h] = scores(h, 0)

    def body(j, carry):
        for h in range(hp):
            s = s_ref[h]
            s_ref[h] = scores(h, j + 1)
            absorb(h, j, s)
        return carry

    lax.fori_loop(0, n_full, body, 0)
    q_chunk = (qi * tq + lax.broadcasted_iota(jnp.int32, (2 * tq, tk), 0) % tq) // CHUNK
    k_chunk = (n_full * tk + lax.broadcasted_iota(jnp.int32, (2 * tq, tk), 1)) // CHUNK
    visible = k_chunk <= q_chunk
    for h in range(hp):
        absorb(h, n_full, jnp.where(visible, s_ref[h], NEG_INF))

    for h in range(hp):
        o = acc_ref[h] / jnp.sum(l_ref[h], axis=1, keepdims=True)
        o_ref[0, :, h * LANES:(h + 1) * LANES] = _diff_finish(
            o[:tq], o[tq:], lam_ref, g_ref, lam_init).astype(BF16)


def _diff_attn_call(qh, kh, vh, lam_vec, subln, lam_init, tq, tk, hp):
    b, heads, t, _ = qh.shape
    assert t % tk == 0 and tk % tq == 0 and tq % CHUNK == 0 and heads % hp == 0
    return pl.pallas_call(
        functools.partial(_diff_attn_kernel, tq=tq, tk=tk, hp=hp, lam_init=lam_init),
        out_shape=jax.ShapeDtypeStruct((b, t, heads * LANES), BF16),
        grid=(b, heads // hp, t // tq),
        in_specs=[pl.BlockSpec((1, hp, tq, LANES), lambda bi, h, i: (bi, h, i, 0)),
                  pl.BlockSpec((1, hp, t, LANES), lambda bi, h, i: (bi, h, 0, 0)),
                  pl.BlockSpec((1, hp, t, LANES), lambda bi, h, i: (bi, h, 0, 0)),
                  pl.BlockSpec(lam_vec.shape, lambda bi, h, i: (0, 0)),
                  pl.BlockSpec((1, LANES), lambda bi, h, i: (0, 0))],
        out_specs=pl.BlockSpec((1, tq, hp * LANES), lambda bi, h, i: (bi, i, h)),
        scratch_shapes=[pltpu.VMEM((hp, 2 * tq, tk), F32)] + [pltpu.VMEM((hp, 2 * tq, LANES), F32)] * 3,
        compiler_params=_cparams("parallel", "parallel", "arbitrary"),
        name="mixer_a_attention",
    )(qh, kh, vh, lam_vec, subln)


def _pad_rows(a, rows):
    return jnp.concatenate([a, jnp.zeros((rows - a.shape[0], a.shape[1]), a.dtype)], axis=0)


def _diff_decode_kernel(q_ref, kp_ref, vp_ref, kn_ref, vn_ref, lam_ref, g_ref, o_ref,
                        *, past, lam_init):
    tn = q_ref.shape[2]
    qq = _split_halves(q_ref[0, 0])
    kp = kp_ref[0].astype(BF16)
    vp = vp_ref[0].astype(BF16)
    kn = _pad_rows(kn_ref[0, 0], LANES)
    vn = _pad_rows(vn_ref[0, 0], LANES)
    dims = (((1,), (1,)), ((), ()))
    s_p = lax.dot_general(qq, kp, dims, preferred_element_type=F32)
    s_n = lax.dot_general(qq, kn, dims, preferred_element_type=F32)

    def visible(shape, k_off, k_count):
        q_pos = past + lax.broadcasted_iota(jnp.int32, shape, 0) % tn
        col = lax.broadcasted_iota(jnp.int32, shape, 1)
        return ((k_off + col) // CHUNK <= q_pos // CHUNK) & (col < k_count)

    s_p = jnp.where(visible(s_p.shape, 0, past), s_p, NEG_INF)
    s_n = jnp.where(visible(s_n.shape, past, tn), s_n, NEG_INF)
    m = jnp.maximum(jnp.max(s_p, axis=1, keepdims=True), jnp.max(s_n, axis=1, keepdims=True))
    e_p = jnp.exp2(s_p - m)
    e_n = jnp.exp2(s_n - m)
    l = jnp.sum(e_p, axis=1, keepdims=True) + jnp.sum(e_n, axis=1, keepdims=True)
    acc = (jnp.dot(e_p.astype(BF16), vp, preferred_element_type=F32)
           + jnp.dot(e_n.astype(BF16), vn, preferred_element_type=F32))
    o = acc / l
    o_ref[0] = _diff_finish(o[:tn], o[tn:], lam_ref, g_ref, lam_init).astype(BF16)


def _diff_decode_call(qh, kh, vh, past_k, past_v, lam_vec, subln, lam_init):
    b, heads, tn, _ = qh.shape
    past = past_k.shape[1]
    new_spec = pl.BlockSpec((1, 1, tn, LANES), lambda bi, h: (bi, h, 0, 0))
    past_spec = pl.BlockSpec((1, past, LANES), lambda bi, h: (bi, 0, h))
    return pl.pallas_call(
        functools.partial(_diff_decode_kernel, past=past, lam_init=lam_init),
        out_shape=jax.ShapeDtypeStruct((b, tn, heads * LANES), BF16),
        grid=(b, heads),
        in_specs=[new_spec, past_spec, past_spec, new_spec, new_spec,
                  pl.BlockSpec(lam_vec.shape, lambda bi, h: (0, 0)),
                  pl.BlockSpec((1, LANES), lambda bi, h: (0, 0))],
        out_specs=pl.BlockSpec((1, tn, LANES), lambda bi, h: (bi, 0, h)),
        compiler_params=_cparams("parallel", "parallel"),
        name="mixer_a_decode_attention",
    )(qh, past_k, past_v, kh, vh, lam_vec, subln)


def _proj_res_kernel(o_ref, h_ref, gate_ref, w_ref, out_ref, *, nb, tb):
    k = o_ref.shape[-1]
    y = jnp.dot(o_ref[...].reshape(nb * tb, k), w_ref[...], preferred_element_type=F32)
    out_ref[...] = h_ref[...] + gate_ref[...] * y.reshape(nb, tb, y.shape[-1])


def _proj_res_call(o, h, gate, w, rows):
    b, t, d = h.shape
    k = o.shape[-1]
    nb, tb = _row_tiling(b, t, rows)
    nt = t // tb
    xmap = lambda i: (i // nt, i % nt, 0)
    return pl.pallas_call(
        functools.partial(_proj_res_kernel, nb=nb, tb=tb),
        out_shape=jax.ShapeDtypeStruct(h.shape, F32),
        grid=(b // nb * nt,),
        in_specs=[pl.BlockSpec((nb, tb, k), xmap),
                  pl.BlockSpec((nb, tb, d), xmap),
                  pl.BlockSpec((nb, 1, d), lambda i: (i // nt, 0, 0)),
                  _const_spec((k, d))],
        out_specs=pl.BlockSpec((nb, tb, d), xmap),
        compiler_params=_cparams("parallel"),
        name="out_proj_residual",
    )(o, h, gate, w)


def _ffn_kernel(h_ref, sh_ref, sc_ref, gate_ref, g_ref, wgu_ref, wd_ref, out_ref,
                *, nb, tb, ff, fc):
    h = h_ref[...]
    d = h.shape[-1]
    u = _modulate(h, g_ref[...], sh_ref[...], sc_ref[...]).reshape(nb * tb, d).astype(BF16)
    acc = jnp.zeros((nb * tb, d), F32)
    for c in range(ff // fc):
        g = jnp.dot(u, wgu_ref[:, c * fc:(c + 1) * fc], preferred_element_type=F32)
        up = jnp.dot(u, wgu_ref[:, ff + c * fc:ff + (c + 1) * fc], preferred_element_type=F32)
        a = ((g * _sigmoid(g)) * up).astype(BF16)
        acc = acc + jnp.dot(a, wd_ref[c * fc:(c + 1) * fc, :], preferred_element_type=F32)
    out_ref[...] = h + gate_ref[...] * acc.reshape(nb, tb, d)


def _ffn_call(h, shift, scale, gate, g, w_gu, w_d, rows):
    b, t, d = h.shape
    ff = w_d.shape[0]
    fc = ff // 2 if (ff // 2) % LANES == 0 else ff
    nb, tb = _row_tiling(b, t, rows)
    nt = t // tb
    xmap = lambda i: (i // nt, i % nt, 0)
    mmap = lambda i: (i // nt, 0, 0)
    return pl.pallas_call(
        functools.partial(_ffn_kernel, nb=nb, tb=tb, ff=ff, fc=fc),
        out_shape=jax.ShapeDtypeStruct(h.shape, F32),
        grid=(b // nb * nt,),
        in_specs=[pl.BlockSpec((nb, tb, d), xmap),
                  pl.BlockSpec((nb, 1, d), mmap),
                  pl.BlockSpec((nb, 1, d), mmap),
                  pl.BlockSpec((nb, 1, d), mmap),
                  _const_spec((1, d)),
                  _const_spec((d, 2 * ff)),
                  _const_spec((ff, d))],
        out_specs=pl.BlockSpec((nb, tb, d), xmap),
        compiler_params=_cparams("parallel"),
        name="dense_swiglu",
    )(h, shift, scale, gate, g, w_gu, w_d)


def _qkv_b_kernel(x_ref, shq_ref, scq_ref, gq_ref, shkv_ref, sckv_ref, gkv_ref, wq_ref, wkv_ref,
                  gsum_ref, qn_ref, kn_ref, cos_ref, s1_ref, s2_ref, q_ref, k_ref, v_ref,
                  *, nb, tb):
    x = x_ref[...]
    d = x.shape[-1]
    y = _rms_scale(x)
    uq = ((y * gq_ref[...]) * (1.0 + scq_ref[...]) + shq_ref[...]).reshape(nb * tb, d)
    ukv = ((y * gkv_ref[...]) * (1.0 + sckv_ref[...]) + shkv_ref[...]).reshape(nb * tb, d)
    q = jnp.dot(uq.astype(BF16), wq_ref[...], preferred_element_type=F32)
    kv = jnp.dot(ukv.astype(BF16), wkv_ref[...], preferred_element_type=F32)
    cos, s1, s2 = (_tile_rows(r[...], nb) for r in (cos_ref, s1_ref, s2_ref))
    q_p = _group_norm_rope(q, gsum_ref, qn_ref[...], cos, s1, s2)
    k_p = _group_norm_rope(kv[:, :LANES], gsum_ref, kn_ref[...], cos, s1, s2)
    qk_scale = HEAD_DIM ** -0.5
    for j, piece in enumerate(q_p):
        q_ref[:, :, j * LANES:(j + 1) * LANES] = (piece * qk_scale).reshape(nb, tb, LANES).astype(BF16)
    k_ref[...] = k_p[0].reshape(nb, tb, LANES)
    v_ref[...] = kv[:, LANES:].reshape(nb, tb, LANES)


def _qkv_b_call(x, mod_q, mod_kv, wq, wkv, gsum, qn, kn, tables, rows):
    b, t, d = x.shape
    nq = wq.shape[1]
    assert wkv.shape[1] == 2 * LANES
    nb, tb = _row_tiling(b, t, rows)
    nt = t // tb
    xmap = lambda i: (i // nt, i % nt, 0)
    mmap = lambda i: (i // nt, 0, 0)
    tmap = lambda i: (i % nt, 0)
    mspec = pl.BlockSpec((nb, 1, d), mmap)
    return pl.pallas_call(
        functools.partial(_qkv_b_kernel, nb=nb, tb=tb),
        out_shape=(jax.ShapeDtypeStruct((b, t, nq), BF16),
                   jax.ShapeDtypeStruct((b, t, LANES), F32),
                   jax.ShapeDtypeStruct((b, t, LANES), F32)),
        grid=(b // nb * nt,),
        in_specs=[pl.BlockSpec((nb, tb, d), xmap),
                  mspec, mspec, _const_spec((1, d)),
                  mspec, mspec, _const_spec((1, d)),
                  _const_spec((d, nq)),
                  _const_spec((d, 2 * LANES)),
                  _const_spec((MXU_DIM, MXU_DIM)),
                  _const_spec((1, nq)),
                  _const_spec((1, LANES)),
                  pl.BlockSpec((tb, LANES), tmap),
                  pl.BlockSpec((tb, LANES), tmap),
                  pl.BlockSpec((tb, LANES), tmap)],
        out_specs=(pl.BlockSpec((nb, tb, nq), xmap),
                   pl.BlockSpec((nb, tb, LANES), xmap),
                   pl.BlockSpec((nb, tb, LANES), xmap)),
        compiler_params=_cparams("parallel"),
        name="mixer_b_qkv",
    )(x, mod_q[0], mod_q[1], mod_q[2], mod_kv[0], mod_kv[1], mod_kv[2], wq, wkv, gsum, qn, kn,
      *tables)


def _tile_kv_head(a, n, groups):
    lane = lax.broadcasted_iota(jnp.int32, a.shape, 1)
    swapped = pltpu.roll(a, HEAD_DIM, 1)
    keep = (lane < HEAD_DIM) if n == 0 else (lane >= HEAD_DIM)
    pair = jnp.where(keep, a, swapped).astype(BF16)
    return jnp.concatenate([pair] * (groups // 2), axis=1)


def _sink_group_attention(qs, kt, vt, mask, sink_col, groups):
    rows, width = qs.shape
    head_of_lane = lax.broadcasted_iota(jnp.int32, (rows, width), 1) // HEAD_DIM
    zero = jnp.zeros_like(qs)
    lhs = jnp.concatenate([jnp.where(head_of_lane == h, qs, zero) for h in range(groups)], axis=0)
    s = lax.dot_general(lhs, kt, (((1,), (1,)), ((), ())), preferred_element_type=F32)
    s = jnp.where(mask, s, NEG_INF)
    m = jnp.maximum(jnp.max(s, axis=1, keepdims=True), sink_col)
    e = jnp.exp(s - m)
    den = jnp.sum(e, axis=1, keepdims=True) + jnp.exp(sink_col - m)
    ob = jnp.dot((e / den).astype(BF16), vt, preferred_element_type=F32)
    out = jnp.zeros((rows, width), F32)
    for h in range(groups):
        out = out + jnp.where(head_of_lane == h, ob[h * rows:(h + 1) * rows], 0.0)
    return out


def _window_attn_kernel(q_ref, kprev_ref, kcur_ref, vprev_ref, vcur_ref, sink_ref, o_ref,
                        *, tq, groups, kv_heads):
    band = (WIN_CHUNKS + 2) * CHUNK
    q = q_ref[0]
    kc = jnp.concatenate([kprev_ref[0], kcur_ref[0]], axis=0)
    vc = jnp.concatenate([vprev_ref[0], vcur_ref[0]], axis=0)
    prev = kprev_ref.shape[1]
    first_chunk = pl.program_id(1) * (tq // CHUNK)
    width = groups * HEAD_DIM
    rows = groups * CHUNK
    col_chunk = lax.broadcasted_iota(jnp.int32, (rows, band), 1) // CHUNK
    outs = [[] for _ in range(tq // CHUNK)]
    for n in range(kv_heads):
        kt = _tile_kv_head(kc, n, groups)
        vt = _tile_kv_head(vc, n, groups)
        for c in range(tq // CHUNK):
            lo = prev + (c + 1) * CHUNK - band
            key_chunk = col_chunk + (first_chunk + c - (WIN_CHUNKS + 1))
            mask = (col_chunk >= 1) & (key_chunk >= 0)
            qs = q[c * CHUNK:(c + 1) * CHUNK, n * width:(n + 1) * width]
            outs[c].append(_sink_group_attention(qs, kt[lo:lo + band], vt[lo:lo + band], mask,
                                                 sink_ref[n], groups))
    for c in range(tq // CHUNK):
        o_ref[0, c * CHUNK:(c + 1) * CHUNK, :] = jnp.concatenate(outs[c], axis=1).astype(BF16)


def _window_attn_call(q, k, v, sink_cols, tq):
    b, t, nq = q.shape
    kv_heads = sink_cols.shape[0]
    groups = nq // (kv_heads * HEAD_DIM)
    prev = (WIN_CHUNKS + 2) * CHUNK
    assert t % tq == 0 and tq % prev == 0
    ratio = tq // prev
    cur_spec = pl.BlockSpec((1, tq, LANES), lambda bi, i: (bi, i, 0))
    prev_spec = pl.BlockSpec((1, prev, LANES), lambda bi, i: (bi, jnp.maximum(i * ratio - 1, 0), 0))
    return pl.pallas_call(
        functools.partial(_window_attn_kernel, tq=tq, groups=groups, kv_heads=kv_heads),
        out_shape=jax.ShapeDtypeStruct((b, t, nq), BF16),
        grid=(b, t // tq),
        in_specs=[pl.BlockSpec((1, tq, nq), lambda bi, i: (bi, i, 0)),
                  prev_spec, cur_spec, prev_spec, cur_spec,
                  pl.BlockSpec(sink_cols.shape, lambda bi, i: (0, 0, 0))],
        out_specs=pl.BlockSpec((1, tq, nq), lambda bi, i: (bi, i, 0)),
        compiler_params=_cparams("parallel", "parallel"),
        name="mixer_b_window_attention",
    )(q, k, k, v, v, sink_cols)


def _window_decode_kernel(q_ref, kp_ref, kn_ref, vp_ref, vn_ref, sink_ref, o_ref,
                          *, past_len, groups, kv_heads):
    q = q_ref[0]
    tn = q.shape[0]
    win = kp_ref.shape[1]
    keys = 2 * LANES
    kc = _pad_rows(jnp.concatenate([kp_ref[0], kn_ref[0]], axis=0), keys)
    vc = _pad_rows(jnp.concatenate([vp_ref[0], vn_ref[0]], axis=0), keys)
    width = groups * HEAD_DIM
    rows = groups * tn
    shape = (rows, keys)
    col = lax.broadcasted_iota(jnp.int32, shape, 1)
    q_chunk = (past_len + lax.broadcasted_iota(jnp.int32, shape, 0) % tn) // CHUNK
    k_pos = past_len - win + col
    k_chunk = k_pos // CHUNK
    mask = ((k_chunk <= q_chunk) & (k_chunk >= q_chunk - WIN_CHUNKS) & (k_pos >= 0)
            & (col < win + tn))
    outs = []
    for n in range(kv_heads):
        outs.append(_sink_group_attention(q[:, n * width:(n + 1) * width],
                                          _tile_kv_head(kc, n, groups), _tile_kv_head(vc, n, groups),
                                          mask, sink_ref[n], groups))
    o_ref[0] = jnp.concatenate(outs, axis=1).astype(BF16)


def _window_decode_call(q, k_new, v_new, past_k, past_v, sink_cols, past_len):
    b, tn, nq = q.shape
    win = past_k.shape[1]
    kv_heads = sink_cols.shape[0]
    groups = nq // (kv_heads * HEAD_DIM)
    assert win + tn <= 2 * LANES
    new_spec = pl.BlockSpec((1, tn, LANES), lambda bi: (bi, 0, 0))
    past_spec = pl.BlockSpec((1, win, LANES), lambda bi: (bi, 0, 0))
    return pl.pallas_call(
        functools.partial(_window_decode_kernel, past_len=past_len, groups=groups, kv_heads=kv_heads),
        out_shape=jax.ShapeDtypeStruct((b, tn, nq), BF16),
        grid=(b,),
        in_specs=[pl.BlockSpec((1, tn, nq), lambda bi: (bi, 0, 0)),
                  past_spec, new_spec, past_spec, new_spec,
                  pl.BlockSpec(sink_cols.shape, lambda bi: (0, 0, 0))],
        out_specs=pl.BlockSpec((1, tn, nq), lambda bi: (bi, 0, 0)),
        compiler_params=_cparams("parallel"),
        name="mixer_b_decode_attention",
    )(q, past_k, k_new, past_v, v_new, sink_cols)


def _route_top2(logits, experts):
    lane = lax.broadcasted_iota(jnp.int32, logits.shape, 1).astype(F32)
    big = float(LANES)
    lg = jnp.where(lane < experts, logits, NEG_INF)
    m1 = jnp.max(lg, axis=1, keepdims=True)
    i1 = jnp.min(jnp.where(lg == m1, lane, big), axis=1, keepdims=True)
    lg2 = jnp.where(lane == i1, NEG_INF, lg)
    m2 = jnp.max(lg2, axis=1, keepdims=True)
    i2 = jnp.min(jnp.where(lg2 == m2, lane, big), axis=1, keepdims=True)
    e2 = jnp.exp(m2 - m1)
    den = 1.0 + e2
    return jnp.where(lane == i1, 1.0 / den, 0.0) + jnp.where(lane == i2, e2 / den, 0.0)


def _moe_kernel(h_ref, sh_ref, sc_ref, gate_ref, g_ref, rhi_ref, rlo_ref, wg_ref, wu_ref, wd_ref,
                out_ref, u_s, rnk_s, gat_s, rnkt_s, cnt_s, xg_s, ye_s, *, nb, tb, experts):
    e = pl.program_id(1)
    k = pl.program_id(2)
    last_k = pl.num_programs(2) - 1
    d = h_ref.shape[-1]
    tm = nb * tb
    r2 = min(2 * MXU_DIM, tm)
    r1 = r2 // 2
    kc = min(2 * MXU_DIM, tm)
    rc = min(MXU_DIM, tm)
    assert nb == 1 or rc == tm

    @pl.when((e == 0) & (k == 0))
    def _route():
        row_i = lax.broadcasted_iota(jnp.int32, (rc, rc), 0)
        col_i = lax.broadcasted_iota(jnp.int32, (rc, rc), 1)
        below = jnp.where(col_i < row_i, 1.0, 0.0).astype(BF16)

        def chunk(r, seen):
            rows = pl.ds(pl.multiple_of(r * rc, rc), rc)
            hx = h_ref[:, rows, :] if nb == 1 else h_ref[...]
            u = _modulate(hx, g_ref[...], sh_ref[...], sc_ref[...]).reshape(rc, d)
            ub = u.astype(BF16)
            ulo = (u - ub.astype(F32)).astype(BF16)
            rhi = rhi_ref[...]
            logits = (jnp.dot(ub, rhi, preferred_element_type=F32)
                      + jnp.dot(ub, rlo_ref[...], preferred_element_type=F32)
                      + jnp.dot(ulo, rhi, preferred_element_type=F32))
            dg = _route_top2(logits, experts)
            sel = dg > 0.0
            sel_f = jnp.where(sel, 1.0, 0.0)
            rank = jnp.dot(below, sel_f.astype(BF16), preferred_element_type=F32) + seen
            u_s[rows, :] = ub
            gat_s[rows, :] = dg
            rnk_s[rows, :] = jnp.where(sel, rank, -1.0)
            return seen + jnp.sum(sel_f, axis=0, keepdims=True)

        counts = lax.fori_loop(0, tm // rc, chunk, jnp.zeros((1, LANES), F32))
        for x in range(experts):
            cnt_s[x] = counts[0, x].astype(jnp.int32)
        rnkt_s[...] = rnk_s[...].T[:rnkt_s.shape[0]]
        out_ref[...] = jnp.zeros(out_ref.shape, F32)

    count = cnt_s[e]
    n2 = (count + r2 - 1) // r2

    @pl.when(k == 0)
    def _gather():
        def tile(t, carry):
            base = pl.multiple_of(t * r2, r2)
            want = (base + lax.broadcasted_iota(jnp.int32, (r2, kc), 0)).astype(F32)
            acc = jnp.zeros((r2, d), F32)
            for c in range(tm // kc):
                pos = rnkt_s[pl.ds(e, 1), c * kc:(c + 1) * kc]
                pick = jnp.where(pos == want, 1.0, 0.0).astype(BF16)
                acc = acc + jnp.dot(pick, u_s[c * kc:(c + 1) * kc, :], preferred_element_type=F32)
            xg_s[pl.ds(base, r2), :] = acc.astype(BF16)
            ye_s[pl.ds(base, r2), :] = jnp.zeros((r2, d), F32)
            return carry

        lax.fori_loop(0, n2, tile, 0)

    def expert_rows(base, rows):
        x = xg_s[pl.ds(base, rows), :]
        g = jnp.dot(x, wg_ref[0], preferred_element_type=F32)
        up = jnp.dot(x, wu_ref[0], preferred_element_type=F32)
        a = ((g * _sigmoid(g)) * up).astype(BF16)
        ye_s[pl.ds(base, rows), :] += jnp.dot(a, wd_ref[0], preferred_element_type=F32)

    n1 = (count + r1 - 1) // r1

    def full_tile(t, carry):
        expert_rows(pl.multiple_of(t * r2, r2), r2)
        return carry

    lax.fori_loop(0, n1 // 2, full_tile, 0)

    @pl.when(n1 % 2 == 1)
    def _half_tile():
        expert_rows(pl.multiple_of((n1 // 2) * r2, r2), r1)

    @pl.when(k == last_k)
    def _scatter():
        lane = lax.broadcasted_iota(jnp.int32, (tm, LANES), 1)
        pos = jnp.sum(jnp.where(lane == e, rnk_s[...], 0.0), axis=1, keepdims=True)
        gate = jnp.sum(jnp.where(lane == e, gat_s[...], 0.0), axis=1, keepdims=True)

        def tile(t, carry):
            base = pl.multiple_of(t * r2, r2)
            want = (base + lax.broadcasted_iota(jnp.int32, (tm, r2), 1)).astype(F32)
            place = jnp.where(pos == want, gate, 0.0).astype(BF16)
            y = ye_s[pl.ds(base, r2), :].astype(BF16)
            out_ref[...] += jnp.dot(place, y, preferred_element_type=F32).reshape(nb, tb, d)
            return carry

        lax.fori_loop(0, n2, tile, 0)

    @pl.when((e == pl.num_programs(1) - 1) & (k == last_k))
    def _residual():
        out_ref[...] = h_ref[...] + gate_ref[...] * out_ref[...]


def _moe_call(h, shift, scale, gate, g, r_hi, r_lo, w_gu, w_d, experts, rows, fc):
    b, t, d = h.shape
    ff = w_d.shape[1]
    assert ff % fc == 0
    nk = ff // fc
    nb, tb = _row_tiling(b, t, rows)
    nt = t // tb
    tm = nb * tb
    xmap = lambda i, e, k: (i // nt, i % nt, 0)
    mmap = lambda i, e, k: (i // nt, 0, 0)
    mspec = pl.BlockSpec((nb, 1, d), mmap)
    sub = 8
    assert experts <= sub and tm % min(2 * MXU_DIM, tm) == 0
    return pl.pallas_call(
        functools.partial(_moe_kernel, nb=nb, tb=tb, experts=experts),
        out_shape=jax.ShapeDtypeStruct(h.shape, F32),
        grid=(b // nb * nt, experts, nk),
        in_specs=[pl.BlockSpec((nb, tb, d), xmap, pipeline_mode=pl.Buffered(1)),
                  mspec, mspec, mspec,
                  _const_spec((1, d)),
                  _const_spec((d, LANES)),
                  _const_spec((d, LANES)),
                  pl.BlockSpec((1, d, fc), lambda i, e, k: (e, 0, k)),
                  pl.BlockSpec((1, d, fc), lambda i, e, k: (e, 0, k + nk)),
                  pl.BlockSpec((1, fc, d), lambda i, e, k: (e, k, 0))],
        out_specs=pl.BlockSpec((nb, tb, d), xmap),
        scratch_shapes=[pltpu.VMEM((tm, d), BF16),
                        pltpu.VMEM((tm, LANES), F32),
                        pltpu.VMEM((tm, LANES), F32),
                        pltpu.VMEM((sub, tm), F32),
                        pltpu.SMEM((sub,), jnp.int32),
                        pltpu.VMEM((tm, d), BF16),
                        pltpu.VMEM((tm, d), F32)],
        compiler_params=_cparams("parallel", "arbitrary", "arbitrary"),
        name="expert_swiglu",
    )(h, shift, scale, gate, g, r_hi, r_lo, w_gu, w_gu, w_d)


def _rope_tables(pos):
    half = ROT_DIM // 2
    inv = ROPE_THETA ** (-(jnp.arange(half, dtype=F32) * 2.0 / ROT_DIM))
    ang = pos.astype(F32)[:, None] * inv[None, :]
    cos, sin = jnp.cos(ang), jnp.sin(ang)
    t = pos.shape[0]
    zeros = lambda n: jnp.zeros((t, n), F32)
    c = jnp.concatenate([cos, cos, jnp.ones((t, HEAD_DIM - ROT_DIM), F32)], axis=1)
    s1 = jnp.concatenate([-sin, zeros(HEAD_DIM - half)], axis=1)
    s2 = jnp.concatenate([zeros(half), sin, zeros(HEAD_DIM - ROT_DIM)], axis=1)
    rep = LANES // HEAD_DIM
    return tuple(jnp.tile(a, (1, rep)) for a in (c, s1, s2))


def _mod_rows(mod, lo, hi):
    d = mod.shape[1]
    return mod[lo:hi].reshape(hi - lo, 1, d)


def _trunk(x, mods, kv_mods, w, past, rows):
    b, t, d = x.shape
    past_len = 0 if past is None else past[0].shape[2]
    pos = past_len + jnp.arange(t, dtype=jnp.int32)
    tables = _rope_tables(pos)
    heads_a = w["a_heads"]
    big = t >= rows

    m0 = mods[0]
    lam_init = 0.8 - 0.6 * math.exp(-0.3 * 0)
    qh, kh, vh, a_k, a_v = _qkv_a_call(x, m0[0], m0[1], w["norm_g"][0][0], w["a_w_qkv"], w["gsum"],
                                       w["a_q_gain"], w["a_k_gain"], tables, heads_a, rows)
    if past is None:
        o = _diff_attn_call(qh, kh, vh, w["a_lambda"], w["a_subln"], lam_init, 256, 512, 4)
    else:
        pk = past[0][0].reshape(b, past_len, heads_a * LANES)
        pv = past[1][0].reshape(b, past_len, heads_a * LANES)
        o = _diff_decode_call(qh, kh, vh, pk, pv, w["a_lambda"], w["a_subln"], lam_init)
    h = _proj_res_call(o, x, m0[2], w["a_w_o"], rows)
    h = _ffn_call(h, m0[3], m0[4], m0[5], w["norm_g"][0][1], w["f_w_gu"], w["f_w_d"], rows)

    m1 = mods[1]
    q, k_sh, v_sh = _qkv_b_call(h, (m1[0], m1[1], w["norm_g"][1][0]),
                                (kv_mods[0], kv_mods[1], w["kv_norm_g"]),
                                w["b_w_q"], w["kv_w"], w["gsum"], w["b_q_gain"], w["kv_k_gain"],
                                tables, rows)
    if past is None:
        o = _window_attn_call(q, k_sh, v_sh, w["sink_cols_prompt"], 256)
    else:
        win = past[2].shape[1]
        o = _window_decode_call(q, k_sh, v_sh, past[2].reshape(b, win, LANES),
                                past[3].reshape(b, win, LANES), w["sink_cols_decode"], past_len)
    h = _proj_res_call(o, h, m1[2], w["b_w_o"], rows)
    h = _moe_call(h, m1[3], m1[4], m1[5], w["norm_g"][1][1], w["r_hi"], w["r_lo"],
                  w["m_w_gu"], w["m_w_d"], w["experts"], min(2048, t) if big else rows, 512)
    return h, a_k, a_v, k_sh, v_sh


def kernel(x_prompt, x_sample, c_prompt, c_sample, cache_a_k, cache_a_v, cache_b_k, cache_b_v,
           w_mod, b_mod, norm_g, a_w_qkv, a_q_norm, a_k_norm, a_lambda, a_subln, a_w_o,
           kv_norm_g, kv_w_mod, kv_b_mod, kv_w, kv_k_norm, b_w_q, b_q_norm, b_sinks, b_w_o,
           f_w_gu, f_w_d, m_router, m_w_gu, m_w_d):
    bp, t, d = x_prompt.shape
    bs, tn, _ = x_sample.shape
    assert w_mod.shape[0] == 2 and a_w_qkv.shape[0] == 1 and b_w_q.shape[0] == 1
    heads_a = cache_a_k.shape[3]
    kv_heads = cache_b_k.shape[2]
    heads_b = b_sinks.shape[1]
    groups = heads_b // kv_heads
    experts = m_router.shape[2]

    c_all = jnp.concatenate([c_prompt, c_sample], axis=0)
    mods = [_mod_call(c_all, w_mod[l], b_mod[l]) for l in range(2)]
    kv_mod = _mod_call(c_all, kv_w_mod, kv_b_mod)

    def split(mod, n, lo, hi):
        m = mod[lo:hi].reshape(hi - lo, n, 1, d)
        return [m[:, j] for j in range(n)]

    group = jnp.arange(MXU_DIM) // HEAD_DIM
    router = jnp.pad(m_router[0], ((0, 0), (0, LANES - experts)))
    r_hi = router.astype(BF16)

    def sink_cols(rows_per_head):
        s = b_sinks[0].astype(F32).reshape(kv_heads, groups, 1)
        return jnp.broadcast_to(s, (kv_heads, groups, rows_per_head)).reshape(
            kv_heads, groups * rows_per_head, 1)

    w = dict(
        a_heads=heads_a, experts=experts,
        norm_g=[[norm_g[l, j].reshape(1, d) for j in range(2)] for l in range(2)],
        gsum=(group[:, None] == group[None, :]).astype(BF16),
        a_w_qkv=a_w_qkv[0].astype(BF16),
        a_q_gain=jnp.tile(a_q_norm[0].reshape(1, LANES), (1, heads_a)),
        a_k_gain=jnp.tile(a_k_norm[0].reshape(1, LANES), (1, heads_a)),
        a_lambda=a_lambda[0].astype(F32),
        a_subln=a_subln[0].reshape(1, LANES),
        a_w_o=a_w_o[0].astype(BF16),
        f_w_gu=f_w_gu[0].astype(BF16), f_w_d=f_w_d[0].astype(BF16),
        kv_norm_g=kv_norm_g.reshape(1, d),
        kv_w=kv_w.astype(BF16),
        kv_k_gain=jnp.tile(kv_k_norm.reshape(1, HEAD_DIM), (1, LANES // HEAD_DIM)),
        b_w_q=b_w_q[0].astype(BF16),
        b_q_gain=jnp.tile(b_q_norm[0].reshape(1, HEAD_DIM), (1, heads_b)),
        b_w_o=b_w_o[0].astype(BF16),
        sink_cols_prompt=sink_cols(CHUNK), sink_cols_decode=sink_cols(tn),
        r_hi=r_hi, r_lo=(router - r_hi.astype(F32)).astype(BF16),
        m_w_gu=m_w_gu[0].astype(BF16), m_w_d=m_w_d[0].astype(BF16),
    )

    rows = 512
    y_p, ak_p, av_p, bk_p, bv_p = _trunk(
        x_prompt, [split(m, 6, 0, bp) for m in mods], split(kv_mod, 2, 0, bp), w, None, rows)
    y_s, ak_s, av_s, bk_s, bv_s = _trunk(
        x_sample, [split(m, 6, bp, bp + bs) for m in mods], split(kv_mod, 2, bp, bp + bs), w,
        (cache_a_k, cache_a_v, cache_b_k, cache_b_v), rows)

    win = min(WIN_CHUNKS * CHUNK, t)
    a_k_shape = lambda b_, t_: (1, b_, t_, heads_a, 2, HEAD_DIM)
    a_v_shape = lambda b_, t_: (1, b_, t_, heads_a, 2 * HEAD_DIM)
    b_shape = lambda b_, t_: (b_, t_, kv_heads, HEAD_DIM)
    return (y_p, y_s,
            ak_p.reshape(a_k_shape(bp, t)), av_p.reshape(a_v_shape(bp, t)),
            bk_p[:, -win:].reshape(b_shape(bp, win)), bv_p[:, -win:].reshape(b_shape(bp, win)),
            ak_s.reshape(a_k_shape(bs, tn)), av_s.reshape(a_v_shape(bs, tn)),
            bk_s.reshape(b_shape(bs, tn)), bv_s.reshape(b_shape(bs, tn)))
```

```python
import functools
import math

import jax
import jax.numpy as jnp
from jax import lax
from jax.experimental import pallas as pl
from jax.experimental.pallas import tpu as pltpu

CHUNK = 64
WIN_CHUNKS = 2
ROT_DIM = 16
ROPE_THETA = 500000.0
EPS = 1e-6
HEAD_DIM = 64
N_ROUTED = 2

LANES = 128
MXU_DIM = 256
VMEM_LIMIT_BYTES = 56 * 1024 * 1024

F32 = jnp.float32
BF16 = jnp.bfloat16
NEG_INF = float("-inf")


def _cparams(*sem):
    return pltpu.CompilerParams(dimension_semantics=sem, vmem_limit_bytes=VMEM_LIMIT_BYTES)


def _row_tiling(batch, seq, rows):
    if seq >= rows:
        assert seq % rows == 0
        return 1, rows
    nb = max(1, min(batch, rows // seq))
    while batch % nb:
        nb -= 1
    return nb, seq


def _const_spec(shape):
    zeros = (0,) * len(shape)
    return pl.BlockSpec(shape, lambda *_: zeros, pipeline_mode=pl.Buffered(1))


def _sigmoid(x):
    return 1.0 / (1.0 + jnp.exp(-x))


def _rms_scale(x):
    return x * lax.rsqrt(jnp.mean(x * x, axis=-1, keepdims=True) + EPS)


def _modulate(x, g, shift, scale):
    return (_rms_scale(x) * g) * (1.0 + scale) + shift


def _group_norm_rope(x, gsum_ref, gain, cos, s1, s2):
    width = x.shape[1]
    slab = MXU_DIM if width % MXU_DIM == 0 else LANES
    gsum = gsum_ref[:slab, :slab]
    pieces = []
    for j in range(width // slab):
        xs = x[:, j * slab:(j + 1) * slab]
        ss = jnp.dot((xs * xs).astype(BF16), gsum, preferred_element_type=F32)
        y = xs * lax.rsqrt(ss * (1.0 / HEAD_DIM) + EPS) * gain[:, j * slab:(j + 1) * slab]
        for t in range(slab // LANES):
            yt = y[:, t * LANES:(t + 1) * LANES]
            pieces.append(yt * cos + pltpu.roll(yt, LANES - ROT_DIM // 2, 1) * s1
                          + pltpu.roll(yt, ROT_DIM // 2, 1) * s2)
    return pieces


def _tile_rows(t, nb):
    return t if nb == 1 else jnp.concatenate([t] * nb, axis=0)


def _mod_kernel(c_ref, w_ref, b_ref, o_ref):
    c = c_ref[...]
    sc = (c * _sigmoid(c)).astype(BF16)
    o_ref[...] = jnp.dot(sc, w_ref[...].astype(BF16), preferred_element_type=F32) + b_ref[...]


def _mod_call(c, w, b):
    rows, d = c.shape
    n = w.shape[1]
    tn = 2048
    assert n % tn == 0
    return pl.pallas_call(
        _mod_kernel,
        out_shape=jax.ShapeDtypeStruct((rows, n), F32),
        grid=(n // tn,),
        in_specs=[pl.BlockSpec((rows, d), lambda j: (0, 0)),
                  pl.BlockSpec((d, tn), lambda j: (0, j)),
                  pl.BlockSpec((1, tn), lambda j: (0, j))],
        out_specs=pl.BlockSpec((rows, tn), lambda j: (0, j)),
        compiler_params=_cparams("parallel"),
        name="adaln_mod",
    )(c, w, b.reshape(1, n))


def _qkv_a_kernel(x_ref, sh_ref, sc_ref, g_ref, w_ref, gsum_ref, qg_ref, kg_ref,
                  cos_ref, s1_ref, s2_ref, qh_ref, kh_ref, vh_ref, ak_ref, av_ref,
                  *, nb, tb, heads):
    x = x_ref[...]
    d = x.shape[-1]
    u = _modulate(x, g_ref[...], sh_ref[...], sc_ref[...]).reshape(nb * tb, d).astype(BF16)
    qkv = jnp.dot(u, w_ref[...], preferred_element_type=F32)
    nqk = heads * 2 * HEAD_DIM
    cos, s1, s2 = (_tile_rows(r[...], nb) for r in (cos_ref, s1_ref, s2_ref))
    q_p = _group_norm_rope(qkv[:, :nqk], gsum_ref, qg_ref[...], cos, s1, s2)
    k_p = _group_norm_rope(qkv[:, nqk:2 * nqk], gsum_ref, kg_ref[...], cos, s1, s2)
    qk_scale = HEAD_DIM ** -0.5 * math.log2(math.e)
    for h in range(heads):
        cols = slice(h * LANES, (h + 1) * LANES)
        v_h = qkv[:, 2 * nqk + h * LANES:2 * nqk + (h + 1) * LANES].reshape(nb, tb, LANES)
        k_h = k_p[h].reshape(nb, tb, LANES)
        qh_ref[:, h] = (q_p[h] * qk_scale).reshape(nb, tb, LANES).astype(BF16)
        kh_ref[:, h] = k_h.astype(BF16)
        vh_ref[:, h] = v_h.astype(BF16)
        ak_ref[:, :, cols] = k_h
        av_ref[:, :, cols] = v_h


def _qkv_a_call(x, shift, scale, g, w, gsum, qg, kg, tables, heads, rows):
    b, t, d = x.shape
    nb, tb = _row_tiling(b, t, rows)
    nt = t // tb
    n = w.shape[1]
    xmap = lambda i: (i // nt, i % nt, 0)
    mmap = lambda i: (i // nt, 0, 0)
    tmap = lambda i: (i % nt, 0)
    hmap = lambda i: (i // nt, 0, i % nt, 0)
    head_shape = jax.ShapeDtypeStruct((b, heads, t, LANES), BF16)
    flat_shape = jax.ShapeDtypeStruct((b, t, heads * LANES), F32)
    return pl.pallas_call(
        functools.partial(_qkv_a_kernel, nb=nb, tb=tb, heads=heads),
        out_shape=(head_shape, head_shape, head_shape, flat_shape, flat_shape),
        grid=(b // nb * nt,),
        in_specs=[pl.BlockSpec((nb, tb, d), xmap),
                  pl.BlockSpec((nb, 1, d), mmap),
                  pl.BlockSpec((nb, 1, d), mmap),
                  _const_spec((1, d)),
                  _const_spec((d, n)),
                  _const_spec((MXU_DIM, MXU_DIM)),
                  _const_spec((1, heads * LANES)),
                  _const_spec((1, heads * LANES)),
                  pl.BlockSpec((tb, LANES), tmap),
                  pl.BlockSpec((tb, LANES), tmap),
                  pl.BlockSpec((tb, LANES), tmap)],
        out_specs=(pl.BlockSpec((nb, heads, tb, LANES), hmap),) * 3
                  + (pl.BlockSpec((nb, tb, heads * LANES), xmap),) * 2,
        compiler_params=_cparams("parallel"),
        name="mixer_a_qkv",
    )(x, shift, scale, g, w, gsum, qg, kg, *tables)


def _split_halves(q):
    lane = lax.broadcasted_iota(jnp.int32, q.shape, 1)
    zero = jnp.zeros_like(q)
    return jnp.concatenate([jnp.where(lane < HEAD_DIM, q, zero),
                            jnp.where(lane >= HEAD_DIM, q, zero)], axis=0)


def _diff_lambda(lam_ref, lam_init):
    lv = lam_ref[...]
    t1 = jnp.sum(lv[0:1] * lv[1:2], axis=1, keepdims=True)
    t2 = jnp.sum(lv[2:3] * lv[3:4], axis=1, keepdims=True)
    return jnp.exp(t1) - jnp.exp(t2) + lam_init


def _diff_finish(o0, o1, lam_ref, g_ref, lam_init):
    o = o0 - _diff_lambda(lam_ref, lam_init) * o1
    return _rms_scale(o) * g_ref[...] * (1.0 - lam_init)


def _diff_attn_kernel(q_ref, k_ref, v_ref, lam_ref, g_ref, o_ref, s_ref, m_ref, l_ref, acc_ref,
                      *, tq, tk, hp, lam_init):
    qi = pl.program_id(2)
    n_full = (qi * tq) // tk
    qqs = [_split_halves(q_ref[0, h]) for h in range(hp)]
    m_ref[...] = jnp.full(m_ref.shape, NEG_INF, F32)
    l_ref[...] = jnp.zeros(l_ref.shape, F32)
    acc_ref[...] = jnp.zeros(acc_ref.shape, F32)

    def scores(h, j):
        kj = k_ref[0, h, pl.ds(pl.multiple_of(j * tk, tk), tk), :]
        return lax.dot_general(qqs[h], kj, (((1,), (1,)), ((), ())), preferred_element_type=F32)

    def absorb(h, j, s):
        vj = v_ref[0, h, pl.ds(pl.multiple_of(j * tk, tk), tk), :]
        m_prev = m_ref[h]
        m_new = jnp.maximum(m_prev, jnp.max(s, axis=1, keepdims=True))
        alpha = jnp.exp2(m_prev - m_new)
        chunks = [jnp.exp2(s[:, c * LANES:(c + 1) * LANES] - m_new) for c in range(tk // LANES)]
        l_ref[h] = alpha * l_ref[h] + functools.reduce(lambda a, b: a + b, chunks)
        p = jnp.concatenate(chunks, axis=1).astype(BF16)
        acc_ref[h] = alpha * acc_ref[h] + jnp.dot(p, vj, preferred_element_type=F32)
        m_ref[h] = m_new

    for h in range(hp):
        s_ref[h] = scores(h, 0)

    def body(j, carry):
        for h in range(hp):
            s = s_ref[h]
            s_ref[h] = scores(h, j + 1)
            absorb(h, j, s)
        return carry

    lax.fori_loop(0, n_full, body, 0)
    q_chunk = (qi * tq + lax.broadcasted_iota(jnp.int32, (2 * tq, tk), 0) % tq) // CHUNK
    k_chunk = (n_full * tk + lax.broadcasted_iota(jnp.int32, (2 * tq, tk), 1)) // CHUNK
    visible = k_chunk <= q_chunk
    for h in range(hp):
        absorb(h, n_full, jnp.where(visible, s_ref[h], NEG_INF))

    for h in range(hp):
        o = acc_ref[h] / jnp.sum(l_ref[h], axis=1, keepdims=True)
        o_ref[0, :, h * LANES:(h + 1) * LANES] = _diff_finish(
            o[:tq], o[tq:], lam_ref, g_ref, lam_init).astype(BF16)


def _diff_attn_call(qh, kh, vh, lam_vec, subln, lam_init, tq, tk, hp):
    b, heads, t, _ = qh.shape
    assert t % tk == 0 and tk % tq == 0 and tq % CHUNK == 0 and heads % hp == 0
    return pl.pallas_call(
        functools.partial(_diff_attn_kernel, tq=tq, tk=tk, hp=hp, lam_init=lam_init),
        out_shape=jax.ShapeDtypeStruct((b, t, heads * LANES), BF16),
        grid=(b, heads // hp, t // tq),
        in_specs=[pl.BlockSpec((1, hp, tq, LANES), lambda bi, h, i: (bi, h, i, 0)),
                  pl.BlockSpec((1, hp, t, LANES), lambda bi, h, i: (bi, h, 0, 0)),
                  pl.BlockSpec((1, hp, t, LANES), lambda bi, h, i: (bi, h, 0, 0)),
                  pl.BlockSpec(lam_vec.shape, lambda bi, h, i: (0, 0)),
                  pl.BlockSpec((1, LANES), lambda bi, h, i: (0, 0))],
        out_specs=pl.BlockSpec((1, tq, hp * LANES), lambda bi, h, i: (bi, i, h)),
        scratch_shapes=[pltpu.VMEM((hp, 2 * tq, tk), F32)] + [pltpu.VMEM((hp, 2 * tq, LANES), F32)] * 3,
        compiler_params=_cparams("parallel", "parallel", "arbitrary"),
        name="mixer_a_attention",
    )(qh, kh, vh, lam_vec, subln)


def _pad_rows(a, rows):
    return jnp.concatenate([a, jnp.zeros((rows - a.shape[0], a.shape[1]), a.dtype)], axis=0)


def _diff_decode_kernel(q_ref, kp_ref, vp_ref, kn_ref, vn_ref, lam_ref, g_ref, o_ref,
                        *, past, lam_init):
    tn = q_ref.shape[2]
    qq = _split_halves(q_ref[0, 0])
    kp = kp_ref[0].astype(BF16)
    vp = vp_ref[0].astype(BF16)
    kn = _pad_rows(kn_ref[0, 0], LANES)
    vn = _pad_rows(vn_ref[0, 0], LANES)
    dims = (((1,), (1,)), ((), ()))
    s_p = lax.dot_general(qq, kp, dims, preferred_element_type=F32)
    s_n = lax.dot_general(qq, kn, dims, preferred_element_type=F32)

    def visible(shape, k_off, k_count):
        q_pos = past + lax.broadcasted_iota(jnp.int32, shape, 0) % tn
        col = lax.broadcasted_iota(jnp.int32, shape, 1)
        return ((k_off + col) // CHUNK <= q_pos // CHUNK) & (col < k_count)

    s_p = jnp.where(visible(s_p.shape, 0, past), s_p, NEG_INF)
    s_n = jnp.where(visible(s_n.shape, past, tn), s_n, NEG_INF)
    m = jnp.maximum(jnp.max(s_p, axis=1, keepdims=True), jnp.max(s_n, axis=1, keepdims=True))
    e_p = jnp.exp2(s_p - m)
    e_n = jnp.exp2(s_n - m)
    l = jnp.sum(e_p, axis=1, keepdims=True) + jnp.sum(e_n, axis=1, keepdims=True)
    acc = (jnp.dot(e_p.astype(BF16), vp, preferred_element_type=F32)
           + jnp.dot(e_n.astype(BF16), vn, preferred_element_type=F32))
    o = acc / l
    o_ref[0] = _diff_finish(o[:tn], o[tn:], lam_ref, g_ref, lam_init).astype(BF16)


def _diff_decode_call(qh, kh, vh, past_k, past_v, lam_vec, subln, lam_init):
    b, heads, tn, _ = qh.shape
    past = past_k.shape[1]
    new_spec = pl.BlockSpec((1, 1, tn, LANES), lambda bi, h: (bi, h, 0, 0))
    past_spec = pl.BlockSpec((1, past, LANES), lambda bi, h: (bi, 0, h))
    return pl.pallas_call(
        functools.partial(_diff_decode_kernel, past=past, lam_init=lam_init),
        out_shape=jax.ShapeDtypeStruct((b, tn, heads * LANES), BF16),
        grid=(b, heads),
        in_specs=[new_spec, past_spec, past_spec, new_spec, new_spec,
                  pl.BlockSpec(lam_vec.shape, lambda bi, h: (0, 0)),
                  pl.BlockSpec((1, LANES), lambda bi, h: (0, 0))],
        out_specs=pl.BlockSpec((1, tn, LANES), lambda bi, h: (bi, 0, h)),
        compiler_params=_cparams("parallel", "parallel"),
        name="mixer_a_decode_attention",
    )(qh, past_k, past_v, kh, vh, lam_vec, subln)


def _proj_res_kernel(o_ref, h_ref, gate_ref, w_ref, out_ref, *, nb, tb):
    k = o_ref.shape[-1]
    y = jnp.dot(o_ref[...].reshape(nb * tb, k), w_ref[...], preferred_element_type=F32)
    out_ref[...] = h_ref[...] + gate_ref[...] * y.reshape(nb, tb, y.shape[-1])


def _proj_res_call(o, h, gate, w, rows):
    b, t, d = h.shape
    k = o.shape[-1]
    nb, tb = _row_tiling(b, t, rows)
    nt = t // tb
    xmap = lambda i: (i // nt, i % nt, 0)
    return pl.pallas_call(
        functools.partial(_proj_res_kernel, nb=nb, tb=tb),
        out_shape=jax.ShapeDtypeStruct(h.shape, F32),
        grid=(b // nb * nt,),
        in_specs=[pl.BlockSpec((nb, tb, k), xmap),
                  pl.BlockSpec((nb, tb, d), xmap),
                  pl.BlockSpec((nb, 1, d), lambda i: (i // nt, 0, 0)),
                  _const_spec((k, d))],
        out_specs=pl.BlockSpec((nb, tb, d), xmap),
        compiler_params=_cparams("parallel"),
        name="out_proj_residual",
    )(o, h, gate, w)


def _ffn_kernel(h_ref, sh_ref, sc_ref, gate_ref, g_ref, wgu_ref, wd_ref, out_ref,
                *, nb, tb, ff, fc):
    h = h_ref[...]
    d = h.shape[-1]
    u = _modulate(h, g_ref[...], sh_ref[...], sc_ref[...]).reshape(nb * tb, d).astype(BF16)
    acc = jnp.zeros((nb * tb, d), F32)
    for c in range(ff // fc):
        g = jnp.dot(u, wgu_ref[:, c * fc:(c + 1) * fc], preferred_element_type=F32)
        up = jnp.dot(u, wgu_ref[:, ff + c * fc:ff + (c + 1) * fc], preferred_element_type=F32)
        a = ((g * _sigmoid(g)) * up).astype(BF16)
        acc = acc + jnp.dot(a, wd_ref[c * fc:(c + 1) * fc, :], preferred_element_type=F32)
    out_ref[...] = h + gate_ref[...] * acc.reshape(nb, tb, d)


def _ffn_call(h, shift, scale, gate, g, w_gu, w_d, rows):
    b, t, d = h.shape
    ff = w_d.shape[0]
    fc = ff // 2 if (ff // 2) % LANES == 0 else ff
    nb, tb = _row_tiling(b, t, rows)
    nt = t // tb
    xmap = lambda i: (i // nt, i % nt, 0)
    mmap = lambda i: (i // nt, 0, 0)
    return pl.pallas_call(
        functools.partial(_ffn_kernel, nb=nb, tb=tb, ff=ff, fc=fc),
        out_shape=jax.ShapeDtypeStruct(h.shape, F32),
        grid=(b // nb * nt,),
        in_specs=[pl.BlockSpec((nb, tb, d), xmap),
                  pl.BlockSpec((nb, 1, d), mmap),
                  pl.BlockSpec((nb, 1, d), mmap),
                  pl.BlockSpec((nb, 1, d), mmap),
                  _const_spec((1, d)),
                  _const_spec((d, 2 * ff)),
                  _const_spec((ff, d))],
        out_specs=pl.BlockSpec((nb, tb, d), xmap),
        compiler_params=_cparams("parallel"),
        name="dense_swiglu",
    )(h, shift, scale, gate, g, w_gu, w_d)


def _qkv_b_kernel(x_ref, shq_ref, scq_ref, gq_ref, shkv_ref, sckv_ref, gkv_ref, wq_ref, wkv_ref,
                  gsum_ref, qn_ref, kn_ref, cos_ref, s1_ref, s2_ref, q_ref, k_ref, v_ref,
                  *, nb, tb):
    x = x_ref[...]
    d = x.shape[-1]
    y = _rms_scale(x)
    uq = ((y * gq_ref[...]) * (1.0 + scq_ref[...]) + shq_ref[...]).reshape(nb * tb, d)
    ukv = ((y * gkv_ref[...]) * (1.0 + sckv_ref[...]) + shkv_ref[...]).reshape(nb * tb, d)
    q = jnp.dot(uq.astype(BF16), wq_ref[...], preferred_element_type=F32)
    kv = jnp.dot(ukv.astype(BF16), wkv_ref[...], preferred_element_type=F32)
    cos, s1, s2 = (_tile_rows(r[...], nb) for r in (cos_ref, s1_ref, s2_ref))
    q_p = _group_norm_rope(q, gsum_ref, qn_ref[...], cos, s1, s2)
    k_p = _group_norm_rope(kv[:, :LANES], gsum_ref, kn_ref[...], cos, s1, s2)
    qk_scale = HEAD_DIM ** -0.5
    for j, piece in enumerate(q_p):
        q_ref[:, :, j * LANES:(j + 1) * LANES] = (piece * qk_scale).reshape(nb, tb, LANES).astype(BF16)
    k_ref[...] = k_p[0].reshape(nb, tb, LANES)
    v_ref[...] = kv[:, LANES:].reshape(nb, tb, LANES)


def _qkv_b_call(x, mod_q, mod_kv, wq, wkv, gsum, qn, kn, tables, rows):
    b, t, d = x.shape
    nq = wq.shape[1]
    assert wkv.shape[1] == 2 * LANES
    nb, tb = _row_tiling(b, t, rows)
    nt = t // tb
    xmap = lambda i: (i // nt, i % nt, 0)
    mmap = lambda i: (i // nt, 0, 0)
    tmap = lambda i: (i % nt, 0)
    mspec = pl.BlockSpec((nb, 1, d), mmap)
    return pl.pallas_call(
        functools.partial(_qkv_b_kernel, nb=nb, tb=tb),
        out_shape=(jax.ShapeDtypeStruct((b, t, nq), BF16),
                   jax.ShapeDtypeStruct((b, t, LANES), F32),
                   jax.ShapeDtypeStruct((b, t, LANES), F32)),
        grid=(b // nb * nt,),
        in_specs=[pl.BlockSpec((nb, tb, d), xmap),
                  mspec, mspec, _const_spec((1, d)),
                  mspec, mspec, _const_spec((1, d)),
                  _const_spec((d, nq)),
                  _const_spec((d, 2 * LANES)),
                  _const_spec((MXU_DIM, MXU_DIM)),
                  _const_spec((1, nq)),
                  _const_spec((1, LANES)),
                  pl.BlockSpec((tb, LANES), tmap),
                  pl.BlockSpec((tb, LANES), tmap),
                  pl.BlockSpec((tb, LANES), tmap)],
        out_specs=(pl.BlockSpec((nb, tb, nq), xmap),
                   pl.BlockSpec((nb, tb, LANES), xmap),
                   pl.BlockSpec((nb, tb, LANES), xmap)),
        compiler_params=_cparams("parallel"),
        name="mixer_b_qkv",
    )(x, mod_q[0], mod_q[1], mod_q[2], mod_kv[0], mod_kv[1], mod_kv[2], wq, wkv, gsum, qn, kn,
      *tables)


def _tile_kv_head(a, n, groups):
    lane = lax.broadcasted_iota(jnp.int32, a.shape, 1)
    swapped = pltpu.roll(a, HEAD_DIM, 1)
    keep = (lane < HEAD_DIM) if n == 0 else (lane >= HEAD_DIM)
    pair = jnp.where(keep, a, swapped).astype(BF16)
    return jnp.concatenate([pair] * (groups // 2), axis=1)


def _sink_group_attention(qs, kt, vt, mask, sink_col, groups):
    rows, width = qs.shape
    head_of_lane = lax.broadcasted_iota(jnp.int32, (rows, width), 1) // HEAD_DIM
    zero = jnp.zeros_like(qs)
    lhs = jnp.concatenate([jnp.where(head_of_lane == h, qs, zero) for h in range(groups)], axis=0)
    s = lax.dot_general(lhs, kt, (((1,), (1,)), ((), ())), preferred_element_type=F32)
    s = jnp.where(mask, s, NEG_INF)
    m = jnp.maximum(jnp.max(s, axis=1, keepdims=True), sink_col)
    e = jnp.exp(s - m)
    den = jnp.sum(e, axis=1, keepdims=True) + jnp.exp(sink_col - m)
    ob = jnp.dot((e / den).astype(BF16), vt, preferred_element_type=F32)
    out = jnp.zeros((rows, width), F32)
    for h in range(groups):
        out = out + jnp.where(head_of_lane == h, ob[h * rows:(h + 1) * rows], 0.0)
    return out


def _window_attn_kernel(q_ref, kprev_ref, kcur_ref, vprev_ref, vcur_ref, sink_ref, o_ref,
                        *, tq, groups, kv_heads):
    q = q_ref[0]
    kc = jnp.concatenate([kprev_ref[0], kcur_ref[0]], axis=0)
    vc = jnp.concatenate([vprev_ref[0], vcur_ref[0]], axis=0)
    win = kprev_ref.shape[1]
    keys = win + tq
    first_chunk = pl.program_id(1) * (tq // CHUNK)
    q_chunk = lax.broadcasted_iota(jnp.int32, (tq, keys), 0) // CHUNK
    k_chunk = lax.broadcasted_iota(jnp.int32, (tq, keys), 1) // CHUNK - win // CHUNK
    visible = ((k_chunk <= q_chunk) & (k_chunk >= q_chunk - WIN_CHUNKS)
               & (k_chunk + first_chunk >= 0))
    bias = jnp.where(visible, 0.0, NEG_INF)
    lane = lax.broadcasted_iota(jnp.int32, (tq, LANES), 1)
    in_half = (lane < HEAD_DIM, lane >= HEAD_DIM)
    zero = jnp.zeros((tq, LANES), BF16)
    pairs = groups // 2
    for n in range(kv_heads):
        kt = _tile_kv_head(kc, n, 2)
        vt = _tile_kv_head(vc, n, 2)
        for j in range(pairs):
            blk = n * pairs + j
            qp = q[:, blk * LANES:(blk + 1) * LANES]
            halves = []
            for half in range(2):
                sink = sink_ref[2 * blk + half]
                qm = jnp.where(in_half[half], qp, zero)
                s = lax.dot_general(qm, kt, (((1,), (1,)), ((), ())), preferred_element_type=F32)
                s = s + bias
                m = jnp.maximum(jnp.max(s, axis=1, keepdims=True), sink)
                e = jnp.exp(s - m)
                den = jnp.sum(e, axis=1, keepdims=True) + jnp.exp(sink - m)
                halves.append(jnp.dot(e.astype(BF16), vt, preferred_element_type=F32) / den)
            o_ref[0, :, blk * LANES:(blk + 1) * LANES] = jnp.where(
                in_half[0], halves[0], halves[1]).astype(BF16)


def _window_attn_call(q, k, v, sinks, kv_heads, tq):
    b, t, nq = q.shape
    groups = nq // (kv_heads * HEAD_DIM)
    win = WIN_CHUNKS * CHUNK
    assert t % tq == 0 and tq % win == 0 and groups % 2 == 0 and win == LANES
    ratio = tq // win
    cur_spec = pl.BlockSpec((1, tq, LANES), lambda bi, i: (bi, i, 0))
    prev_spec = pl.BlockSpec((1, win, LANES), lambda bi, i: (bi, jnp.maximum(i * ratio - 1, 0), 0))
    return pl.pallas_call(
        functools.partial(_window_attn_kernel, tq=tq, groups=groups, kv_heads=kv_heads),
        out_shape=jax.ShapeDtypeStruct((b, t, nq), BF16),
        grid=(b, t // tq),
        in_specs=[pl.BlockSpec((1, tq, nq), lambda bi, i: (bi, i, 0)),
                  prev_spec, cur_spec, prev_spec, cur_spec,
                  pl.BlockSpec(memory_space=pltpu.SMEM)],
        out_specs=pl.BlockSpec((1, tq, nq), lambda bi, i: (bi, i, 0)),
        compiler_params=_cparams("parallel", "parallel"),
        name="mixer_b_window_attention",
    )(q, k, k, v, v, sinks)


def _window_decode_kernel(q_ref, kp_ref, kn_ref, vp_ref, vn_ref, sink_ref, o_ref,
                          *, past_len, groups, kv_heads):
    q = q_ref[0]
    tn = q.shape[0]
    win = kp_ref.shape[1]
    keys = 2 * LANES
    kc = _pad_rows(jnp.concatenate([kp_ref[0], kn_ref[0]], axis=0), keys)
    vc = _pad_rows(jnp.concatenate([vp_ref[0], vn_ref[0]], axis=0), keys)
    width = groups * HEAD_DIM
    rows = groups * tn
    shape = (rows, keys)
    col = lax.broadcasted_iota(jnp.int32, shape, 1)
    q_chunk = (past_len + lax.broadcasted_iota(jnp.int32, shape, 0) % tn) // CHUNK
    k_pos = past_len - win + col
    k_chunk = k_pos // CHUNK
    mask = ((k_chunk <= q_chunk) & (k_chunk >= q_chunk - WIN_CHUNKS) & (k_pos >= 0)
            & (col < win + tn))
    outs = []
    for n in range(kv_heads):
        outs.append(_sink_group_attention(q[:, n * width:(n + 1) * width],
                                          _tile_kv_head(kc, n, groups), _tile_kv_head(vc, n, groups),
                                          mask, sink_ref[n], groups))
    o_ref[0] = jnp.concatenate(outs, axis=1).astype(BF16)


def _window_decode_call(q, k_new, v_new, past_k, past_v, sink_cols, past_len):
    b, tn, nq = q.shape
    win = past_k.shape[1]
    kv_heads = sink_cols.shape[0]
    groups = nq // (kv_heads * HEAD_DIM)
    assert win + tn <= 2 * LANES
    new_spec = pl.BlockSpec((1, tn, LANES), lambda bi: (bi, 0, 0))
    past_spec = pl.BlockSpec((1, win, LANES), lambda bi: (bi, 0, 0))
    return pl.pallas_call(
        functools.partial(_window_decode_kernel, past_len=past_len, groups=groups, kv_heads=kv_heads),
        out_shape=jax.ShapeDtypeStruct((b, tn, nq), BF16),
        grid=(b,),
        in_specs=[pl.BlockSpec((1, tn, nq), lambda bi: (bi, 0, 0)),
                  past_spec, new_spec, past_spec, new_spec,
                  pl.BlockSpec(sink_cols.shape, lambda bi: (0, 0, 0))],
        out_specs=pl.BlockSpec((1, tn, nq), lambda bi: (bi, 0, 0)),
        compiler_params=_cparams("parallel"),
        name="mixer_b_decode_attention",
    )(q, past_k, k_new, past_v, v_new, sink_cols)


def _route_top2(logits, experts):
    lane = lax.broadcasted_iota(jnp.int32, logits.shape, 1).astype(F32)
    big = float(LANES)
    lg = jnp.where(lane < experts, logits, NEG_INF)
    m1 = jnp.max(lg, axis=1, keepdims=True)
    i1 = jnp.min(jnp.where(lg == m1, lane, big), axis=1, keepdims=True)
    lg2 = jnp.where(lane == i1, NEG_INF, lg)
    m2 = jnp.max(lg2, axis=1, keepdims=True)
    i2 = jnp.min(jnp.where(lg2 == m2, lane, big), axis=1, keepdims=True)
    e2 = jnp.exp(m2 - m1)
    den = 1.0 + e2
    return jnp.where(lane == i1, 1.0 / den, 0.0) + jnp.where(lane == i2, e2 / den, 0.0)


def _moe_kernel(h_ref, sh_ref, sc_ref, gate_ref, g_ref, rhi_ref, rlo_ref, wg_ref, wu_ref, wd_ref,
                out_ref, u_s, rnk_s, gat_s, rnkt_s, cnt_s, xg_s, ye_s, *, nb, tb, experts):
    e = pl.program_id(1)
    k = pl.program_id(2)
    last_k = pl.num_programs(2) - 1
    d = h_ref.shape[-1]
    tm = nb * tb
    r2 = min(2 * MXU_DIM, tm)
    r1 = r2 // 2
    r0 = r2 // 4
    kc = min(2 * MXU_DIM, tm)
    rc = min(MXU_DIM, tm)
    assert nb == 1 or rc == tm

    @pl.when((e == 0) & (k == 0))
    def _route():
        row_i = lax.broadcasted_iota(jnp.int32, (rc, rc), 0)
        col_i = lax.broadcasted_iota(jnp.int32, (rc, rc), 1)
        below = jnp.where(col_i < row_i, 1.0, 0.0).astype(BF16)

        def chunk(r, seen):
            rows = pl.ds(pl.multiple_of(r * rc, rc), rc)
            hx = h_ref[:, rows, :] if nb == 1 else h_ref[...]
            u = _modulate(hx, g_ref[...], sh_ref[...], sc_ref[...]).reshape(rc, d)
            ub = u.astype(BF16)
            ulo = (u - ub.astype(F32)).astype(BF16)
            rhi = rhi_ref[...]
            logits = (jnp.dot(ub, rhi, preferred_element_type=F32)
                      + jnp.dot(ub, rlo_ref[...], preferred_element_type=F32)
                      + jnp.dot(ulo, rhi, preferred_element_type=F32))
            dg = _route_top2(logits, experts)
            sel = dg > 0.0
            sel_f = jnp.where(sel, 1.0, 0.0)
            rank = jnp.dot(below, sel_f.astype(BF16), preferred_element_type=F32) + seen
            u_s[rows, :] = ub
            gat_s[rows, :] = dg
            rnk_s[rows, :] = jnp.where(sel, rank, -1.0)
            return seen + jnp.sum(sel_f, axis=0, keepdims=True)

        counts = lax.fori_loop(0, tm // rc, chunk, jnp.zeros((1, LANES), F32))
        for x in range(experts):
            cnt_s[x] = counts[0, x].astype(jnp.int32)
        rnkt_s[...] = rnk_s[...].T[:rnkt_s.shape[0]]
        out_ref[...] = jnp.zeros(out_ref.shape, F32)

    count = cnt_s[e]
    n2 = (count + r2 - 1) // r2

    @pl.when(k == 0)
    def _gather():
        def tile(t, carry):
            base = pl.multiple_of(t * r2, r2)
            want = (base + lax.broadcasted_iota(jnp.int32, (r2, kc), 0)).astype(F32)
            acc = jnp.zeros((r2, d), F32)
            for c in range(tm // kc):
                pos = rnkt_s[pl.ds(e, 1), c * kc:(c + 1) * kc]
                pick = jnp.where(pos == want, 1.0, 0.0).astype(BF16)
                acc = acc + jnp.dot(pick, u_s[c * kc:(c + 1) * kc, :], preferred_element_type=F32)
            xg_s[pl.ds(base, r2), :] = acc.astype(BF16)
            ye_s[pl.ds(base, r2), :] = jnp.zeros((r2, d), F32)
            return carry

        lax.fori_loop(0, n2, tile, 0)

    def expert_rows(base, rows):
        x = xg_s[pl.ds(base, rows), :]
        g = jnp.dot(x, wg_ref[0], preferred_element_type=F32)
        up = jnp.dot(x, wu_ref[0], preferred_element_type=F32)
        a = ((g * _sigmoid(g)) * up).astype(BF16)
        ye_s[pl.ds(base, rows), :] += jnp.dot(a, wd_ref[0], preferred_element_type=F32)

    n_full = count // r2
    rest = count - n_full * r2
    n_full = n_full + (rest > r1).astype(jnp.int32)

    def full_tile(t, carry):
        expert_rows(pl.multiple_of(t * r2, r2), r2)
        return carry

    lax.fori_loop(0, n_full, full_tile, 0)

    @pl.when((rest > r0) & (rest <= r1))
    def _half_tile():
        expert_rows(pl.multiple_of(n_full * r2, r2), r1)

    @pl.when((rest > 0) & (rest <= r0))
    def _quarter_tile():
        expert_rows(pl.multiple_of(n_full * r2, r2), r0)

    @pl.when(k == last_k)
    def _scatter():
        lane = lax.broadcasted_iota(jnp.int32, (tm, LANES), 1)
        pos = jnp.sum(jnp.where(lane == e, rnk_s[...], 0.0), axis=1, keepdims=True)
        gate = jnp.sum(jnp.where(lane == e, gat_s[...], 0.0), axis=1, keepdims=True)

        def tile(t, carry):
            base = pl.multiple_of(t * r2, r2)
            want = (base + lax.broadcasted_iota(jnp.int32, (tm, r2), 1)).astype(F32)
            place = jnp.where(pos == want, gate, 0.0).astype(BF16)
            y = ye_s[pl.ds(base, r2), :].astype(BF16)
            out_ref[...] += jnp.dot(place, y, preferred_element_type=F32).reshape(nb, tb, d)
            return carry

        lax.fori_loop(0, n2, tile, 0)

    @pl.when((e == pl.num_programs(1) - 1) & (k == last_k))
    def _residual():
        out_ref[...] = h_ref[...] + gate_ref[...] * out_ref[...]


def _moe_call(h, shift, scale, gate, g, r_hi, r_lo, w_gu, w_d, experts, rows, fc):
    b, t, d = h.shape
    ff = w_d.shape[1]
    assert ff % fc == 0
    nk = ff // fc
    nb, tb = _row_tiling(b, t, rows)
    nt = t // tb
    tm = nb * tb
    xmap = lambda i, e, k: (i // nt, i % nt, 0)
    mmap = lambda i, e, k: (i // nt, 0, 0)
    mspec = pl.BlockSpec((nb, 1, d), mmap)
    sub = 8
    assert experts <= sub and tm % min(2 * MXU_DIM, tm) == 0
    return pl.pallas_call(
        functools.partial(_moe_kernel, nb=nb, tb=tb, experts=experts),
        out_shape=jax.ShapeDtypeStruct(h.shape, F32),
        grid=(b // nb * nt, experts, nk),
        in_specs=[pl.BlockSpec((nb, tb, d), xmap, pipeline_mode=pl.Buffered(1)),
                  mspec, mspec, mspec,
                  _const_spec((1, d)),
                  _const_spec((d, LANES)),
                  _const_spec((d, LANES)),
                  pl.BlockSpec((1, d, fc), lambda i, e, k: (e, 0, k)),
                  pl.BlockSpec((1, d, fc), lambda i, e, k: (e, 0, k + nk)),
                  pl.BlockSpec((1, fc, d), lambda i, e, k: (e, k, 0))],
        out_specs=pl.BlockSpec((nb, tb, d), xmap),
        scratch_shapes=[pltpu.VMEM((tm, d), BF16),
                        pltpu.VMEM((tm, LANES), F32),
                        pltpu.VMEM((tm, LANES), F32),
                        pltpu.VMEM((sub, tm), F32),
                        pltpu.SMEM((sub,), jnp.int32),
                        pltpu.VMEM((tm, d), BF16),
                        pltpu.VMEM((tm, d), F32)],
        compiler_params=_cparams("parallel", "arbitrary", "arbitrary"),
        name="expert_swiglu",
    )(h, shift, scale, gate, g, r_hi, r_lo, w_gu, w_gu, w_d)


def _rope_tables(pos):
    half = ROT_DIM // 2
    inv = ROPE_THETA ** (-(jnp.arange(half, dtype=F32) * 2.0 / ROT_DIM))
    ang = pos.astype(F32)[:, None] * inv[None, :]
    cos, sin = jnp.cos(ang), jnp.sin(ang)
    t = pos.shape[0]
    zeros = lambda n: jnp.zeros((t, n), F32)
    c = jnp.concatenate([cos, cos, jnp.ones((t, HEAD_DIM - ROT_DIM), F32)], axis=1)
    s1 = jnp.concatenate([-sin, zeros(HEAD_DIM - half)], axis=1)
    s2 = jnp.concatenate([zeros(half), sin, zeros(HEAD_DIM - ROT_DIM)], axis=1)
    rep = LANES // HEAD_DIM
    return tuple(jnp.tile(a, (1, rep)) for a in (c, s1, s2))


def _mod_rows(mod, lo, hi):
    d = mod.shape[1]
    return mod[lo:hi].reshape(hi - lo, 1, d)


def _trunk(x, mods, kv_mods, w, past, rows):
    b, t, d = x.shape
    past_len = 0 if past is None else past[0].shape[2]
    pos = past_len + jnp.arange(t, dtype=jnp.int32)
    tables = _rope_tables(pos)
    heads_a = w["a_heads"]
    big = t >= rows

    m0 = mods[0]
    lam_init = 0.8 - 0.6 * math.exp(-0.3 * 0)
    qh, kh, vh, a_k, a_v = _qkv_a_call(x, m0[0], m0[1], w["norm_g"][0][0], w["a_w_qkv"], w["gsum"],
                                       w["a_q_gain"], w["a_k_gain"], tables, heads_a, rows)
    if past is None:
        o = _diff_attn_call(qh, kh, vh, w["a_lambda"], w["a_subln"], lam_init, 256, 512, 4)
    else:
        pk = past[0][0].reshape(b, past_len, heads_a * LANES)
        pv = past[1][0].reshape(b, past_len, heads_a * LANES)
        o = _diff_decode_call(qh, kh, vh, pk, pv, w["a_lambda"], w["a_subln"], lam_init)
    h = _proj_res_call(o, x, m0[2], w["a_w_o"], rows)
    h = _ffn_call(h, m0[3], m0[4], m0[5], w["norm_g"][0][1], w["f_w_gu"], w["f_w_d"], rows)

    m1 = mods[1]
    q, k_sh, v_sh = _qkv_b_call(h, (m1[0], m1[1], w["norm_g"][1][0]),
                                (kv_mods[0], kv_mods[1], w["kv_norm_g"]),
                                w["b_w_q"], w["kv_w"], w["gsum"], w["b_q_gain"], w["kv_k_gain"],
                                tables, rows)
    if past is None:
        o = _window_attn_call(q, k_sh, v_sh, w["sinks"], w["kv_heads"], 256)
    else:
        win = past[2].shape[1]
        o = _window_decode_call(q, k_sh, v_sh, past[2].reshape(b, win, LANES),
                                past[3].reshape(b, win, LANES), w["sink_cols_decode"], past_len)
    h = _proj_res_call(o, h, m1[2], w["b_w_o"], rows)
    h = _moe_call(h, m1[3], m1[4], m1[5], w["norm_g"][1][1], w["r_hi"], w["r_lo"],
                  w["m_w_gu"], w["m_w_d"], w["experts"], min(2048, t) if big else rows, 512)
    return h, a_k, a_v, k_sh, v_sh


def kernel(x_prompt, x_sample, c_prompt, c_sample, cache_a_k, cache_a_v, cache_b_k, cache_b_v,
           w_mod, b_mod, norm_g, a_w_qkv, a_q_norm, a_k_norm, a_lambda, a_subln, a_w_o,
           kv_norm_g, kv_w_mod, kv_b_mod, kv_w, kv_k_norm, b_w_q, b_q_norm, b_sinks, b_w_o,
           f_w_gu, f_w_d, m_router, m_w_gu, m_w_d):
    bp, t, d = x_prompt.shape
    bs, tn, _ = x_sample.shape
    assert w_mod.shape[0] == 2 and a_w_qkv.shape[0] == 1 and b_w_q.shape[0] == 1
    heads_a = cache_a_k.shape[3]
    kv_heads = cache_b_k.shape[2]
    heads_b = b_sinks.shape[1]
    groups = heads_b // kv_heads
    experts = m_router.shape[2]

    c_all = jnp.concatenate([c_prompt, c_sample], axis=0)
    mods = [_mod_call(c_all, w_mod[l], b_mod[l]) for l in range(2)]
    kv_mod = _mod_call(c_all, kv_w_mod, kv_b_mod)

    def split(mod, n, lo, hi):
        m = mod[lo:hi].reshape(hi - lo, n, 1, d)
        return [m[:, j] for j in range(n)]

    group = jnp.arange(MXU_DIM) // HEAD_DIM
    router = jnp.pad(m_router[0], ((0, 0), (0, LANES - experts)))
    r_hi = router.astype(BF16)

    def sink_cols(rows_per_head):
        s = b_sinks[0].astype(F32).reshape(kv_heads, groups, 1)
        return jnp.broadcast_to(s, (kv_heads, groups, rows_per_head)).reshape(
            kv_heads, groups * rows_per_head, 1)

    w = dict(
        a_heads=heads_a, experts=experts,
        norm_g=[[norm_g[l, j].reshape(1, d) for j in range(2)] for l in range(2)],
        gsum=(group[:, None] == group[None, :]).astype(BF16),
        a_w_qkv=a_w_qkv[0].astype(BF16),
        a_q_gain=jnp.tile(a_q_norm[0].reshape(1, LANES), (1, heads_a)),
        a_k_gain=jnp.tile(a_k_norm[0].reshape(1, LANES), (1, heads_a)),
        a_lambda=a_lambda[0].astype(F32),
        a_subln=a_subln[0].reshape(1, LANES),
        a_w_o=a_w_o[0].astype(BF16),
        f_w_gu=f_w_gu[0].astype(BF16), f_w_d=f_w_d[0].astype(BF16),
        kv_norm_g=kv_norm_g.reshape(1, d),
        kv_w=kv_w.astype(BF16),
        kv_k_gain=jnp.tile(kv_k_norm.reshape(1, HEAD_DIM), (1, LANES // HEAD_DIM)),
        b_w_q=b_w_q[0].astype(BF16),
        b_q_gain=jnp.tile(b_q_norm[0].reshape(1, HEAD_DIM), (1, heads_b)),
        b_w_o=b_w_o[0].astype(BF16),
        sinks=b_sinks[0].astype(F32), kv_heads=kv_heads, sink_cols_decode=sink_cols(tn),
        r_hi=r_hi, r_lo=(router - r_hi.astype(F32)).astype(BF16),
        m_w_gu=m_w_gu[0].astype(BF16), m_w_d=m_w_d[0].astype(BF16),
    )

    rows = 512
    y_p, ak_p, av_p, bk_p, bv_p = _trunk(
        x_prompt, [split(m, 6, 0, bp) for m in mods], split(kv_mod, 2, 0, bp), w, None, rows)
    y_s, ak_s, av_s, bk_s, bv_s = _trunk(
        x_sample, [split(m, 6, bp, bp + bs) for m in mods], split(kv_mod, 2, bp, bp + bs), w,
        (cache_a_k, cache_a_v, cache_b_k, cache_b_v), rows)

    win = min(WIN_CHUNKS * CHUNK, t)
    a_k_shape = lambda b_, t_: (1, b_, t_, heads_a, 2, HEAD_DIM)
    a_v_shape = lambda b_, t_: (1, b_, t_, heads_a, 2 * HEAD_DIM)
    b_shape = lambda b_, t_: (b_, t_, kv_heads, HEAD_DIM)
    return (y_p, y_s,
            ak_p.reshape(a_k_shape(bp, t)), av_p.reshape(a_v_shape(bp, t)),
            bk_p[:, -win:].reshape(b_shape(bp, win)), bv_p[:, -win:].reshape(b_shape(bp, win)),
            ak_s.reshape(a_k_shape(bs, tn)), av_s.reshape(a_v_shape(bs, tn)),
            bk_s.reshape(b_shape(bs, tn)), bv_s.reshape(b_shape(bs, tn)))
```

```python
import functools
import math

import jax
import jax.numpy as jnp
from jax import lax
from jax.experimental import pallas as pl
from jax.experimental.pallas import tpu as pltpu

CHUNK = 64
WIN_CHUNKS = 2
ROT_DIM = 16
ROPE_THETA = 500000.0
EPS = 1e-6
HEAD_DIM = 64
N_ROUTED = 2
MOE_FF_CHUNK = 512

LANES = 128
MXU_DIM = 256
VMEM_LIMIT_BYTES = 56 * 1024 * 1024

F32 = jnp.float32
BF16 = jnp.bfloat16
NEG_INF = float("-inf")


def _cparams(*sem):
    return pltpu.CompilerParams(dimension_semantics=sem, vmem_limit_bytes=VMEM_LIMIT_BYTES)


def _row_tiling(batch, seq, rows):
    if seq >= rows:
        assert seq % rows == 0
        return 1, rows
    nb = max(1, min(batch, rows // seq))
    while batch % nb:
        nb -= 1
    return nb, seq


def _const_spec(shape):
    zeros = (0,) * len(shape)
    return pl.BlockSpec(shape, lambda *_: zeros, pipeline_mode=pl.Buffered(1))


def _sigmoid(x):
    return 1.0 / (1.0 + jnp.exp(-x))


def _rms_scale(x):
    return x * lax.rsqrt(jnp.mean(x * x, axis=-1, keepdims=True) + EPS)


def _modulate(x, g, shift, scale):
    return (_rms_scale(x) * g) * (1.0 + scale) + shift


def _group_norm_rope(x, gsum_ref, gain, cos, s1, s2):
    width = x.shape[1]
    slab = MXU_DIM if width % MXU_DIM == 0 else LANES
    gsum = gsum_ref[:slab, :slab]
    pieces = []
    for j in range(width // slab):
        xs = x[:, j * slab:(j + 1) * slab]
        ss = jnp.dot((xs * xs).astype(BF16), gsum, preferred_element_type=F32)
        y = xs * lax.rsqrt(ss * (1.0 / HEAD_DIM) + EPS) * gain[:, j * slab:(j + 1) * slab]
        for t in range(slab // LANES):
            yt = y[:, t * LANES:(t + 1) * LANES]
            pieces.append(yt * cos + pltpu.roll(yt, LANES - ROT_DIM // 2, 1) * s1
                          + pltpu.roll(yt, ROT_DIM // 2, 1) * s2)
    return pieces


def _tile_rows(t, nb):
    return t if nb == 1 else jnp.concatenate([t] * nb, axis=0)


def _mod_kernel(c_ref, w_ref, b_ref, o_ref):
    c = c_ref[...]
    sc = (c * _sigmoid(c)).astype(BF16)
    o_ref[...] = jnp.dot(sc, w_ref[...].astype(BF16), preferred_element_type=F32) + b_ref[...]


def _mod_call(c, w, b):
    rows, d = c.shape
    n = w.shape[1]
    tn = 2048
    assert n % tn == 0
    return pl.pallas_call(
        _mod_kernel,
        out_shape=jax.ShapeDtypeStruct((rows, n), F32),
        grid=(n // tn,),
        in_specs=[pl.BlockSpec((rows, d), lambda j: (0, 0)),
                  pl.BlockSpec((d, tn), lambda j: (0, j)),
                  pl.BlockSpec((1, tn), lambda j: (0, j))],
        out_specs=pl.BlockSpec((rows, tn), lambda j: (0, j)),
        compiler_params=_cparams("parallel"),
        name="adaln_mod",
    )(c, w, b.reshape(1, n))


def _qkv_a_kernel(x_ref, sh_ref, sc_ref, g_ref, w_ref, gsum_ref, qg_ref, kg_ref,
                  cos_ref, s1_ref, s2_ref, qh_ref, kh_ref, vh_ref, ak_ref, av_ref,
                  *, nb, tb, heads):
    x = x_ref[...]
    d = x.shape[-1]
    u = _modulate(x, g_ref[...], sh_ref[...], sc_ref[...]).reshape(nb * tb, d).astype(BF16)
    qkv = jnp.dot(u, w_ref[...], preferred_element_type=F32)
    nqk = heads * 2 * HEAD_DIM
    cos, s1, s2 = (_tile_rows(r[...], nb) for r in (cos_ref, s1_ref, s2_ref))
    q_p = _group_norm_rope(qkv[:, :nqk], gsum_ref, qg_ref[...], cos, s1, s2)
    k_p = _group_norm_rope(qkv[:, nqk:2 * nqk], gsum_ref, kg_ref[...], cos, s1, s2)
    qk_scale = HEAD_DIM ** -0.5 * math.log2(math.e)
    for h in range(heads):
        cols = slice(h * LANES, (h + 1) * LANES)
        v_h = qkv[:, 2 * nqk + h * LANES:2 * nqk + (h + 1) * LANES].reshape(nb, tb, LANES)
        k_h = k_p[h].reshape(nb, tb, LANES)
        qh_ref[:, h] = (q_p[h] * qk_scale).reshape(nb, tb, LANES).astype(BF16)
        kh_ref[:, h] = k_h.astype(BF16)
        vh_ref[:, h] = v_h.astype(BF16)
        ak_ref[:, :, cols] = k_h
        av_ref[:, :, cols] = v_h


def _qkv_a_call(x, shift, scale, g, w, gsum, qg, kg, tables, heads, rows):
    b, t, d = x.shape
    nb, tb = _row_tiling(b, t, rows)
    nt = t // tb
    n = w.shape[1]
    xmap = lambda i: (i // nt, i % nt, 0)
    mmap = lambda i: (i // nt, 0, 0)
    tmap = lambda i: (i % nt, 0)
    hmap = lambda i: (i // nt, 0, i % nt, 0)
    head_shape = jax.ShapeDtypeStruct((b, heads, t, LANES), BF16)
    flat_shape = jax.ShapeDtypeStruct((b, t, heads * LANES), F32)
    return pl.pallas_call(
        functools.partial(_qkv_a_kernel, nb=nb, tb=tb, heads=heads),
        out_shape=(head_shape, head_shape, head_shape, flat_shape, flat_shape),
        grid=(b // nb * nt,),
        in_specs=[pl.BlockSpec((nb, tb, d), xmap),
                  pl.BlockSpec((nb, 1, d), mmap),
                  pl.BlockSpec((nb, 1, d), mmap),
                  _const_spec((1, d)),
                  _const_spec((d, n)),
                  _const_spec((MXU_DIM, MXU_DIM)),
                  _const_spec((1, heads * LANES)),
                  _const_spec((1, heads * LANES)),
                  pl.BlockSpec((tb, LANES), tmap),
                  pl.BlockSpec((tb, LANES), tmap),
                  pl.BlockSpec((tb, LANES), tmap)],
        out_specs=(pl.BlockSpec((nb, heads, tb, LANES), hmap),) * 3
                  + (pl.BlockSpec((nb, tb, heads * LANES), xmap),) * 2,
        compiler_params=_cparams("parallel"),
        name="mixer_a_qkv",
    )(x, shift, scale, g, w, gsum, qg, kg, *tables)


def _split_halves(q):
    lane = lax.broadcasted_iota(jnp.int32, q.shape, 1)
    zero = jnp.zeros_like(q)
    return jnp.concatenate([jnp.where(lane < HEAD_DIM, q, zero),
                            jnp.where(lane >= HEAD_DIM, q, zero)], axis=0)


def _diff_lambda(lam_ref, lam_init):
    lv = lam_ref[...]
    t1 = jnp.sum(lv[0:1] * lv[1:2], axis=1, keepdims=True)
    t2 = jnp.sum(lv[2:3] * lv[3:4], axis=1, keepdims=True)
    return jnp.exp(t1) - jnp.exp(t2) + lam_init


def _diff_finish(o0, o1, lam_ref, g_ref, lam_init):
    o = o0 - _diff_lambda(lam_ref, lam_init) * o1
    return _rms_scale(o) * g_ref[...] * (1.0 - lam_init)


def _diff_attn_kernel(q_ref, k_ref, v_ref, lam_ref, g_ref, o_ref, s_ref, m_ref, l_ref, acc_ref,
                      *, tq, tk, hp, lam_init):
    qi = pl.program_id(2)
    n_full = (qi * tq) // tk
    qqs = [_split_halves(q_ref[0, h]) for h in range(hp)]
    m_ref[...] = jnp.full(m_ref.shape, NEG_INF, F32)
    l_ref[...] = jnp.zeros(l_ref.shape, F32)
    acc_ref[...] = jnp.zeros(acc_ref.shape, F32)

    def scores(h, j):
        kj = k_ref[0, h, pl.ds(pl.multiple_of(j * tk, tk), tk), :]
        return lax.dot_general(qqs[h], kj, (((1,), (1,)), ((), ())), preferred_element_type=F32)

    def absorb(h, j, s):
        vj = v_ref[0, h, pl.ds(pl.multiple_of(j * tk, tk), tk), :]
        m_prev = m_ref[h]
        m_new = jnp.maximum(m_prev, jnp.max(s, axis=1, keepdims=True))
        alpha = jnp.exp2(m_prev - m_new)
        chunks = [jnp.exp2(s[:, c * LANES:(c + 1) * LANES] - m_new) for c in range(tk // LANES)]
        l_ref[h] = alpha * l_ref[h] + functools.reduce(lambda a, b: a + b, chunks)
        p = jnp.concatenate(chunks, axis=1).astype(BF16)
        acc_ref[h] = alpha * acc_ref[h] + jnp.dot(p, vj, preferred_element_type=F32)
        m_ref[h] = m_new

    for h in range(hp):
        s_ref[h] = scores(h, 0)

    def body(j, carry):
        for h in range(hp):
            s = s_ref[h]
            s_ref[h] = scores(h, j + 1)
            absorb(h, j, s)
        return carry

    lax.fori_loop(0, n_full, body, 0)
    q_chunk = (qi * tq + lax.broadcasted_iota(jnp.int32, (2 * tq, tk), 0) % tq) // CHUNK
    k_chunk = (n_full * tk + lax.broadcasted_iota(jnp.int32, (2 * tq, tk), 1)) // CHUNK
    visible = k_chunk <= q_chunk
    for h in range(hp):
        absorb(h, n_full, jnp.where(visible, s_ref[h], NEG_INF))

    for h in range(hp):
        o = acc_ref[h] / jnp.sum(l_ref[h], axis=1, keepdims=True)
        o_ref[0, :, h * LANES:(h + 1) * LANES] = _diff_finish(
            o[:tq], o[tq:], lam_ref, g_ref, lam_init).astype(BF16)


def _diff_attn_call(qh, kh, vh, lam_vec, subln, lam_init, tq, tk, hp):
    b, heads, t, _ = qh.shape
    assert t % tk == 0 and tk % tq == 0 and tq % CHUNK == 0 and heads % hp == 0
    return pl.pallas_call(
        functools.partial(_diff_attn_kernel, tq=tq, tk=tk, hp=hp, lam_init=lam_init),
        out_shape=jax.ShapeDtypeStruct((b, t, heads * LANES), BF16),
        grid=(b, heads // hp, t // tq),
        in_specs=[pl.BlockSpec((1, hp, tq, LANES), lambda bi, h, i: (bi, h, i, 0)),
                  pl.BlockSpec((1, hp, t, LANES), lambda bi, h, i: (bi, h, 0, 0), pipeline_mode=pl.Buffered(1)),
                  pl.BlockSpec((1, hp, t, LANES), lambda bi, h, i: (bi, h, 0, 0), pipeline_mode=pl.Buffered(1)),
                  pl.BlockSpec(lam_vec.shape, lambda bi, h, i: (0, 0)),
                  pl.BlockSpec((1, LANES), lambda bi, h, i: (0, 0))],
        out_specs=pl.BlockSpec((1, tq, hp * LANES), lambda bi, h, i: (bi, i, h)),
        scratch_shapes=[pltpu.VMEM((hp, 2 * tq, tk), F32)] + [pltpu.VMEM((hp, 2 * tq, LANES), F32)] * 3,
        compiler_params=_cparams("parallel", "parallel", "arbitrary"),
        name="mixer_a_attention",
    )(qh, kh, vh, lam_vec, subln)


def _pad_rows(a, rows):
    return jnp.concatenate([a, jnp.zeros((rows - a.shape[0], a.shape[1]), a.dtype)], axis=0)


def _diff_decode_kernel(q_ref, kp_ref, vp_ref, kn_ref, vn_ref, lam_ref, g_ref, o_ref,
                        *, past, lam_init):
    tn = q_ref.shape[2]
    qq = _split_halves(q_ref[0, 0])
    kp = kp_ref[0].astype(BF16)
    vp = vp_ref[0].astype(BF16)
    kn = _pad_rows(kn_ref[0, 0], LANES)
    vn = _pad_rows(vn_ref[0, 0], LANES)
    dims = (((1,), (1,)), ((), ()))
    s_p = lax.dot_general(qq, kp, dims, preferred_element_type=F32)
    s_n = lax.dot_general(qq, kn, dims, preferred_element_type=F32)

    def visible(shape, k_off, k_count):
        q_pos = past + lax.broadcasted_iota(jnp.int32, shape, 0) % tn
        col = lax.broadcasted_iota(jnp.int32, shape, 1)
        return ((k_off + col) // CHUNK <= q_pos // CHUNK) & (col < k_count)

    s_p = jnp.where(visible(s_p.shape, 0, past), s_p, NEG_INF)
    s_n = jnp.where(visible(s_n.shape, past, tn), s_n, NEG_INF)
    m = jnp.maximum(jnp.max(s_p, axis=1, keepdims=True), jnp.max(s_n, axis=1, keepdims=True))
    e_p = jnp.exp2(s_p - m)
    e_n = jnp.exp2(s_n - m)
    l = jnp.sum(e_p, axis=1, keepdims=True) + jnp.sum(e_n, axis=1, keepdims=True)
    acc = (jnp.dot(e_p.astype(BF16), vp, preferred_element_type=F32)
           + jnp.dot(e_n.astype(BF16), vn, preferred_element_type=F32))
    o = acc / l
    o_ref[0] = _diff_finish(o[:tn], o[tn:], lam_ref, g_ref, lam_init).astype(BF16)


def _diff_decode_call(qh, kh, vh, past_k, past_v, lam_vec, subln, lam_init):
    b, heads, tn, _ = qh.shape
    past = past_k.shape[1]
    new_spec = pl.BlockSpec((1, 1, tn, LANES), lambda bi, h: (bi, h, 0, 0))
    past_spec = pl.BlockSpec((1, past, LANES), lambda bi, h: (bi, 0, h))
    return pl.pallas_call(
        functools.partial(_diff_decode_kernel, past=past, lam_init=lam_init),
        out_shape=jax.ShapeDtypeStruct((b, tn, heads * LANES), BF16),
        grid=(b, heads),
        in_specs=[new_spec, past_spec, past_spec, new_spec, new_spec,
                  pl.BlockSpec(lam_vec.shape, lambda bi, h: (0, 0)),
                  pl.BlockSpec((1, LANES), lambda bi, h: (0, 0))],
        out_specs=pl.BlockSpec((1, tn, LANES), lambda bi, h: (bi, 0, h)),
        compiler_params=_cparams("parallel", "parallel"),
        name="mixer_a_decode_attention",
    )(qh, past_k, past_v, kh, vh, lam_vec, subln)


def _proj_res_kernel(o_ref, h_ref, gate_ref, w_ref, out_ref, *, nb, tb):
    k = o_ref.shape[-1]
    y = jnp.dot(o_ref[...].reshape(nb * tb, k), w_ref[...], preferred_element_type=F32)
    out_ref[...] = h_ref[...] + gate_ref[...] * y.reshape(nb, tb, y.shape[-1])


def _proj_res_call(o, h, gate, w, rows):
    b, t, d = h.shape
    k = o.shape[-1]
    nb, tb = _row_tiling(b, t, rows)
    nt = t // tb
    xmap = lambda i: (i // nt, i % nt, 0)
    return pl.pallas_call(
        functools.partial(_proj_res_kernel, nb=nb, tb=tb),
        out_shape=jax.ShapeDtypeStruct(h.shape, F32),
        grid=(b // nb * nt,),
        in_specs=[pl.BlockSpec((nb, tb, k), xmap),
                  pl.BlockSpec((nb, tb, d), xmap),
                  pl.BlockSpec((nb, 1, d), lambda i: (i // nt, 0, 0)),
                  _const_spec((k, d))],
        out_specs=pl.BlockSpec((nb, tb, d), xmap),
        compiler_params=_cparams("parallel"),
        name="out_proj_residual",
    )(o, h, gate, w)


def _ffn_kernel(o_ref, x_ref, ogate_ref, wo_ref, sh_ref, sc_ref, gate_ref, g_ref, wgu_ref, wd_ref,
                out_ref, *, nb, tb, ff, fc):
    d = x_ref.shape[-1]
    y = jnp.dot(o_ref[...].reshape(nb * tb, o_ref.shape[-1]), wo_ref[...],
                preferred_element_type=F32)
    h = x_ref[...] + ogate_ref[...] * y.reshape(nb, tb, d)
    u = _modulate(h, g_ref[...], sh_ref[...], sc_ref[...]).reshape(nb * tb, d).astype(BF16)
    acc = jnp.zeros((nb * tb, d), F32)
    for c in range(ff // fc):
        g = jnp.dot(u, wgu_ref[:, c * fc:(c + 1) * fc], preferred_element_type=F32)
        up = jnp.dot(u, wgu_ref[:, ff + c * fc:ff + (c + 1) * fc], preferred_element_type=F32)
        a = ((g * _sigmoid(g)) * up).astype(BF16)
        acc = acc + jnp.dot(a, wd_ref[c * fc:(c + 1) * fc, :], preferred_element_type=F32)
    out_ref[...] = h + gate_ref[...] * acc.reshape(nb, tb, d)


def _ffn_call(o, x, o_gate, w_o, shift, scale, gate, g, w_gu, w_d, rows):
    b, t, d = x.shape
    k = o.shape[-1]
    ff = w_d.shape[0]
    fc = ff // 2 if (ff // 2) % LANES == 0 else ff
    nb, tb = _row_tiling(b, t, rows)
    nt = t // tb
    xmap = lambda i: (i // nt, i % nt, 0)
    mspec = pl.BlockSpec((nb, 1, d), lambda i: (i // nt, 0, 0))
    return pl.pallas_call(
        functools.partial(_ffn_kernel, nb=nb, tb=tb, ff=ff, fc=fc),
        out_shape=jax.ShapeDtypeStruct(x.shape, F32),
        grid=(b // nb * nt,),
        in_specs=[pl.BlockSpec((nb, tb, k), xmap),
                  pl.BlockSpec((nb, tb, d), xmap),
                  mspec,
                  _const_spec((k, d)),
                  mspec, mspec, mspec,
                  _const_spec((1, d)),
                  _const_spec((d, 2 * ff)),
                  _const_spec((ff, d))],
        out_specs=pl.BlockSpec((nb, tb, d), xmap),
        compiler_params=_cparams("parallel"),
        name="out_proj_dense_swiglu",
    )(o, x, o_gate, w_o, shift, scale, gate, g, w_gu, w_d)


def _qkv_b_kernel(x_ref, shq_ref, scq_ref, gq_ref, shkv_ref, sckv_ref, gkv_ref, wq_ref, wkv_ref,
                  gsum_ref, qn_ref, kn_ref, cos_ref, s1_ref, s2_ref, q_ref, k_ref, v_ref,
                  *, nb, tb):
    x = x_ref[...]
    d = x.shape[-1]
    y = _rms_scale(x)
    uq = ((y * gq_ref[...]) * (1.0 + scq_ref[...]) + shq_ref[...]).reshape(nb * tb, d)
    ukv = ((y * gkv_ref[...]) * (1.0 + sckv_ref[...]) + shkv_ref[...]).reshape(nb * tb, d)
    q = jnp.dot(uq.astype(BF16), wq_ref[...], preferred_element_type=F32)
    kv = jnp.dot(ukv.astype(BF16), wkv_ref[...], preferred_element_type=F32)
    cos, s1, s2 = (_tile_rows(r[...], nb) for r in (cos_ref, s1_ref, s2_ref))
    q_p = _group_norm_rope(q, gsum_ref, qn_ref[...], cos, s1, s2)
    k_p = _group_norm_rope(kv[:, :LANES], gsum_ref, kn_ref[...], cos, s1, s2)
    qk_scale = HEAD_DIM ** -0.5
    for j, piece in enumerate(q_p):
        q_ref[:, :, j * LANES:(j + 1) * LANES] = (piece * qk_scale).reshape(nb, tb, LANES).astype(BF16)
    k_ref[...] = k_p[0].reshape(nb, tb, LANES)
    v_ref[...] = kv[:, LANES:].reshape(nb, tb, LANES)


def _qkv_b_call(x, mod_q, mod_kv, wq, wkv, gsum, qn, kn, tables, rows):
    b, t, d = x.shape
    nq = wq.shape[1]
    assert wkv.shape[1] == 2 * LANES
    nb, tb = _row_tiling(b, t, rows)
    nt = t // tb
    xmap = lambda i: (i // nt, i % nt, 0)
    mmap = lambda i: (i // nt, 0, 0)
    tmap = lambda i: (i % nt, 0)
    mspec = pl.BlockSpec((nb, 1, d), mmap)
    return pl.pallas_call(
        functools.partial(_qkv_b_kernel, nb=nb, tb=tb),
        out_shape=(jax.ShapeDtypeStruct((b, t, nq), BF16),
                   jax.ShapeDtypeStruct((b, t, LANES), F32),
                   jax.ShapeDtypeStruct((b, t, LANES), F32)),
        grid=(b // nb * nt,),
        in_specs=[pl.BlockSpec((nb, tb, d), xmap),
                  mspec, mspec, _const_spec((1, d)),
                  mspec, mspec, _const_spec((1, d)),
                  _const_spec((d, nq)),
                  _const_spec((d, 2 * LANES)),
                  _const_spec((MXU_DIM, MXU_DIM)),
                  _const_spec((1, nq)),
                  _const_spec((1, LANES)),
                  pl.BlockSpec((tb, LANES), tmap),
                  pl.BlockSpec((tb, LANES), tmap),
                  pl.BlockSpec((tb, LANES), tmap)],
        out_specs=(pl.BlockSpec((nb, tb, nq), xmap),
                   pl.BlockSpec((nb, tb, LANES), xmap),
                   pl.BlockSpec((nb, tb, LANES), xmap)),
        compiler_params=_cparams("parallel"),
        name="mixer_b_qkv",
    )(x, mod_q[0], mod_q[1], mod_q[2], mod_kv[0], mod_kv[1], mod_kv[2], wq, wkv, gsum, qn, kn,
      *tables)


def _tile_kv_head(a, n, groups):
    lane = lax.broadcasted_iota(jnp.int32, a.shape, 1)
    swapped = pltpu.roll(a, HEAD_DIM, 1)
    keep = (lane < HEAD_DIM) if n == 0 else (lane >= HEAD_DIM)
    pair = jnp.where(keep, a, swapped).astype(BF16)
    return jnp.concatenate([pair] * (groups // 2), axis=1)


def _sink_group_attention(qs, kt, vt, mask, sink_col, groups):
    rows, width = qs.shape
    head_of_lane = lax.broadcasted_iota(jnp.int32, (rows, width), 1) // HEAD_DIM
    zero = jnp.zeros_like(qs)
    lhs = jnp.concatenate([jnp.where(head_of_lane == h, qs, zero) for h in range(groups)], axis=0)
    s = lax.dot_general(lhs, kt, (((1,), (1,)), ((), ())), preferred_element_type=F32)
    s = jnp.where(mask, s, NEG_INF)
    m = jnp.maximum(jnp.max(s, axis=1, keepdims=True), sink_col)
    e = jnp.exp(s - m)
    den = jnp.sum(e, axis=1, keepdims=True) + jnp.exp(sink_col - m)
    ob = jnp.dot((e / den).astype(BF16), vt, preferred_element_type=F32)
    out = jnp.zeros((rows, width), F32)
    for h in range(groups):
        out = out + jnp.where(head_of_lane == h, ob[h * rows:(h + 1) * rows], 0.0)
    return out


def _window_attn_kernel(q_ref, kprev_ref, kcur_ref, vprev_ref, vcur_ref, sink_ref, o_ref,
                        *, tq, groups, kv_heads):
    q = q_ref[0]
    kc = jnp.concatenate([kprev_ref[0], kcur_ref[0]], axis=0)
    vc = jnp.concatenate([vprev_ref[0], vcur_ref[0]], axis=0)
    win = kprev_ref.shape[1]
    keys = win + tq
    first_chunk = pl.program_id(1) * (tq // CHUNK)
    q_chunk = lax.broadcasted_iota(jnp.int32, (tq, keys), 0) // CHUNK
    k_chunk = lax.broadcasted_iota(jnp.int32, (tq, keys), 1) // CHUNK - win // CHUNK
    visible = ((k_chunk <= q_chunk) & (k_chunk >= q_chunk - WIN_CHUNKS)
               & (k_chunk + first_chunk >= 0))
    bias = jnp.where(visible, 0.0, NEG_INF)
    lane = lax.broadcasted_iota(jnp.int32, (tq, LANES), 1)
    in_half = (lane < HEAD_DIM, lane >= HEAD_DIM)
    zero = jnp.zeros((tq, LANES), BF16)
    pairs = groups // 2
    for n in range(kv_heads):
        kt = _tile_kv_head(kc, n, 2)
        vt = _tile_kv_head(vc, n, 2)
        for j in range(pairs):
            blk = n * pairs + j
            qp = q[:, blk * LANES:(blk + 1) * LANES]
            halves = []
            for half in range(2):
                sink = sink_ref[2 * blk + half]
                qm = jnp.where(in_half[half], qp, zero)
                s = lax.dot_general(qm, kt, (((1,), (1,)), ((), ())), preferred_element_type=F32)
                s = s + bias
                m = jnp.maximum(jnp.max(s, axis=1, keepdims=True), sink)
                e = jnp.exp(s - m)
                den = jnp.sum(e, axis=1, keepdims=True) + jnp.exp(sink - m)
                halves.append(jnp.dot(e.astype(BF16), vt, preferred_element_type=F32) / den)
            o_ref[0, :, blk * LANES:(blk + 1) * LANES] = jnp.where(
                in_half[0], halves[0], halves[1]).astype(BF16)


def _window_attn_call(q, k, v, sinks, kv_heads, tq):
    b, t, nq = q.shape
    groups = nq // (kv_heads * HEAD_DIM)
    win = WIN_CHUNKS * CHUNK
    assert t % tq == 0 and tq % win == 0 and groups % 2 == 0 and win == LANES
    ratio = tq // win
    cur_spec = pl.BlockSpec((1, tq, LANES), lambda bi, i: (bi, i, 0))
    prev_spec = pl.BlockSpec((1, win, LANES), lambda bi, i: (bi, jnp.maximum(i * ratio - 1, 0), 0))
    return pl.pallas_call(
        functools.partial(_window_attn_kernel, tq=tq, groups=groups, kv_heads=kv_heads),
        out_shape=jax.ShapeDtypeStruct((b, t, nq), BF16),
        grid=(b, t // tq),
        in_specs=[pl.BlockSpec((1, tq, nq), lambda bi, i: (bi, i, 0)),
                  prev_spec, cur_spec, prev_spec, cur_spec,
                  pl.BlockSpec(memory_space=pltpu.SMEM)],
        out_specs=pl.BlockSpec((1, tq, nq), lambda bi, i: (bi, i, 0)),
        compiler_params=_cparams("parallel", "parallel"),
        name="mixer_b_window_attention",
    )(q, k, k, v, v, sinks)


def _window_decode_kernel(q_ref, kp_ref, kn_ref, vp_ref, vn_ref, sink_ref, o_ref,
                          *, past_len, groups, kv_heads):
    q = q_ref[0]
    tn = q.shape[0]
    win = kp_ref.shape[1]
    keys = 2 * LANES
    kc = _pad_rows(jnp.concatenate([kp_ref[0], kn_ref[0]], axis=0), keys)
    vc = _pad_rows(jnp.concatenate([vp_ref[0], vn_ref[0]], axis=0), keys)
    width = groups * HEAD_DIM
    rows = groups * tn
    shape = (rows, keys)
    col = lax.broadcasted_iota(jnp.int32, shape, 1)
    q_chunk = (past_len + lax.broadcasted_iota(jnp.int32, shape, 0) % tn) // CHUNK
    k_pos = past_len - win + col
    k_chunk = k_pos // CHUNK
    mask = ((k_chunk <= q_chunk) & (k_chunk >= q_chunk - WIN_CHUNKS) & (k_pos >= 0)
            & (col < win + tn))
    outs = []
    for n in range(kv_heads):
        outs.append(_sink_group_attention(q[:, n * width:(n + 1) * width],
                                          _tile_kv_head(kc, n, groups), _tile_kv_head(vc, n, groups),
                                          mask, sink_ref[n], groups))
    o_ref[0] = jnp.concatenate(outs, axis=1).astype(BF16)


def _window_decode_call(q, k_new, v_new, past_k, past_v, sink_cols, past_len):
    b, tn, nq = q.shape
    win = past_k.shape[1]
    kv_heads = sink_cols.shape[0]
    groups = nq // (kv_heads * HEAD_DIM)
    assert win + tn <= 2 * LANES
    new_spec = pl.BlockSpec((1, tn, LANES), lambda bi: (bi, 0, 0))
    past_spec = pl.BlockSpec((1, win, LANES), lambda bi: (bi, 0, 0))
    return pl.pallas_call(
        functools.partial(_window_decode_kernel, past_len=past_len, groups=groups, kv_heads=kv_heads),
        out_shape=jax.ShapeDtypeStruct((b, tn, nq), BF16),
        grid=(b,),
        in_specs=[pl.BlockSpec((1, tn, nq), lambda bi: (bi, 0, 0)),
                  past_spec, new_spec, past_spec, new_spec,
                  pl.BlockSpec(sink_cols.shape, lambda bi: (0, 0, 0))],
        out_specs=pl.BlockSpec((1, tn, nq), lambda bi: (bi, 0, 0)),
        compiler_params=_cparams("parallel"),
        name="mixer_b_decode_attention",
    )(q, past_k, k_new, past_v, v_new, sink_cols)


def _route_top2(logits, experts):
    lane = lax.broadcasted_iota(jnp.int32, logits.shape, 1).astype(F32)
    big = float(LANES)
    lg = jnp.where(lane < experts, logits, NEG_INF)
    m1 = jnp.max(lg, axis=1, keepdims=True)
    i1 = jnp.min(jnp.where(lg == m1, lane, big), axis=1, keepdims=True)
    lg2 = jnp.where(lane == i1, NEG_INF, lg)
    m2 = jnp.max(lg2, axis=1, keepdims=True)
    i2 = jnp.min(jnp.where(lg2 == m2, lane, big), axis=1, keepdims=True)
    e2 = jnp.exp(m2 - m1)
    den = 1.0 + e2
    return jnp.where(lane == i1, 1.0 / den, 0.0) + jnp.where(lane == i2, e2 / den, 0.0)


def _moe_kernel(h_ref, sh_ref, sc_ref, gate_ref, g_ref, rhi_ref, rlo_ref, wg_ref, wu_ref, wd_ref,
                out_ref, u_s, rnk_s, gat_s, rnkt_s, cnt_s, xg_s, ye_s, *, nb, tb, experts):
    e = pl.program_id(1)
    k = pl.program_id(2)
    last_k = pl.num_programs(2) - 1
    d = h_ref.shape[-1]
    tm = nb * tb
    r2 = min(2 * MXU_DIM, tm)
    r1 = r2 // 2
    r0 = r2 // 4
    kc = min(2 * MXU_DIM, tm)
    rc = min(MXU_DIM, tm)
    assert nb == 1 or rc == tm

    @pl.when((e == 0) & (k == 0))
    def _route():
        row_i = lax.broadcasted_iota(jnp.int32, (rc, rc), 0)
        col_i = lax.broadcasted_iota(jnp.int32, (rc, rc), 1)
        below = jnp.where(col_i < row_i, 1.0, 0.0).astype(BF16)

        def chunk(r, seen):
            rows = pl.ds(pl.multiple_of(r * rc, rc), rc)
            hx = h_ref[:, rows, :] if nb == 1 else h_ref[...]
            u = _modulate(hx, g_ref[...], sh_ref[...], sc_ref[...]).reshape(rc, d)
            ub = u.astype(BF16)
            ulo = (u - ub.astype(F32)).astype(BF16)
            rhi = rhi_ref[...]
            logits = (jnp.dot(ub, rhi, preferred_element_type=F32)
                      + jnp.dot(ub, rlo_ref[...], preferred_element_type=F32)
                      + jnp.dot(ulo, rhi, preferred_element_type=F32))
            dg = _route_top2(logits, experts)
            sel = dg > 0.0
            sel_f = jnp.where(sel, 1.0, 0.0)
            rank = jnp.dot(below, sel_f.astype(BF16), preferred_element_type=F32) + seen
            u_s[rows, :] = ub
            gat_s[rows, :] = dg
            rnk_s[rows, :] = jnp.where(sel, rank, -1.0)
            return seen + jnp.sum(sel_f, axis=0, keepdims=True)

        counts = lax.fori_loop(0, tm // rc, chunk, jnp.zeros((1, LANES), F32))
        for x in range(experts):
            cnt_s[x] = counts[0, x].astype(jnp.int32)
        rnkt_s[...] = rnk_s[...].T[:rnkt_s.shape[0]]
        out_ref[...] = jnp.zeros(out_ref.shape, F32)

    count = cnt_s[e]
    n2 = (count + r2 - 1) // r2

    @pl.when(k == 0)
    def _gather():
        def tile(t, carry):
            base = pl.multiple_of(t * r2, r2)
            want = (base + lax.broadcasted_iota(jnp.int32, (r2, kc), 0)).astype(F32)
            acc = jnp.zeros((r2, d), F32)
            for c in range(tm // kc):
                pos = rnkt_s[pl.ds(e, 1), c * kc:(c + 1) * kc]
                pick = jnp.where(pos == want, 1.0, 0.0).astype(BF16)
                acc = acc + jnp.dot(pick, u_s[c * kc:(c + 1) * kc, :], preferred_element_type=F32)
            xg_s[pl.ds(base, r2), :] = acc.astype(BF16)
            ye_s[pl.ds(base, r2), :] = jnp.zeros((r2, d), F32)
            return carry

        lax.fori_loop(0, n2, tile, 0)

    def expert_rows(base, rows):
        x = xg_s[pl.ds(base, rows), :]
        g = jnp.dot(x, wg_ref[0, 0], preferred_element_type=F32)
        up = jnp.dot(x, wu_ref[0, 0], preferred_element_type=F32)
        a = ((g * _sigmoid(g)) * up).astype(BF16)
        ye_s[pl.ds(base, rows), :] += jnp.dot(a, wd_ref[0], preferred_element_type=F32)

    n_full = count // r2
    rest = count - n_full * r2
    n_full = n_full + (rest > r1).astype(jnp.int32)

    def full_tile(t, carry):
        expert_rows(pl.multiple_of(t * r2, r2), r2)
        return carry

    lax.fori_loop(0, n_full, full_tile, 0)

    @pl.when((rest > r0) & (rest <= r1))
    def _half_tile():
        expert_rows(pl.multiple_of(n_full * r2, r2), r1)

    @pl.when((rest > 0) & (rest <= r0))
    def _quarter_tile():
        expert_rows(pl.multiple_of(n_full * r2, r2), r0)

    @pl.when(k == last_k)
    def _scatter():
        lane = lax.broadcasted_iota(jnp.int32, (tm, LANES), 1)
        pos = jnp.sum(jnp.where(lane == e, rnk_s[...], 0.0), axis=1, keepdims=True)
        gate = jnp.sum(jnp.where(lane == e, gat_s[...], 0.0), axis=1, keepdims=True)

        def tile(t, carry):
            base = pl.multiple_of(t * r2, r2)
            want = (base + lax.broadcasted_iota(jnp.int32, (tm, r2), 1)).astype(F32)
            place = jnp.where(pos == want, gate, 0.0).astype(BF16)
            y = ye_s[pl.ds(base, r2), :].astype(BF16)
            out_ref[...] += jnp.dot(place, y, preferred_element_type=F32).reshape(nb, tb, d)
            return carry

        lax.fori_loop(0, n2, tile, 0)

    @pl.when((e == pl.num_programs(1) - 1) & (k == last_k))
    def _residual():
        out_ref[...] = h_ref[...] + gate_ref[...] * out_ref[...]


def _moe_call(h, shift, scale, gate, g, r_hi, r_lo, w_gu, w_d, experts, rows):
    b, t, d = h.shape
    fc = w_gu.shape[3]
    nk = w_gu.shape[1] // 2
    assert w_d.shape[1] == nk * fc
    nb, tb = _row_tiling(b, t, rows)
    nt = t // tb
    tm = nb * tb
    xmap = lambda i, e, k: (i // nt, i % nt, 0)
    mmap = lambda i, e, k: (i // nt, 0, 0)
    mspec = pl.BlockSpec((nb, 1, d), mmap)
    sub = 8
    assert experts <= sub and tm % min(2 * MXU_DIM, tm) == 0
    return pl.pallas_call(
        functools.partial(_moe_kernel, nb=nb, tb=tb, experts=experts),
        out_shape=jax.ShapeDtypeStruct(h.shape, F32),
        grid=(b // nb * nt, experts, nk),
        in_specs=[pl.BlockSpec((nb, tb, d), xmap, pipeline_mode=pl.Buffered(1)),
                  mspec, mspec, mspec,
                  _const_spec((1, d)),
                  _const_spec((d, LANES)),
                  _const_spec((d, LANES)),
                  pl.BlockSpec((1, 1, d, fc), lambda i, e, k: (e, k, 0, 0)),
                  pl.BlockSpec((1, 1, d, fc), lambda i, e, k: (e, k + nk, 0, 0)),
                  pl.BlockSpec((1, fc, d), lambda i, e, k: (e, k, 0))],
        out_specs=pl.BlockSpec((nb, tb, d), xmap, pipeline_mode=pl.Buffered(1)),
        scratch_shapes=[pltpu.VMEM((tm, d), BF16),
                        pltpu.VMEM((tm, LANES), F32),
                        pltpu.VMEM((tm, LANES), F32),
                        pltpu.VMEM((sub, tm), F32),
                        pltpu.SMEM((sub,), jnp.int32),
                        pltpu.VMEM((tm, d), BF16),
                        pltpu.VMEM((tm, d), F32)],
        compiler_params=_cparams("parallel", "arbitrary", "arbitrary"),
        name="expert_swiglu",
    )(h, shift, scale, gate, g, r_hi, r_lo, w_gu, w_gu, w_d)


def _rope_tables(pos):
    half = ROT_DIM // 2
    inv = ROPE_THETA ** (-(jnp.arange(half, dtype=F32) * 2.0 / ROT_DIM))
    ang = pos.astype(F32)[:, None] * inv[None, :]
    cos, sin = jnp.cos(ang), jnp.sin(ang)
    t = pos.shape[0]
    zeros = lambda n: jnp.zeros((t, n), F32)
    c = jnp.concatenate([cos, cos, jnp.ones((t, HEAD_DIM - ROT_DIM), F32)], axis=1)
    s1 = jnp.concatenate([-sin, zeros(HEAD_DIM - half)], axis=1)
    s2 = jnp.concatenate([zeros(half), sin, zeros(HEAD_DIM - ROT_DIM)], axis=1)
    rep = LANES // HEAD_DIM
    return tuple(jnp.tile(a, (1, rep)) for a in (c, s1, s2))


def _mod_rows(mod, lo, hi):
    d = mod.shape[1]
    return mod[lo:hi].reshape(hi - lo, 1, d)


def _trunk(x, mods, kv_mods, w, past, rows):
    b, t, d = x.shape
    past_len = 0 if past is None else past[0].shape[2]
    pos = past_len + jnp.arange(t, dtype=jnp.int32)
    tables = _rope_tables(pos)
    heads_a = w["a_heads"]
    big = t >= rows

    m0 = mods[0]
    lam_init = 0.8 - 0.6 * math.exp(-0.3 * 0)
    qh, kh, vh, a_k, a_v = _qkv_a_call(x, m0[0], m0[1], w["norm_g"][0][0], w["a_w_qkv"], w["gsum"],
                                       w["a_q_gain"], w["a_k_gain"], tables, heads_a, rows)
    if past is None:
        o = _diff_attn_call(qh, kh, vh, w["a_lambda"], w["a_subln"], lam_init, 256, 512, 8)
    else:
        pk = past[0][0].reshape(b, past_len, heads_a * LANES)
        pv = past[1][0].reshape(b, past_len, heads_a * LANES)
        o = _diff_decode_call(qh, kh, vh, pk, pv, w["a_lambda"], w["a_subln"], lam_init)
    h = _ffn_call(o, x, m0[2], w["a_w_o"], m0[3], m0[4], m0[5], w["norm_g"][0][1],
                  w["f_w_gu"], w["f_w_d"], rows)

    m1 = mods[1]
    q, k_sh, v_sh = _qkv_b_call(h, (m1[0], m1[1], w["norm_g"][1][0]),
                                (kv_mods[0], kv_mods[1], w["kv_norm_g"]),
                                w["b_w_q"], w["kv_w"], w["gsum"], w["b_q_gain"], w["kv_k_gain"],
                                tables, rows)
    if past is None:
        o = _window_attn_call(q, k_sh, v_sh, w["sinks"], w["kv_heads"], 256)
    else:
        win = past[2].shape[1]
        o = _window_decode_call(q, k_sh, v_sh, past[2].reshape(b, win, LANES),
                                past[3].reshape(b, win, LANES), w["sink_cols_decode"], past_len)
    h = _proj_res_call(o, h, m1[2], w["b_w_o"], rows)
    h = _moe_call(h, m1[3], m1[4], m1[5], w["norm_g"][1][1], w["r_hi"], w["r_lo"],
                  w["m_w_gu"], w["m_w_d"], w["experts"], min(2048, t) if big else rows)
    return h, a_k, a_v, k_sh, v_sh


def kernel(x_prompt, x_sample, c_prompt, c_sample, cache_a_k, cache_a_v, cache_b_k, cache_b_v,
           w_mod, b_mod, norm_g, a_w_qkv, a_q_norm, a_k_norm, a_lambda, a_subln, a_w_o,
           kv_norm_g, kv_w_mod, kv_b_mod, kv_w, kv_k_norm, b_w_q, b_q_norm, b_sinks, b_w_o,
           f_w_gu, f_w_d, m_router, m_w_gu, m_w_d):
    bp, t, d = x_prompt.shape
    bs, tn, _ = x_sample.shape
    assert w_mod.shape[0] == 2 and a_w_qkv.shape[0] == 1 and b_w_q.shape[0] == 1
    heads_a = cache_a_k.shape[3]
    kv_heads = cache_b_k.shape[2]
    heads_b = b_sinks.shape[1]
    groups = heads_b // kv_heads
    experts = m_router.shape[2]

    c_all = jnp.concatenate([c_prompt, c_sample], axis=0)
    mods = [_mod_call(c_all, w_mod[l], b_mod[l]) for l in range(2)]
    kv_mod = _mod_call(c_all, kv_w_mod, kv_b_mod)

    def split(mod, n, lo, hi):
        m = mod[lo:hi].reshape(hi - lo, n, 1, d)
        return [m[:, j] for j in range(n)]

    group = jnp.arange(MXU_DIM) // HEAD_DIM
    router = jnp.pad(m_router[0], ((0, 0), (0, LANES - experts)))
    r_hi = router.astype(BF16)

    def sink_cols(rows_per_head):
        s = b_sinks[0].astype(F32).reshape(kv_heads, groups, 1)
        return jnp.broadcast_to(s, (kv_heads, groups, rows_per_head)).reshape(
            kv_heads, groups * rows_per_head, 1)

    w = dict(
        a_heads=heads_a, experts=experts,
        norm_g=[[norm_g[l, j].reshape(1, d) for j in range(2)] for l in range(2)],
        gsum=(group[:, None] == group[None, :]).astype(BF16),
        a_w_qkv=a_w_qkv[0].astype(BF16),
        a_q_gain=jnp.tile(a_q_norm[0].reshape(1, LANES), (1, heads_a)),
        a_k_gain=jnp.tile(a_k_norm[0].reshape(1, LANES), (1, heads_a)),
        a_lambda=a_lambda[0].astype(F32),
        a_subln=a_subln[0].reshape(1, LANES),
        a_w_o=a_w_o[0].astype(BF16),
        f_w_gu=f_w_gu[0].astype(BF16), f_w_d=f_w_d[0].astype(BF16),
        kv_norm_g=kv_norm_g.reshape(1, d),
        kv_w=kv_w.astype(BF16),
        kv_k_gain=jnp.tile(kv_k_norm.reshape(1, HEAD_DIM), (1, LANES // HEAD_DIM)),
        b_w_q=b_w_q[0].astype(BF16),
        b_q_gain=jnp.tile(b_q_norm[0].reshape(1, HEAD_DIM), (1, heads_b)),
        b_w_o=b_w_o[0].astype(BF16),
        sinks=b_sinks[0].astype(F32), kv_heads=kv_heads, sink_cols_decode=sink_cols(tn),
        r_hi=r_hi, r_lo=(router - r_hi.astype(F32)).astype(BF16),
        m_w_gu=m_w_gu[0].reshape(experts, d, -1, MOE_FF_CHUNK).transpose(0, 2, 1, 3).astype(BF16),
        m_w_d=m_w_d[0].astype(BF16),
    )

    rows = 512
    y_p, ak_p, av_p, bk_p, bv_p = _trunk(
        x_prompt, [split(m, 6, 0, bp) for m in mods], split(kv_mod, 2, 0, bp), w, None, rows)
    y_s, ak_s, av_s, bk_s, bv_s = _trunk(
        x_sample, [split(m, 6, bp, bp + bs) for m in mods], split(kv_mod, 2, bp, bp + bs), w,
        (cache_a_k, cache_a_v, cache_b_k, cache_b_v), rows)

    win = min(WIN_CHUNKS * CHUNK, t)
    a_k_shape = lambda b_, t_: (1, b_, t_, heads_a, 2, HEAD_DIM)
    a_v_shape = lambda b_, t_: (1, b_, t_, heads_a, 2 * HEAD_DIM)
    b_shape = lambda b_, t_: (b_, t_, kv_heads, HEAD_DIM)
    return (y_p, y_s,
            ak_p.reshape(a_k_shape(bp, t)), av_p.reshape(a_v_shape(bp, t)),
            bk_p[:, -win:].reshape(b_shape(bp, win)), bv_p[:, -win:].reshape(b_shape(bp, win)),
            ak_s.reshape(a_k_shape(bs, tn)), av_s.reshape(a_v_shape(bs, tn)),
            bk_s.reshape(b_shape(bs, tn)), bv_s.reshape(b_shape(bs, tn)))
```

```python
import functools
import math

import jax
import jax.numpy as jnp
from jax import lax
from jax.experimental import pallas as pl
from jax.experimental.pallas import tpu as pltpu

CHUNK = 64
WIN_CHUNKS = 2
ROT_DIM = 16
ROPE_THETA = 500000.0
EPS = 1e-6
HEAD_DIM = 64
N_ROUTED = 2
MOE_FF_CHUNK = 512

LANES = 128
MXU_DIM = 256
VMEM_LIMIT_BYTES = 56 * 1024 * 1024

F32 = jnp.float32
BF16 = jnp.bfloat16
NEG_INF = float("-inf")


def _cparams(*sem):
    return pltpu.CompilerParams(dimension_semantics=sem, vmem_limit_bytes=VMEM_LIMIT_BYTES)


def _row_tiling(batch, seq, rows):
    if seq >= rows:
        assert seq % rows == 0
        return 1, rows
    nb = max(1, min(batch, rows // seq))
    while batch % nb:
        nb -= 1
    return nb, seq


def _const_spec(shape):
    zeros = (0,) * len(shape)
    return pl.BlockSpec(shape, lambda *_: zeros, pipeline_mode=pl.Buffered(1))


def _sigmoid(x):
    return 1.0 / (1.0 + jnp.exp(-x))


def _rms_scale(x):
    return x * lax.rsqrt(jnp.mean(x * x, axis=-1, keepdims=True) + EPS)


def _modulate(x, g, shift, scale):
    return (_rms_scale(x) * g) * (1.0 + scale) + shift


def _group_norm_rope(x, gsum_ref, gain, cos, s1, s2):
    width = x.shape[1]
    slab = MXU_DIM if width % MXU_DIM == 0 else LANES
    gsum = gsum_ref[:slab, :slab]
    pieces = []
    for j in range(width // slab):
        xs = x[:, j * slab:(j + 1) * slab]
        ss = jnp.dot((xs * xs).astype(BF16), gsum, preferred_element_type=F32)
        y = xs * lax.rsqrt(ss * (1.0 / HEAD_DIM) + EPS) * gain[:, j * slab:(j + 1) * slab]
        for t in range(slab // LANES):
            yt = y[:, t * LANES:(t + 1) * LANES]
            pieces.append(yt * cos + pltpu.roll(yt, LANES - ROT_DIM // 2, 1) * s1
                          + pltpu.roll(yt, ROT_DIM // 2, 1) * s2)
    return pieces


def _tile_rows(t, nb):
    return t if nb == 1 else jnp.concatenate([t] * nb, axis=0)


def _mod_kernel(c_ref, w_ref, b_ref, o_ref):
    c = c_ref[...]
    sc = (c * _sigmoid(c)).astype(BF16)
    o_ref[...] = jnp.dot(sc, w_ref[...].astype(BF16), preferred_element_type=F32) + b_ref[...]


def _mod_call(c, w, b):
    rows, d = c.shape
    n = w.shape[1]
    tn = 2048
    assert n % tn == 0
    return pl.pallas_call(
        _mod_kernel,
        out_shape=jax.ShapeDtypeStruct((rows, n), F32),
        grid=(n // tn,),
        in_specs=[pl.BlockSpec((rows, d), lambda j: (0, 0)),
                  pl.BlockSpec((d, tn), lambda j: (0, j)),
                  pl.BlockSpec((1, tn), lambda j: (0, j))],
        out_specs=pl.BlockSpec((rows, tn), lambda j: (0, j)),
        compiler_params=_cparams("parallel"),
        name="adaln_mod",
    )(c, w, b.reshape(1, n))


def _qkv_a_kernel(x_ref, sh_ref, sc_ref, g_ref, w_ref, gsum_ref, qg_ref, kg_ref,
                  cos_ref, s1_ref, s2_ref, qh_ref, kh_ref, vh_ref, ak_ref, av_ref,
                  *, nb, tb, heads):
    x = x_ref[...]
    d = x.shape[-1]
    u = _modulate(x, g_ref[...], sh_ref[...], sc_ref[...]).reshape(nb * tb, d).astype(BF16)
    qkv = jnp.dot(u, w_ref[...], preferred_element_type=F32)
    nqk = heads * 2 * HEAD_DIM
    cos, s1, s2 = (_tile_rows(r[...], nb) for r in (cos_ref, s1_ref, s2_ref))
    q_p = _group_norm_rope(qkv[:, :nqk], gsum_ref, qg_ref[...], cos, s1, s2)
    k_p = _group_norm_rope(qkv[:, nqk:2 * nqk], gsum_ref, kg_ref[...], cos, s1, s2)
    qk_scale = HEAD_DIM ** -0.5 * math.log2(math.e)
    for h in range(heads):
        cols = slice(h * LANES, (h + 1) * LANES)
        v_h = qkv[:, 2 * nqk + h * LANES:2 * nqk + (h + 1) * LANES].reshape(nb, tb, LANES)
        k_h = k_p[h].reshape(nb, tb, LANES)
        qh_ref[:, h] = (q_p[h] * qk_scale).reshape(nb, tb, LANES).astype(BF16)
        kh_ref[:, h] = k_h.astype(BF16)
        vh_ref[:, h] = v_h.astype(BF16)
        ak_ref[:, :, cols] = k_h
        av_ref[:, :, cols] = v_h


def _qkv_a_call(x, shift, scale, g, w, gsum, qg, kg, tables, heads, rows):
    b, t, d = x.shape
    nb, tb = _row_tiling(b, t, rows)
    nt = t // tb
    n = w.shape[1]
    xmap = lambda i: (i // nt, i % nt, 0)
    mmap = lambda i: (i // nt, 0, 0)
    tmap = lambda i: (i % nt, 0)
    hmap = lambda i: (i // nt, 0, i % nt, 0)
    head_shape = jax.ShapeDtypeStruct((b, heads, t, LANES), BF16)
    flat_shape = jax.ShapeDtypeStruct((b, t, heads * LANES), F32)
    return pl.pallas_call(
        functools.partial(_qkv_a_kernel, nb=nb, tb=tb, heads=heads),
        out_shape=(head_shape, head_shape, head_shape, flat_shape, flat_shape),
        grid=(b // nb * nt,),
        in_specs=[pl.BlockSpec((nb, tb, d), xmap),
                  pl.BlockSpec((nb, 1, d), mmap),
                  pl.BlockSpec((nb, 1, d), mmap),
                  _const_spec((1, d)),
                  _const_spec((d, n)),
                  _const_spec((MXU_DIM, MXU_DIM)),
                  _const_spec((1, heads * LANES)),
                  _const_spec((1, heads * LANES)),
                  pl.BlockSpec((tb, LANES), tmap),
                  pl.BlockSpec((tb, LANES), tmap),
                  pl.BlockSpec((tb, LANES), tmap)],
        out_specs=(pl.BlockSpec((nb, heads, tb, LANES), hmap),) * 3
                  + (pl.BlockSpec((nb, tb, heads * LANES), xmap),) * 2,
        compiler_params=_cparams("parallel"),
        name="mixer_a_qkv",
    )(x, shift, scale, g, w, gsum, qg, kg, *tables)


def _split_halves(q):
    lane = lax.broadcasted_iota(jnp.int32, q.shape, 1)
    zero = jnp.zeros_like(q)
    return jnp.concatenate([jnp.where(lane < HEAD_DIM, q, zero),
                            jnp.where(lane >= HEAD_DIM, q, zero)], axis=0)


def _diff_lambda(lam_ref, lam_init):
    lv = lam_ref[...]
    t1 = jnp.sum(lv[0:1] * lv[1:2], axis=1, keepdims=True)
    t2 = jnp.sum(lv[2:3] * lv[3:4], axis=1, keepdims=True)
    return jnp.exp(t1) - jnp.exp(t2) + lam_init


def _diff_finish(o0, o1, lam_ref, g_ref, lam_init):
    o = o0 - _diff_lambda(lam_ref, lam_init) * o1
    return _rms_scale(o) * g_ref[...] * (1.0 - lam_init)


def _diff_attn_kernel(q_ref, k_ref, v_ref, lam_ref, g_ref, o_ref, s_ref, m_ref, l_ref, acc_ref,
                      *, tq, tk, hp, lam_init):
    qi = pl.program_id(2)
    n_full = (qi * tq) // tk
    qqs = [_split_halves(q_ref[0, h]) for h in range(hp)]
    m_ref[...] = jnp.full(m_ref.shape, NEG_INF, F32)
    l_ref[...] = jnp.zeros(l_ref.shape, F32)
    acc_ref[...] = jnp.zeros(acc_ref.shape, F32)

    def scores(h, j):
        kj = k_ref[0, h, pl.ds(pl.multiple_of(j * tk, tk), tk), :]
        return lax.dot_general(qqs[h], kj, (((1,), (1,)), ((), ())), preferred_element_type=F32)

    def absorb(h, j, s):
        vj = v_ref[0, h, pl.ds(pl.multiple_of(j * tk, tk), tk), :]
        m_prev = m_ref[h]
        m_new = jnp.maximum(m_prev, jnp.max(s, axis=1, keepdims=True))
        alpha = jnp.exp2(m_prev - m_new)
        chunks = [jnp.exp2(s[:, c * LANES:(c + 1) * LANES] - m_new) for c in range(tk // LANES)]
        l_ref[h] = alpha * l_ref[h] + functools.reduce(lambda a, b: a + b, chunks)
        p = jnp.concatenate(chunks, axis=1).astype(BF16)
        acc_ref[h] = alpha * acc_ref[h] + jnp.dot(p, vj, preferred_element_type=F32)
        m_ref[h] = m_new

    for h in range(hp):
        s_ref[h] = scores(h, 0)

    def body(j, carry):
        for h in range(hp):
            s = s_ref[h]
            s_ref[h] = scores(h, j + 1)
            absorb(h, j, s)
        return carry

    lax.fori_loop(0, n_full, body, 0)
    q_chunk = (qi * tq + lax.broadcasted_iota(jnp.int32, (2 * tq, tk), 0) % tq) // CHUNK
    k_chunk = (n_full * tk + lax.broadcasted_iota(jnp.int32, (2 * tq, tk), 1)) // CHUNK
    visible = k_chunk <= q_chunk
    for h in range(hp):
        absorb(h, n_full, jnp.where(visible, s_ref[h], NEG_INF))

    for h in range(hp):
        o = acc_ref[h] / jnp.sum(l_ref[h], axis=1, keepdims=True)
        o_ref[0, :, h * LANES:(h + 1) * LANES] = _diff_finish(
            o[:tq], o[tq:], lam_ref, g_ref, lam_init).astype(BF16)


def _diff_attn_call(qh, kh, vh, lam_vec, subln, lam_init, tq, tk, hp):
    b, heads, t, _ = qh.shape
    assert t % tk == 0 and tk % tq == 0 and tq % CHUNK == 0 and heads % hp == 0
    return pl.pallas_call(
        functools.partial(_diff_attn_kernel, tq=tq, tk=tk, hp=hp, lam_init=lam_init),
        out_shape=jax.ShapeDtypeStruct((b, t, heads * LANES), BF16),
        grid=(b, heads // hp, t // tq),
        in_specs=[pl.BlockSpec((1, hp, tq, LANES), lambda bi, h, i: (bi, h, i, 0)),
                  pl.BlockSpec((1, hp, t, LANES), lambda bi, h, i: (bi, h, 0, 0), pipeline_mode=pl.Buffered(1)),
                  pl.BlockSpec((1, hp, t, LANES), lambda bi, h, i: (bi, h, 0, 0), pipeline_mode=pl.Buffered(1)),
                  pl.BlockSpec(lam_vec.shape, lambda bi, h, i: (0, 0)),
                  pl.BlockSpec((1, LANES), lambda bi, h, i: (0, 0))],
        out_specs=pl.BlockSpec((1, tq, hp * LANES), lambda bi, h, i: (bi, i, h)),
        scratch_shapes=[pltpu.VMEM((hp, 2 * tq, tk), F32)] + [pltpu.VMEM((hp, 2 * tq, LANES), F32)] * 3,
        compiler_params=_cparams("parallel", "parallel", "arbitrary"),
        name="mixer_a_attention",
    )(qh, kh, vh, lam_vec, subln)


def _pad_rows(a, rows):
    return jnp.concatenate([a, jnp.zeros((rows - a.shape[0], a.shape[1]), a.dtype)], axis=0)


def _diff_decode_kernel(q_ref, kp_ref, vp_ref, kn_ref, vn_ref, lam_ref, g_ref, o_ref,
                        *, past, lam_init):
    tn = q_ref.shape[2]
    qq = _split_halves(q_ref[0, 0])
    kp = kp_ref[0].astype(BF16)
    vp = vp_ref[0].astype(BF16)
    kn = _pad_rows(kn_ref[0, 0], LANES)
    vn = _pad_rows(vn_ref[0, 0], LANES)
    dims = (((1,), (1,)), ((), ()))
    s_p = lax.dot_general(qq, kp, dims, preferred_element_type=F32)
    s_n = lax.dot_general(qq, kn, dims, preferred_element_type=F32)

    def visible(shape, k_off, k_count):
        q_pos = past + lax.broadcasted_iota(jnp.int32, shape, 0) % tn
        col = lax.broadcasted_iota(jnp.int32, shape, 1)
        return ((k_off + col) // CHUNK <= q_pos // CHUNK) & (col < k_count)

    s_p = jnp.where(visible(s_p.shape, 0, past), s_p, NEG_INF)
    s_n = jnp.where(visible(s_n.shape, past, tn), s_n, NEG_INF)
    m = jnp.maximum(jnp.max(s_p, axis=1, keepdims=True), jnp.max(s_n, axis=1, keepdims=True))
    e_p = jnp.exp2(s_p - m)
    e_n = jnp.exp2(s_n - m)
    l = jnp.sum(e_p, axis=1, keepdims=True) + jnp.sum(e_n, axis=1, keepdims=True)
    acc = (jnp.dot(e_p.astype(BF16), vp, preferred_element_type=F32)
           + jnp.dot(e_n.astype(BF16), vn, preferred_element_type=F32))
    o = acc / l
    o_ref[0] = _diff_finish(o[:tn], o[tn:], lam_ref, g_ref, lam_init).astype(BF16)


def _diff_decode_call(qh, kh, vh, past_k, past_v, lam_vec, subln, lam_init):
    b, heads, tn, _ = qh.shape
    past = past_k.shape[1]
    new_spec = pl.BlockSpec((1, 1, tn, LANES), lambda bi, h: (bi, h, 0, 0))
    past_spec = pl.BlockSpec((1, past, LANES), lambda bi, h: (bi, 0, h))
    return pl.pallas_call(
        functools.partial(_diff_decode_kernel, past=past, lam_init=lam_init),
        out_shape=jax.ShapeDtypeStruct((b, tn, heads * LANES), BF16),
        grid=(b, heads),
        in_specs=[new_spec, past_spec, past_spec, new_spec, new_spec,
                  pl.BlockSpec(lam_vec.shape, lambda bi, h: (0, 0)),
                  pl.BlockSpec((1, LANES), lambda bi, h: (0, 0))],
        out_specs=pl.BlockSpec((1, tn, LANES), lambda bi, h: (bi, 0, h)),
        compiler_params=_cparams("parallel", "parallel"),
        name="mixer_a_decode_attention",
    )(qh, past_k, past_v, kh, vh, lam_vec, subln)


def _proj_res_kernel(o_ref, h_ref, gate_ref, w_ref, out_ref, *, nb, tb):
    k = o_ref.shape[-1]
    y = jnp.dot(o_ref[...].reshape(nb * tb, k), w_ref[...], preferred_element_type=F32)
    out_ref[...] = h_ref[...] + gate_ref[...] * y.reshape(nb, tb, y.shape[-1])


def _proj_res_call(o, h, gate, w, rows):
    b, t, d = h.shape
    k = o.shape[-1]
    nb, tb = _row_tiling(b, t, rows)
    nt = t // tb
    xmap = lambda i: (i // nt, i % nt, 0)
    return pl.pallas_call(
        functools.partial(_proj_res_kernel, nb=nb, tb=tb),
        out_shape=jax.ShapeDtypeStruct(h.shape, F32),
        grid=(b // nb * nt,),
        in_specs=[pl.BlockSpec((nb, tb, k), xmap),
                  pl.BlockSpec((nb, tb, d), xmap),
                  pl.BlockSpec((nb, 1, d), lambda i: (i // nt, 0, 0)),
                  _const_spec((k, d))],
        out_specs=pl.BlockSpec((nb, tb, d), xmap),
        compiler_params=_cparams("parallel"),
        name="out_proj_residual",
    )(o, h, gate, w)


def _ffn_kernel(o_ref, x_ref, ogate_ref, wo_ref, sh_ref, sc_ref, gate_ref, g_ref, wgu_ref, wd_ref,
                out_ref, *, nb, tb, ff, fc):
    d = x_ref.shape[-1]
    y = jnp.dot(o_ref[...].reshape(nb * tb, o_ref.shape[-1]), wo_ref[...],
                preferred_element_type=F32)
    h = x_ref[...] + ogate_ref[...] * y.reshape(nb, tb, d)
    u = _modulate(h, g_ref[...], sh_ref[...], sc_ref[...]).reshape(nb * tb, d).astype(BF16)
    acc = jnp.zeros((nb * tb, d), F32)
    for c in range(ff // fc):
        g = jnp.dot(u, wgu_ref[:, c * fc:(c + 1) * fc], preferred_element_type=F32)
        up = jnp.dot(u, wgu_ref[:, ff + c * fc:ff + (c + 1) * fc], preferred_element_type=F32)
        a = ((g * _sigmoid(g)) * up).astype(BF16)
        acc = acc + jnp.dot(a, wd_ref[c * fc:(c + 1) * fc, :], preferred_element_type=F32)
    out_ref[...] = h + gate_ref[...] * acc.reshape(nb, tb, d)


def _ffn_call(o, x, o_gate, w_o, shift, scale, gate, g, w_gu, w_d, rows):
    b, t, d = x.shape
    k = o.shape[-1]
    ff = w_d.shape[0]
    fc = ff // 2 if (ff // 2) % LANES == 0 else ff
    nb, tb = _row_tiling(b, t, rows)
    nt = t // tb
    xmap = lambda i: (i // nt, i % nt, 0)
    mspec = pl.BlockSpec((nb, 1, d), lambda i: (i // nt, 0, 0))
    return pl.pallas_call(
        functools.partial(_ffn_kernel, nb=nb, tb=tb, ff=ff, fc=fc),
        out_shape=jax.ShapeDtypeStruct(x.shape, F32),
        grid=(b // nb * nt,),
        in_specs=[pl.BlockSpec((nb, tb, k), xmap),
                  pl.BlockSpec((nb, tb, d), xmap),
                  mspec,
                  _const_spec((k, d)),
                  mspec, mspec, mspec,
                  _const_spec((1, d)),
                  _const_spec((d, 2 * ff)),
                  _const_spec((ff, d))],
        out_specs=pl.BlockSpec((nb, tb, d), xmap),
        compiler_params=_cparams("parallel"),
        name="out_proj_dense_swiglu",
    )(o, x, o_gate, w_o, shift, scale, gate, g, w_gu, w_d)


def _qkv_b_kernel(x_ref, shq_ref, scq_ref, gq_ref, shkv_ref, sckv_ref, gkv_ref, wq_ref, wkv_ref,
                  gsum_ref, qn_ref, kn_ref, cos_ref, s1_ref, s2_ref, q_ref, k_ref, v_ref,
                  *, nb, tb):
    x = x_ref[...]
    d = x.shape[-1]
    y = _rms_scale(x)
    uq = ((y * gq_ref[...]) * (1.0 + scq_ref[...]) + shq_ref[...]).reshape(nb * tb, d)
    ukv = ((y * gkv_ref[...]) * (1.0 + sckv_ref[...]) + shkv_ref[...]).reshape(nb * tb, d)
    q = jnp.dot(uq.astype(BF16), wq_ref[...], preferred_element_type=F32)
    kv = jnp.dot(ukv.astype(BF16), wkv_ref[...], preferred_element_type=F32)
    cos, s1, s2 = (_tile_rows(r[...], nb) for r in (cos_ref, s1_ref, s2_ref))
    q_p = _group_norm_rope(q, gsum_ref, qn_ref[...], cos, s1, s2)
    k_p = _group_norm_rope(kv[:, :LANES], gsum_ref, kn_ref[...], cos, s1, s2)
    qk_scale = HEAD_DIM ** -0.5
    for j, piece in enumerate(q_p):
        q_ref[:, :, j * LANES:(j + 1) * LANES] = (piece * qk_scale).reshape(nb, tb, LANES).astype(BF16)
    k_ref[...] = k_p[0].reshape(nb, tb, LANES)
    v_ref[...] = kv[:, LANES:].reshape(nb, tb, LANES)


def _qkv_b_call(x, mod_q, mod_kv, wq, wkv, gsum, qn, kn, tables, rows):
    b, t, d = x.shape
    nq = wq.shape[1]
    assert wkv.shape[1] == 2 * LANES
    nb, tb = _row_tiling(b, t, rows)
    nt = t // tb
    xmap = lambda i: (i // nt, i % nt, 0)
    mmap = lambda i: (i // nt, 0, 0)
    tmap = lambda i: (i % nt, 0)
    mspec = pl.BlockSpec((nb, 1, d), mmap)
    return pl.pallas_call(
        functools.partial(_qkv_b_kernel, nb=nb, tb=tb),
        out_shape=(jax.ShapeDtypeStruct((b, t, nq), BF16),
                   jax.ShapeDtypeStruct((b, t, LANES), F32),
                   jax.ShapeDtypeStruct((b, t, LANES), F32)),
        grid=(b // nb * nt,),
        in_specs=[pl.BlockSpec((nb, tb, d), xmap),
                  mspec, mspec, _const_spec((1, d)),
                  mspec, mspec, _const_spec((1, d)),
                  _const_spec((d, nq)),
                  _const_spec((d, 2 * LANES)),
                  _const_spec((MXU_DIM, MXU_DIM)),
                  _const_spec((1, nq)),
                  _const_spec((1, LANES)),
                  pl.BlockSpec((tb, LANES), tmap),
                  pl.BlockSpec((tb, LANES), tmap),
                  pl.BlockSpec((tb, LANES), tmap)],
        out_specs=(pl.BlockSpec((nb, tb, nq), xmap),
                   pl.BlockSpec((nb, tb, LANES), xmap),
                   pl.BlockSpec((nb, tb, LANES), xmap)),
        compiler_params=_cparams("parallel"),
        name="mixer_b_qkv",
    )(x, mod_q[0], mod_q[1], mod_q[2], mod_kv[0], mod_kv[1], mod_kv[2], wq, wkv, gsum, qn, kn,
      *tables)


def _tile_kv_head(a, n, groups):
    lane = lax.broadcasted_iota(jnp.int32, a.shape, 1)
    swapped = pltpu.roll(a, HEAD_DIM, 1)
    keep = (lane < HEAD_DIM) if n == 0 else (lane >= HEAD_DIM)
    pair = jnp.where(keep, a, swapped).astype(BF16)
    return jnp.concatenate([pair] * (groups // 2), axis=1)


def _sink_group_attention(qs, kt, vt, mask, sink_col, groups):
    rows, width = qs.shape
    head_of_lane = lax.broadcasted_iota(jnp.int32, (rows, width), 1) // HEAD_DIM
    zero = jnp.zeros_like(qs)
    lhs = jnp.concatenate([jnp.where(head_of_lane == h, qs, zero) for h in range(groups)], axis=0)
    s = lax.dot_general(lhs, kt, (((1,), (1,)), ((), ())), preferred_element_type=F32)
    s = jnp.where(mask, s, NEG_INF)
    m = jnp.maximum(jnp.max(s, axis=1, keepdims=True), sink_col)
    e = jnp.exp(s - m)
    den = jnp.sum(e, axis=1, keepdims=True) + jnp.exp(sink_col - m)
    ob = jnp.dot((e / den).astype(BF16), vt, preferred_element_type=F32)
    out = jnp.zeros((rows, width), F32)
    for h in range(groups):
        out = out + jnp.where(head_of_lane == h, ob[h * rows:(h + 1) * rows], 0.0)
    return out


def _window_attn_kernel(q_ref, kprev_ref, kcur_ref, vprev_ref, vcur_ref, sink_ref, o_ref,
                        *, tq, groups, kv_heads):
    q = q_ref[0]
    kc = jnp.concatenate([kprev_ref[0], kcur_ref[0]], axis=0)
    vc = jnp.concatenate([vprev_ref[0], vcur_ref[0]], axis=0)
    win = kprev_ref.shape[1]
    keys = win + tq
    first_chunk = pl.program_id(1) * (tq // CHUNK)
    q_chunk = lax.broadcasted_iota(jnp.int32, (tq, keys), 0) // CHUNK
    k_chunk = lax.broadcasted_iota(jnp.int32, (tq, keys), 1) // CHUNK - win // CHUNK
    visible = ((k_chunk <= q_chunk) & (k_chunk >= q_chunk - WIN_CHUNKS)
               & (k_chunk + first_chunk >= 0))
    bias = jnp.where(visible, 0.0, NEG_INF)
    lane = lax.broadcasted_iota(jnp.int32, (tq, LANES), 1)
    in_half = (lane < HEAD_DIM, lane >= HEAD_DIM)
    zero = jnp.zeros((tq, LANES), BF16)
    pairs = groups // 2
    for n in range(kv_heads):
        kt = _tile_kv_head(kc, n, 2)
        vt = _tile_kv_head(vc, n, 2)
        for j in range(pairs):
            blk = n * pairs + j
            qp = q[:, blk * LANES:(blk + 1) * LANES]
            halves = []
            for half in range(2):
                sink = sink_ref[2 * blk + half]
                qm = jnp.where(in_half[half], qp, zero)
                s = lax.dot_general(qm, kt, (((1,), (1,)), ((), ())), preferred_element_type=F32)
                s = s + bias
                m = jnp.maximum(jnp.max(s, axis=1, keepdims=True), sink)
                e = jnp.exp(s - m)
                den = jnp.sum(e, axis=1, keepdims=True) + jnp.exp(sink - m)
                halves.append(jnp.dot(e.astype(BF16), vt, preferred_element_type=F32) / den)
            o_ref[0, :, blk * LANES:(blk + 1) * LANES] = jnp.where(
                in_half[0], halves[0], halves[1]).astype(BF16)


def _window_attn_call(q, k, v, sinks, kv_heads, tq):
    b, t, nq = q.shape
    groups = nq // (kv_heads * HEAD_DIM)
    win = WIN_CHUNKS * CHUNK
    assert t % tq == 0 and tq % win == 0 and groups % 2 == 0 and win == LANES
    ratio = tq // win
    cur_spec = pl.BlockSpec((1, tq, LANES), lambda bi, i: (bi, i, 0))
    prev_spec = pl.BlockSpec((1, win, LANES), lambda bi, i: (bi, jnp.maximum(i * ratio - 1, 0), 0))
    return pl.pallas_call(
        functools.partial(_window_attn_kernel, tq=tq, groups=groups, kv_heads=kv_heads),
        out_shape=jax.ShapeDtypeStruct((b, t, nq), BF16),
        grid=(b, t // tq),
        in_specs=[pl.BlockSpec((1, tq, nq), lambda bi, i: (bi, i, 0)),
                  prev_spec, cur_spec, prev_spec, cur_spec,
                  pl.BlockSpec(memory_space=pltpu.SMEM)],
        out_specs=pl.BlockSpec((1, tq, nq), lambda bi, i: (bi, i, 0)),
        compiler_params=_cparams("parallel", "parallel"),
        name="mixer_b_window_attention",
    )(q, k, k, v, v, sinks)


def _window_decode_kernel(q_ref, kp_ref, kn_ref, vp_ref, vn_ref, sink_ref, o_ref,
                          *, past_len, groups, kv_heads):
    q = q_ref[0]
    tn = q.shape[0]
    win = kp_ref.shape[1]
    keys = 2 * LANES
    kc = _pad_rows(jnp.concatenate([kp_ref[0], kn_ref[0]], axis=0), keys)
    vc = _pad_rows(jnp.concatenate([vp_ref[0], vn_ref[0]], axis=0), keys)
    width = groups * HEAD_DIM
    rows = groups * tn
    shape = (rows, keys)
    col = lax.broadcasted_iota(jnp.int32, shape, 1)
    q_chunk = (past_len + lax.broadcasted_iota(jnp.int32, shape, 0) % tn) // CHUNK
    k_pos = past_len - win + col
    k_chunk = k_pos // CHUNK
    mask = ((k_chunk <= q_chunk) & (k_chunk >= q_chunk - WIN_CHUNKS) & (k_pos >= 0)
            & (col < win + tn))
    outs = []
    for n in range(kv_heads):
        outs.append(_sink_group_attention(q[:, n * width:(n + 1) * width],
                                          _tile_kv_head(kc, n, groups), _tile_kv_head(vc, n, groups),
                                          mask, sink_ref[n], groups))
    o_ref[0] = jnp.concatenate(outs, axis=1).astype(BF16)


def _window_decode_call(q, k_new, v_new, past_k, past_v, sink_cols, past_len):
    b, tn, nq = q.shape
    win = past_k.shape[1]
    kv_heads = sink_cols.shape[0]
    groups = nq // (kv_heads * HEAD_DIM)
    assert win + tn <= 2 * LANES
    new_spec = pl.BlockSpec((1, tn, LANES), lambda bi: (bi, 0, 0))
    past_spec = pl.BlockSpec((1, win, LANES), lambda bi: (bi, 0, 0))
    return pl.pallas_call(
        functools.partial(_window_decode_kernel, past_len=past_len, groups=groups, kv_heads=kv_heads),
        out_shape=jax.ShapeDtypeStruct((b, tn, nq), BF16),
        grid=(b,),
        in_specs=[pl.BlockSpec((1, tn, nq), lambda bi: (bi, 0, 0)),
                  past_spec, new_spec, past_spec, new_spec,
                  pl.BlockSpec(sink_cols.shape, lambda bi: (0, 0, 0))],
        out_specs=pl.BlockSpec((1, tn, nq), lambda bi: (bi, 0, 0)),
        compiler_params=_cparams("parallel"),
        name="mixer_b_decode_attention",
    )(q, past_k, k_new, past_v, v_new, sink_cols)


def _route_top2(logits, experts):
    lane = lax.broadcasted_iota(jnp.int32, logits.shape, 1).astype(F32)
    big = float(LANES)
    lg = jnp.where(lane < experts, logits, NEG_INF)
    m1 = jnp.max(lg, axis=1, keepdims=True)
    i1 = jnp.min(jnp.where(lg == m1, lane, big), axis=1, keepdims=True)
    lg2 = jnp.where(lane == i1, NEG_INF, lg)
    m2 = jnp.max(lg2, axis=1, keepdims=True)
    i2 = jnp.min(jnp.where(lg2 == m2, lane, big), axis=1, keepdims=True)
    e2 = jnp.exp(m2 - m1)
    den = 1.0 + e2
    return jnp.where(lane == i1, 1.0 / den, 0.0) + jnp.where(lane == i2, e2 / den, 0.0)


def _moe_kernel(h_ref, sh_ref, sc_ref, gate_ref, g_ref, rhi_ref, rlo_ref, wg_ref, wu_ref, wd_ref,
                out_ref, u_s, rnk_s, gat_s, rnkt_s, cnt_s, xg_s, ye_s, *, nb, tb, experts):
    e = pl.program_id(1)
    k = pl.program_id(2)
    last_k = pl.num_programs(2) - 1
    d = h_ref.shape[-1]
    tm = nb * tb
    r2 = min(2 * MXU_DIM, tm)
    sizes = [r2] + [r for r in (r2 // 2, r2 // 4) if r >= LANES]
    kc = min(2 * MXU_DIM, tm)
    rc = min(MXU_DIM, tm)
    assert nb == 1 or rc == tm

    @pl.when((e == 0) & (k == 0))
    def _route():
        row_i = lax.broadcasted_iota(jnp.int32, (rc, rc), 0)
        col_i = lax.broadcasted_iota(jnp.int32, (rc, rc), 1)
        below = jnp.where(col_i < row_i, 1.0, 0.0).astype(BF16)

        def chunk(r, seen):
            rows = pl.ds(pl.multiple_of(r * rc, rc), rc)
            hx = h_ref[:, rows, :] if nb == 1 else h_ref[...]
            u = _modulate(hx, g_ref[...], sh_ref[...], sc_ref[...]).reshape(rc, d)
            ub = u.astype(BF16)
            ulo = (u - ub.astype(F32)).astype(BF16)
            rhi = rhi_ref[...]
            logits = (jnp.dot(ub, rhi, preferred_element_type=F32)
                      + jnp.dot(ub, rlo_ref[...], preferred_element_type=F32)
                      + jnp.dot(ulo, rhi, preferred_element_type=F32))
            dg = _route_top2(logits, experts)
            sel = dg > 0.0
            sel_f = jnp.where(sel, 1.0, 0.0)
            rank = jnp.dot(below, sel_f.astype(BF16), preferred_element_type=F32) + seen
            u_s[rows, :] = ub
            gat_s[rows, :] = dg
            rnk_s[rows, :] = jnp.where(sel, rank, -1.0)
            return seen + jnp.sum(sel_f, axis=0, keepdims=True)

        counts = lax.fori_loop(0, tm // rc, chunk, jnp.zeros((1, LANES), F32))
        for x in range(experts):
            cnt_s[x] = counts[0, x].astype(jnp.int32)
        rnkt_s[...] = rnk_s[...].T[:rnkt_s.shape[0]]
        out_ref[...] = jnp.zeros(out_ref.shape, F32)

    count = cnt_s[e]
    whole = count // r2
    rest = count - whole * r2
    n_full = whole + (rest > sizes[1]).astype(jnp.int32) if len(sizes) > 1 else (count + r2 - 1) // r2

    def for_each_tile(fn):
        def full_tile(t, carry):
            fn(pl.multiple_of(t * r2, r2), r2)
            return carry

        lax.fori_loop(0, n_full, full_tile, 0)
        for i in range(1, len(sizes)):
            lo = sizes[i + 1] if i + 1 < len(sizes) else 0

            @pl.when((rest > lo) & (rest <= sizes[i]))
            def _small_tile(rows=sizes[i]):
                fn(pl.multiple_of(whole * r2, r2), rows)

    def gather_rows(base, rows):
        want = (base + lax.broadcasted_iota(jnp.int32, (rows, kc), 0)).astype(F32)
        acc = jnp.zeros((rows, d), F32)
        for c in range(tm // kc):
            pos = rnkt_s[pl.ds(e, 1), c * kc:(c + 1) * kc]
            pick = jnp.where(pos == want, 1.0, 0.0).astype(BF16)
            acc = acc + jnp.dot(pick, u_s[c * kc:(c + 1) * kc, :], preferred_element_type=F32)
        xg_s[pl.ds(base, rows), :] = acc.astype(BF16)
        ye_s[pl.ds(base, rows), :] = jnp.zeros((rows, d), F32)

    def expert_rows(base, rows):
        x = xg_s[pl.ds(base, rows), :]
        g = jnp.dot(x, wg_ref[0], preferred_element_type=F32)
        up = jnp.dot(x, wu_ref[0], preferred_element_type=F32)
        a = ((g * _sigmoid(g)) * up).astype(BF16)
        ye_s[pl.ds(base, rows), :] += jnp.dot(a, wd_ref[0], preferred_element_type=F32)

    @pl.when(k == 0)
    def _gather():
        for_each_tile(gather_rows)

    for_each_tile(expert_rows)

    @pl.when(k == last_k)
    def _scatter():
        lane = lax.broadcasted_iota(jnp.int32, (tm, LANES), 1)
        pos = jnp.sum(jnp.where(lane == e, rnk_s[...], 0.0), axis=1, keepdims=True)
        gate = jnp.sum(jnp.where(lane == e, gat_s[...], 0.0), axis=1, keepdims=True)

        def scatter_rows(base, rows):
            want = (base + lax.broadcasted_iota(jnp.int32, (tm, rows), 1)).astype(F32)
            place = jnp.where(pos == want, gate, 0.0).astype(BF16)
            y = ye_s[pl.ds(base, rows), :].astype(BF16)
            out_ref[...] += jnp.dot(place, y, preferred_element_type=F32).reshape(nb, tb, d)

        for_each_tile(scatter_rows)

    @pl.when((e == pl.num_programs(1) - 1) & (k == last_k))
    def _residual():
        out_ref[...] = h_ref[...] + gate_ref[...] * out_ref[...]


def _moe_call(h, shift, scale, gate, g, r_hi, r_lo, w_gu, w_d, experts, rows):
    b, t, d = h.shape
    ff = w_d.shape[1]
    fc = MOE_FF_CHUNK
    assert ff % fc == 0
    nk = ff // fc
    nb, tb = _row_tiling(b, t, rows)
    nt = t // tb
    tm = nb * tb
    xmap = lambda i, e, k: (i // nt, i % nt, 0)
    mmap = lambda i, e, k: (i // nt, 0, 0)
    mspec = pl.BlockSpec((nb, 1, d), mmap)
    sub = 8
    assert experts <= sub and tm % min(2 * MXU_DIM, tm) == 0
    return pl.pallas_call(
        functools.partial(_moe_kernel, nb=nb, tb=tb, experts=experts),
        out_shape=jax.ShapeDtypeStruct(h.shape, F32),
        grid=(b // nb * nt, experts, nk),
        in_specs=[pl.BlockSpec((nb, tb, d), xmap, pipeline_mode=pl.Buffered(1)),
                  mspec, mspec, mspec,
                  _const_spec((1, d)),
                  _const_spec((d, LANES)),
                  _const_spec((d, LANES)),
                  pl.BlockSpec((1, d, fc), lambda i, e, k: (e, 0, k)),
                  pl.BlockSpec((1, d, fc), lambda i, e, k: (e, 0, k + nk)),
                  pl.BlockSpec((1, fc, d), lambda i, e, k: (e, k, 0))],
        out_specs=pl.BlockSpec((nb, tb, d), xmap, pipeline_mode=pl.Buffered(1)),
        scratch_shapes=[pltpu.VMEM((tm, d), BF16),
                        pltpu.VMEM((tm, LANES), F32),
                        pltpu.VMEM((tm, LANES), F32),
                        pltpu.VMEM((sub, tm), F32),
                        pltpu.SMEM((sub,), jnp.int32),
                        pltpu.VMEM((tm, d), BF16),
                        pltpu.VMEM((tm, d), F32)],
        compiler_params=_cparams("parallel", "arbitrary", "arbitrary"),
        name="expert_swiglu",
    )(h, shift, scale, gate, g, r_hi, r_lo, w_gu, w_gu, w_d)


def _rope_tables(pos):
    half = ROT_DIM // 2
    inv = ROPE_THETA ** (-(jnp.arange(half, dtype=F32) * 2.0 / ROT_DIM))
    ang = pos.astype(F32)[:, None] * inv[None, :]
    cos, sin = jnp.cos(ang), jnp.sin(ang)
    t = pos.shape[0]
    zeros = lambda n: jnp.zeros((t, n), F32)
    c = jnp.concatenate([cos, cos, jnp.ones((t, HEAD_DIM - ROT_DIM), F32)], axis=1)
    s1 = jnp.concatenate([-sin, zeros(HEAD_DIM - half)], axis=1)
    s2 = jnp.concatenate([zeros(half), sin, zeros(HEAD_DIM - ROT_DIM)], axis=1)
    rep = LANES // HEAD_DIM
    return tuple(jnp.tile(a, (1, rep)) for a in (c, s1, s2))


def _mod_rows(mod, lo, hi):
    d = mod.shape[1]
    return mod[lo:hi].reshape(hi - lo, 1, d)


def _trunk(x, mods, kv_mods, w, past, rows):
    b, t, d = x.shape
    past_len = 0 if past is None else past[0].shape[2]
    pos = past_len + jnp.arange(t, dtype=jnp.int32)
    tables = _rope_tables(pos)
    heads_a = w["a_heads"]
    big = t >= rows

    m0 = mods[0]
    lam_init = 0.8 - 0.6 * math.exp(-0.3 * 0)
    qh, kh, vh, a_k, a_v = _qkv_a_call(x, m0[0], m0[1], w["norm_g"][0][0], w["a_w_qkv"], w["gsum"],
                                       w["a_q_gain"], w["a_k_gain"], tables, heads_a, rows)
    if past is None:
        o = _diff_attn_call(qh, kh, vh, w["a_lambda"], w["a_subln"], lam_init, 256, 512, 8)
    else:
        pk = past[0][0].reshape(b, past_len, heads_a * LANES)
        pv = past[1][0].reshape(b, past_len, heads_a * LANES)
        o = _diff_decode_call(qh, kh, vh, pk, pv, w["a_lambda"], w["a_subln"], lam_init)
    h = _ffn_call(o, x, m0[2], w["a_w_o"], m0[3], m0[4], m0[5], w["norm_g"][0][1],
                  w["f_w_gu"], w["f_w_d"], rows)

    m1 = mods[1]
    q, k_sh, v_sh = _qkv_b_call(h, (m1[0], m1[1], w["norm_g"][1][0]),
                                (kv_mods[0], kv_mods[1], w["kv_norm_g"]),
                                w["b_w_q"], w["kv_w"], w["gsum"], w["b_q_gain"], w["kv_k_gain"],
                                tables, rows)
    if past is None:
        o = _window_attn_call(q, k_sh, v_sh, w["sinks"], w["kv_heads"], 256)
    else:
        win = past[2].shape[1]
        o = _window_decode_call(q, k_sh, v_sh, past[2].reshape(b, win, LANES),
                                past[3].reshape(b, win, LANES), w["sink_cols_decode"], past_len)
    h = _proj_res_call(o, h, m1[2], w["b_w_o"], rows)
    h = _moe_call(h, m1[3], m1[4], m1[5], w["norm_g"][1][1], w["r_hi"], w["r_lo"],
                  w["m_w_gu"], w["m_w_d"], w["experts"], min(2048, t) if big else rows)
    return h, a_k, a_v, k_sh, v_sh


def kernel(x_prompt, x_sample, c_prompt, c_sample, cache_a_k, cache_a_v, cache_b_k, cache_b_v,
           w_mod, b_mod, norm_g, a_w_qkv, a_q_norm, a_k_norm, a_lambda, a_subln, a_w_o,
           kv_norm_g, kv_w_mod, kv_b_mod, kv_w, kv_k_norm, b_w_q, b_q_norm, b_sinks, b_w_o,
           f_w_gu, f_w_d, m_router, m_w_gu, m_w_d):
    bp, t, d = x_prompt.shape
    bs, tn, _ = x_sample.shape
    assert w_mod.shape[0] == 2 and a_w_qkv.shape[0] == 1 and b_w_q.shape[0] == 1
    heads_a = cache_a_k.shape[3]
    kv_heads = cache_b_k.shape[2]
    heads_b = b_sinks.shape[1]
    groups = heads_b // kv_heads
    experts = m_router.shape[2]

    c_all = jnp.concatenate([c_prompt, c_sample], axis=0)
    mods = [_mod_call(c_all, w_mod[l], b_mod[l]) for l in range(2)]
    kv_mod = _mod_call(c_all, kv_w_mod, kv_b_mod)

    def split(mod, n, lo, hi):
        m = mod[lo:hi].reshape(hi - lo, n, 1, d)
        return [m[:, j] for j in range(n)]

    group = jnp.arange(MXU_DIM) // HEAD_DIM
    router = jnp.pad(m_router[0], ((0, 0), (0, LANES - experts)))
    r_hi = router.astype(BF16)

    def sink_cols(rows_per_head):
        s = b_sinks[0].astype(F32).reshape(kv_heads, groups, 1)
        return jnp.broadcast_to(s, (kv_heads, groups, rows_per_head)).reshape(
            kv_heads, groups * rows_per_head, 1)

    w = dict(
        a_heads=heads_a, experts=experts,
        norm_g=[[norm_g[l, j].reshape(1, d) for j in range(2)] for l in range(2)],
        gsum=(group[:, None] == group[None, :]).astype(BF16),
        a_w_qkv=a_w_qkv[0].astype(BF16),
        a_q_gain=jnp.tile(a_q_norm[0].reshape(1, LANES), (1, heads_a)),
        a_k_gain=jnp.tile(a_k_norm[0].reshape(1, LANES), (1, heads_a)),
        a_lambda=a_lambda[0].astype(F32),
        a_subln=a_subln[0].reshape(1, LANES),
        a_w_o=a_w_o[0].astype(BF16),
        f_w_gu=f_w_gu[0].astype(BF16), f_w_d=f_w_d[0].astype(BF16),
        kv_norm_g=kv_norm_g.reshape(1, d),
        kv_w=kv_w.astype(BF16),
        kv_k_gain=jnp.tile(kv_k_norm.reshape(1, HEAD_DIM), (1, LANES // HEAD_DIM)),
        b_w_q=b_w_q[0].astype(BF16),
        b_q_gain=jnp.tile(b_q_norm[0].reshape(1, HEAD_DIM), (1, heads_b)),
        b_w_o=b_w_o[0].astype(BF16),
        sinks=b_sinks[0].astype(F32), kv_heads=kv_heads, sink_cols_decode=sink_cols(tn),
        r_hi=r_hi, r_lo=(router - r_hi.astype(F32)).astype(BF16),
        m_w_gu=m_w_gu[0].astype(BF16), m_w_d=m_w_d[0].astype(BF16),
    )

    rows = 512
    y_p, ak_p, av_p, bk_p, bv_p = _trunk(
        x_prompt, [split(m, 6, 0, bp) for m in mods], split(kv_mod, 2, 0, bp), w, None, rows)
    y_s, ak_s, av_s, bk_s, bv_s = _trunk(
        x_sample, [split(m, 6, bp, bp + bs) for m in mods], split(kv_mod, 2, bp, bp + bs), w,
        (cache_a_k, cache_a_v, cache_b_k, cache_b_v), rows)

    win = min(WIN_CHUNKS * CHUNK, t)
    a_k_shape = lambda b_, t_: (1, b_, t_, heads_a, 2, HEAD_DIM)
    a_v_shape = lambda b_, t_: (1, b_, t_, heads_a, 2 * HEAD_DIM)
    b_shape = lambda b_, t_: (b_, t_, kv_heads, HEAD_DIM)
    return (y_p, y_s,
            ak_p.reshape(a_k_shape(bp, t)), av_p.reshape(a_v_shape(bp, t)),
            bk_p[:, -win:].reshape(b_shape(bp, win)), bv_p[:, -win:].reshape(b_shape(bp, win)),
            ak_s.reshape(a_k_shape(bs, tn)), av_s.reshape(a_v_shape(bs, tn)),
            bk_s.reshape(b_shape(bs, tn)), bv_s.reshape(b_shape(bs, tn)))
```

```python
import functools
import math

import jax
import jax.numpy as jnp
from jax import lax
from jax.experimental import pallas as pl
from jax.experimental.pallas import tpu as pltpu

CHUNK = 64
WIN_CHUNKS = 2
ROT_DIM = 16
ROPE_THETA = 500000.0
EPS = 1e-6
HEAD_DIM = 64
N_ROUTED = 2
MOE_FF_CHUNK = 512

LANES = 128
MXU_DIM = 256
VMEM_LIMIT_BYTES = 56 * 1024 * 1024

F32 = jnp.float32
BF16 = jnp.bfloat16
NEG_INF = float("-inf")


def _cparams(*sem):
    return pltpu.CompilerParams(dimension_semantics=sem, vmem_limit_bytes=VMEM_LIMIT_BYTES)


def _row_tiling(batch, seq, rows):
    if seq >= rows:
        assert seq % rows == 0
        return 1, rows
    nb = max(1, min(batch, rows // seq))
    while batch % nb:
        nb -= 1
    return nb, seq


def _const_spec(shape):
    zeros = (0,) * len(shape)
    return pl.BlockSpec(shape, lambda *_: zeros, pipeline_mode=pl.Buffered(1))


def _sigmoid(x):
    return 1.0 / (1.0 + jnp.exp(-x))


def _rms_scale(x):
    return x * lax.rsqrt(jnp.mean(x * x, axis=-1, keepdims=True) + EPS)


def _modulate(x, g, shift, scale):
    return (_rms_scale(x) * g) * (1.0 + scale) + shift


def _group_norm_rope(x, gsum_ref, gain, cos, s1, s2):
    width = x.shape[1]
    slab = MXU_DIM if width % MXU_DIM == 0 else LANES
    gsum = gsum_ref[:slab, :slab]
    pieces = []
    for j in range(width // slab):
        xs = x[:, j * slab:(j + 1) * slab]
        ss = jnp.dot((xs * xs).astype(BF16), gsum, preferred_element_type=F32)
        y = xs * lax.rsqrt(ss * (1.0 / HEAD_DIM) + EPS) * gain[:, j * slab:(j + 1) * slab]
        for t in range(slab // LANES):
            yt = y[:, t * LANES:(t + 1) * LANES]
            pieces.append(yt * cos + pltpu.roll(yt, LANES - ROT_DIM // 2, 1) * s1
                          + pltpu.roll(yt, ROT_DIM // 2, 1) * s2)
    return pieces


def _tile_rows(t, nb):
    return t if nb == 1 else jnp.concatenate([t] * nb, axis=0)


def _mod_kernel(c_ref, w_ref, b_ref, o_ref):
    c = c_ref[...]
    sc = (c * _sigmoid(c)).astype(BF16)
    o_ref[...] = jnp.dot(sc, w_ref[...].astype(BF16), preferred_element_type=F32) + b_ref[...]


def _mod_call(c, w, b):
    rows, d = c.shape
    n = w.shape[1]
    tn = 2048
    assert n % tn == 0
    return pl.pallas_call(
        _mod_kernel,
        out_shape=jax.ShapeDtypeStruct((rows, n), F32),
        grid=(n // tn,),
        in_specs=[pl.BlockSpec((rows, d), lambda j: (0, 0)),
                  pl.BlockSpec((d, tn), lambda j: (0, j)),
                  pl.BlockSpec((1, tn), lambda j: (0, j))],
        out_specs=pl.BlockSpec((rows, tn), lambda j: (0, j)),
        compiler_params=_cparams("parallel"),
        name="adaln_mod",
    )(c, w, b.reshape(1, n))


def _qkv_a_kernel(x_ref, sh_ref, sc_ref, g_ref, w_ref, gsum_ref, qg_ref, kg_ref,
                  cos_ref, s1_ref, s2_ref, qh_ref, kh_ref, vh_ref, ak_ref, av_ref,
                  *, nb, tb, heads):
    d = x_ref.shape[-1]
    nqk = heads * 2 * HEAD_DIM
    qk_scale = HEAD_DIM ** -0.5 * math.log2(math.e)
    parts = 4 if (nb == 1 and tb % (4 * LANES) == 0) else 1
    tr = tb // parts
    for p in range(parts):
        rows = slice(p * tr, (p + 1) * tr)
        x = x_ref[:, rows, :]
        u = _modulate(x, g_ref[...], sh_ref[...], sc_ref[...]).reshape(nb * tr, d).astype(BF16)
        qkv = jnp.dot(u, w_ref[...], preferred_element_type=F32)
        cos, s1, s2 = (_tile_rows(r[rows, :], nb) for r in (cos_ref, s1_ref, s2_ref))
        q_p = _group_norm_rope(qkv[:, :nqk], gsum_ref, qg_ref[...], cos, s1, s2)
        k_p = _group_norm_rope(qkv[:, nqk:2 * nqk], gsum_ref, kg_ref[...], cos, s1, s2)
        for h in range(heads):
            cols = slice(h * LANES, (h + 1) * LANES)
            v_h = qkv[:, 2 * nqk + h * LANES:2 * nqk + (h + 1) * LANES].reshape(nb, tr, LANES)
            k_h = k_p[h].reshape(nb, tr, LANES)
            qh_ref[:, h, rows, :] = (q_p[h] * qk_scale).reshape(nb, tr, LANES).astype(BF16)
            kh_ref[:, h, rows, :] = k_h.astype(BF16)
            vh_ref[:, h, rows, :] = v_h.astype(BF16)
            ak_ref[:, rows, cols] = k_h
            av_ref[:, rows, cols] = v_h


def _qkv_a_call(x, shift, scale, g, w, gsum, qg, kg, tables, heads, rows):
    b, t, d = x.shape
    nb, tb = _row_tiling(b, t, rows)
    nt = t // tb
    n = w.shape[1]
    xmap = lambda i: (i // nt, i % nt, 0)
    mmap = lambda i: (i // nt, 0, 0)
    tmap = lambda i: (i % nt, 0)
    hmap = lambda i: (i // nt, 0, i % nt, 0)
    head_shape = jax.ShapeDtypeStruct((b, heads, t, LANES), BF16)
    flat_shape = jax.ShapeDtypeStruct((b, t, heads * LANES), F32)
    return pl.pallas_call(
        functools.partial(_qkv_a_kernel, nb=nb, tb=tb, heads=heads),
        out_shape=(head_shape, head_shape, head_shape, flat_shape, flat_shape),
        grid=(b // nb * nt,),
        in_specs=[pl.BlockSpec((nb, tb, d), xmap),
                  pl.BlockSpec((nb, 1, d), mmap),
                  pl.BlockSpec((nb, 1, d), mmap),
                  _const_spec((1, d)),
                  _const_spec((d, n)),
                  _const_spec((MXU_DIM, MXU_DIM)),
                  _const_spec((1, heads * LANES)),
                  _const_spec((1, heads * LANES)),
                  pl.BlockSpec((tb, LANES), tmap),
                  pl.BlockSpec((tb, LANES), tmap),
                  pl.BlockSpec((tb, LANES), tmap)],
        out_specs=(pl.BlockSpec((nb, heads, tb, LANES), hmap),) * 3
                  + (pl.BlockSpec((nb, tb, heads * LANES), xmap),) * 2,
        compiler_params=_cparams("parallel"),
        name="mixer_a_qkv",
    )(x, shift, scale, g, w, gsum, qg, kg, *tables)


def _split_halves(q):
    lane = lax.broadcasted_iota(jnp.int32, q.shape, 1)
    zero = jnp.zeros_like(q)
    return jnp.concatenate([jnp.where(lane < HEAD_DIM, q, zero),
                            jnp.where(lane >= HEAD_DIM, q, zero)], axis=0)


def _diff_lambda(lam_ref, lam_init):
    lv = lam_ref[...]
    t1 = jnp.sum(lv[0:1] * lv[1:2], axis=1, keepdims=True)
    t2 = jnp.sum(lv[2:3] * lv[3:4], axis=1, keepdims=True)
    return jnp.exp(t1) - jnp.exp(t2) + lam_init


def _diff_finish(o0, o1, lam_ref, g_ref, lam_init):
    o = o0 - _diff_lambda(lam_ref, lam_init) * o1
    return _rms_scale(o) * g_ref[...] * (1.0 - lam_init)


def _diff_attn_kernel(q_ref, k_ref, v_ref, lam_ref, g_ref, o_ref, s_ref, m_ref, l_ref, acc_ref,
                      *, tq, tk, hp, lam_init):
    qi = pl.program_id(2)
    n_full = (qi * tq) // tk
    qqs = [_split_halves(q_ref[0, h]) for h in range(hp)]
    m_ref[...] = jnp.full(m_ref.shape, NEG_INF, F32)
    l_ref[...] = jnp.zeros(l_ref.shape, F32)
    acc_ref[...] = jnp.zeros(acc_ref.shape, F32)

    def scores(h, j):
        kj = k_ref[0, h, pl.ds(pl.multiple_of(j * tk, tk), tk), :]
        return lax.dot_general(qqs[h], kj, (((1,), (1,)), ((), ())), preferred_element_type=F32)

    def absorb(h, j, s):
        vj = v_ref[0, h, pl.ds(pl.multiple_of(j * tk, tk), tk), :]
        m_prev = m_ref[h]
        m_new = jnp.maximum(m_prev, jnp.max(s, axis=1, keepdims=True))
        alpha = jnp.exp2(m_prev - m_new)
        chunks = [jnp.exp2(s[:, c * LANES:(c + 1) * LANES] - m_new) for c in range(tk // LANES)]
        l_ref[h] = alpha * l_ref[h] + functools.reduce(lambda a, b: a + b, chunks)
        p = jnp.concatenate(chunks, axis=1).astype(BF16)
        acc_ref[h] = alpha * acc_ref[h] + jnp.dot(p, vj, preferred_element_type=F32)
        m_ref[h] = m_new

    for h in range(hp):
        s_ref[h] = scores(h, 0)

    def body(j, carry):
        for h in range(hp):
            s = s_ref[h]
            s_ref[h] = scores(h, j + 1)
            absorb(h, j, s)
        return carry

    lax.fori_loop(0, n_full, body, 0)
    q_chunk = (qi * tq + lax.broadcasted_iota(jnp.int32, (2 * tq, tk), 0) % tq) // CHUNK
    k_chunk = (n_full * tk + lax.broadcasted_iota(jnp.int32, (2 * tq, tk), 1)) // CHUNK
    bias = jnp.where(k_chunk <= q_chunk, 0.0, NEG_INF)
    for h in range(hp):
        absorb(h, n_full, s_ref[h] + bias)

    for h in range(hp):
        o = acc_ref[h] / jnp.sum(l_ref[h], axis=1, keepdims=True)
        o_ref[0, :, h * LANES:(h + 1) * LANES] = _diff_finish(
            o[:tq], o[tq:], lam_ref, g_ref, lam_init).astype(BF16)


def _diff_attn_call(qh, kh, vh, lam_vec, subln, lam_init, tq, tk, hp):
    b, heads, t, _ = qh.shape
    assert t % tk == 0 and tk % tq == 0 and tq % CHUNK == 0 and heads % hp == 0
    return pl.pallas_call(
        functools.partial(_diff_attn_kernel, tq=tq, tk=tk, hp=hp, lam_init=lam_init),
        out_shape=jax.ShapeDtypeStruct((b, t, heads * LANES), BF16),
        grid=(b, heads // hp, t // tq),
        in_specs=[pl.BlockSpec((1, hp, tq, LANES), lambda bi, h, i: (bi, h, i, 0)),
                  pl.BlockSpec((1, hp, t, LANES), lambda bi, h, i: (bi, h, 0, 0), pipeline_mode=pl.Buffered(1)),
                  pl.BlockSpec((1, hp, t, LANES), lambda bi, h, i: (bi, h, 0, 0), pipeline_mode=pl.Buffered(1)),
                  pl.BlockSpec(lam_vec.shape, lambda bi, h, i: (0, 0)),
                  pl.BlockSpec((1, LANES), lambda bi, h, i: (0, 0))],
        out_specs=pl.BlockSpec((1, tq, hp * LANES), lambda bi, h, i: (bi, i, h)),
        scratch_shapes=[pltpu.VMEM((hp, 2 * tq, tk), F32)] + [pltpu.VMEM((hp, 2 * tq, LANES), F32)] * 3,
        compiler_params=_cparams("parallel", "parallel", "arbitrary"),
        name="mixer_a_attention",
    )(qh, kh, vh, lam_vec, subln)


def _pad_rows(a, rows):
    return jnp.concatenate([a, jnp.zeros((rows - a.shape[0], a.shape[1]), a.dtype)], axis=0)


def _diff_decode_kernel(q_ref, kp_ref, vp_ref, kn_ref, vn_ref, lam_ref, g_ref, o_ref,
                        *, past, lam_init):
    tn = q_ref.shape[2]
    qq = _split_halves(q_ref[0, 0])
    kp = kp_ref[0].astype(BF16)
    vp = vp_ref[0].astype(BF16)
    kn = _pad_rows(kn_ref[0, 0], LANES)
    vn = _pad_rows(vn_ref[0, 0], LANES)
    dims = (((1,), (1,)), ((), ()))
    s_p = lax.dot_general(qq, kp, dims, preferred_element_type=F32)
    s_n = lax.dot_general(qq, kn, dims, preferred_element_type=F32)

    def visible(shape, k_off, k_count):
        q_pos = past + lax.broadcasted_iota(jnp.int32, shape, 0) % tn
        col = lax.broadcasted_iota(jnp.int32, shape, 1)
        return ((k_off + col) // CHUNK <= q_pos // CHUNK) & (col < k_count)

    s_p = jnp.where(visible(s_p.shape, 0, past), s_p, NEG_INF)
    s_n = jnp.where(visible(s_n.shape, past, tn), s_n, NEG_INF)
    m = jnp.maximum(jnp.max(s_p, axis=1, keepdims=True), jnp.max(s_n, axis=1, keepdims=True))
    e_p = jnp.exp2(s_p - m)
    e_n = jnp.exp2(s_n - m)
    l = jnp.sum(e_p, axis=1, keepdims=True) + jnp.sum(e_n, axis=1, keepdims=True)
    acc = (jnp.dot(e_p.astype(BF16), vp, preferred_element_type=F32)
           + jnp.dot(e_n.astype(BF16), vn, preferred_element_type=F32))
    o = acc / l
    o_ref[0] = _diff_finish(o[:tn], o[tn:], lam_ref, g_ref, lam_init).astype(BF16)


def _diff_decode_call(qh, kh, vh, past_k, past_v, lam_vec, subln, lam_init):
    b, heads, tn, _ = qh.shape
    past = past_k.shape[1]
    new_spec = pl.BlockSpec((1, 1, tn, LANES), lambda bi, h: (bi, h, 0, 0))
    past_spec = pl.BlockSpec((1, past, LANES), lambda bi, h: (bi, 0, h))
    return pl.pallas_call(
        functools.partial(_diff_decode_kernel, past=past, lam_init=lam_init),
        out_shape=jax.ShapeDtypeStruct((b, tn, heads * LANES), BF16),
        grid=(b, heads),
        in_specs=[new_spec, past_spec, past_spec, new_spec, new_spec,
                  pl.BlockSpec(lam_vec.shape, lambda bi, h: (0, 0)),
                  pl.BlockSpec((1, LANES), lambda bi, h: (0, 0))],
        out_specs=pl.BlockSpec((1, tn, LANES), lambda bi, h: (bi, 0, h)),
        compiler_params=_cparams("parallel", "parallel"),
        name="mixer_a_decode_attention",
    )(qh, past_k, past_v, kh, vh, lam_vec, subln)


def _ffn_kernel(o_ref, x_ref, ogate_ref, wo_ref, sh_ref, sc_ref, gate_ref, g_ref, wgu_ref, wd_ref,
                out_ref, *, nb, tb, ff, fc):
    d = x_ref.shape[-1]
    y = jnp.dot(o_ref[...].reshape(nb * tb, o_ref.shape[-1]), wo_ref[...],
                preferred_element_type=F32)
    h = x_ref[...] + ogate_ref[...] * y.reshape(nb, tb, d)
    u = _modulate(h, g_ref[...], sh_ref[...], sc_ref[...]).reshape(nb * tb, d).astype(BF16)
    acc = jnp.zeros((nb * tb, d), F32)
    for c in range(ff // fc):
        g = jnp.dot(u, wgu_ref[:, c * fc:(c + 1) * fc], preferred_element_type=F32)
        up = jnp.dot(u, wgu_ref[:, ff + c * fc:ff + (c + 1) * fc], preferred_element_type=F32)
        a = ((g * _sigmoid(g)) * up).astype(BF16)
        acc = acc + jnp.dot(a, wd_ref[c * fc:(c + 1) * fc, :], preferred_element_type=F32)
    out_ref[...] = h + gate_ref[...] * acc.reshape(nb, tb, d)


def _ffn_call(o, x, o_gate, w_o, shift, scale, gate, g, w_gu, w_d, rows):
    b, t, d = x.shape
    k = o.shape[-1]
    ff = w_d.shape[0]
    fc = ff // 2 if (ff // 2) % LANES == 0 else ff
    nb, tb = _row_tiling(b, t, rows)
    nt = t // tb
    xmap = lambda i: (i // nt, i % nt, 0)
    mspec = pl.BlockSpec((nb, 1, d), lambda i: (i // nt, 0, 0))
    return pl.pallas_call(
        functools.partial(_ffn_kernel, nb=nb, tb=tb, ff=ff, fc=fc),
        out_shape=jax.ShapeDtypeStruct(x.shape, F32),
        grid=(b // nb * nt,),
        in_specs=[pl.BlockSpec((nb, tb, k), xmap),
                  pl.BlockSpec((nb, tb, d), xmap),
                  mspec,
                  _const_spec((k, d)),
                  mspec, mspec, mspec,
                  _const_spec((1, d)),
                  _const_spec((d, 2 * ff)),
                  _const_spec((ff, d))],
        out_specs=pl.BlockSpec((nb, tb, d), xmap),
        compiler_params=_cparams("parallel"),
        name="out_proj_dense_swiglu",
    )(o, x, o_gate, w_o, shift, scale, gate, g, w_gu, w_d)


def _qkv_b_kernel(x_ref, shq_ref, scq_ref, gq_ref, shkv_ref, sckv_ref, gkv_ref, wq_ref, wkv_ref,
                  gsum_ref, qn_ref, kn_ref, cos_ref, s1_ref, s2_ref, q_ref, k_ref, v_ref,
                  *, nb, tb):
    d = x_ref.shape[-1]
    qk_scale = HEAD_DIM ** -0.5
    parts = 4 if (nb == 1 and tb % (4 * LANES) == 0) else 1
    tr = tb // parts
    for p in range(parts):
        rows = slice(p * tr, (p + 1) * tr)
        y = _rms_scale(x_ref[:, rows, :])
        uq = ((y * gq_ref[...]) * (1.0 + scq_ref[...]) + shq_ref[...]).reshape(nb * tr, d)
        ukv = ((y * gkv_ref[...]) * (1.0 + sckv_ref[...]) + shkv_ref[...]).reshape(nb * tr, d)
        q = jnp.dot(uq.astype(BF16), wq_ref[...], preferred_element_type=F32)
        kv = jnp.dot(ukv.astype(BF16), wkv_ref[...], preferred_element_type=F32)
        cos, s1, s2 = (_tile_rows(r[rows, :], nb) for r in (cos_ref, s1_ref, s2_ref))
        q_p = _group_norm_rope(q, gsum_ref, qn_ref[...], cos, s1, s2)
        k_p = _group_norm_rope(kv[:, :LANES], gsum_ref, kn_ref[...], cos, s1, s2)
        for j, piece in enumerate(q_p):
            q_ref[:, rows, j * LANES:(j + 1) * LANES] = (
                piece * qk_scale).reshape(nb, tr, LANES).astype(BF16)
        k_ref[:, rows, :] = k_p[0].reshape(nb, tr, LANES)
        v_ref[:, rows, :] = kv[:, LANES:].reshape(nb, tr, LANES)


def _qkv_b_call(x, mod_q, mod_kv, wq, wkv, gsum, qn, kn, tables, rows):
    b, t, d = x.shape
    nq = wq.shape[1]
    assert wkv.shape[1] == 2 * LANES
    nb, tb = _row_tiling(b, t, rows)
    nt = t // tb
    xmap = lambda i: (i // nt, i % nt, 0)
    mmap = lambda i: (i // nt, 0, 0)
    tmap = lambda i: (i % nt, 0)
    mspec = pl.BlockSpec((nb, 1, d), mmap)
    return pl.pallas_call(
        functools.partial(_qkv_b_kernel, nb=nb, tb=tb),
        out_shape=(jax.ShapeDtypeStruct((b, t, nq), BF16),
                   jax.ShapeDtypeStruct((b, t, LANES), F32),
                   jax.ShapeDtypeStruct((b, t, LANES), F32)),
        grid=(b // nb * nt,),
        in_specs=[pl.BlockSpec((nb, tb, d), xmap),
                  mspec, mspec, _const_spec((1, d)),
                  mspec, mspec, _const_spec((1, d)),
                  _const_spec((d, nq)),
                  _const_spec((d, 2 * LANES)),
                  _const_spec((MXU_DIM, MXU_DIM)),
                  _const_spec((1, nq)),
                  _const_spec((1, LANES)),
                  pl.BlockSpec((tb, LANES), tmap),
                  pl.BlockSpec((tb, LANES), tmap),
                  pl.BlockSpec((tb, LANES), tmap)],
        out_specs=(pl.BlockSpec((nb, tb, nq), xmap),
                   pl.BlockSpec((nb, tb, LANES), xmap),
                   pl.BlockSpec((nb, tb, LANES), xmap)),
        compiler_params=_cparams("parallel"),
        name="mixer_b_qkv",
    )(x, mod_q[0], mod_q[1], mod_q[2], mod_kv[0], mod_kv[1], mod_kv[2], wq, wkv, gsum, qn, kn,
      *tables)


def _tile_kv_head(a, n, groups):
    lane = lax.broadcasted_iota(jnp.int32, a.shape, 1)
    swapped = pltpu.roll(a, HEAD_DIM, 1)
    keep = (lane < HEAD_DIM) if n == 0 else (lane >= HEAD_DIM)
    pair = jnp.where(keep, a, swapped).astype(BF16)
    return jnp.concatenate([pair] * (groups // 2), axis=1)


def _sink_group_attention(qs, kt, vt, mask, sink_col, groups):
    rows, width = qs.shape
    head_of_lane = lax.broadcasted_iota(jnp.int32, (rows, width), 1) // HEAD_DIM
    zero = jnp.zeros_like(qs)
    lhs = jnp.concatenate([jnp.where(head_of_lane == h, qs, zero) for h in range(groups)], axis=0)
    s = lax.dot_general(lhs, kt, (((1,), (1,)), ((), ())), preferred_element_type=F32)
    s = jnp.where(mask, s, NEG_INF)
    m = jnp.maximum(jnp.max(s, axis=1, keepdims=True), sink_col)
    e = jnp.exp(s - m)
    den = jnp.sum(e, axis=1, keepdims=True) + jnp.exp(sink_col - m)
    ob = jnp.dot((e / den).astype(BF16), vt, preferred_element_type=F32)
    out = jnp.zeros((rows, width), F32)
    for h in range(groups):
        out = out + jnp.where(head_of_lane == h, ob[h * rows:(h + 1) * rows], 0.0)
    return out


def _window_attn_kernel(q_ref, kprev_ref, kcur_ref, vprev_ref, vcur_ref, sink_ref, o_ref,
                        *, tq, groups, kv_heads):
    q = q_ref[0]
    kc = jnp.concatenate([kprev_ref[0], kcur_ref[0]], axis=0)
    vc = jnp.concatenate([vprev_ref[0], vcur_ref[0]], axis=0)
    win = kprev_ref.shape[1]
    keys = win + tq
    first_chunk = pl.program_id(1) * (tq // CHUNK)
    q_chunk = lax.broadcasted_iota(jnp.int32, (tq, keys), 0) // CHUNK
    k_chunk = lax.broadcasted_iota(jnp.int32, (tq, keys), 1) // CHUNK - win // CHUNK
    visible = ((k_chunk <= q_chunk) & (k_chunk >= q_chunk - WIN_CHUNKS)
               & (k_chunk + first_chunk >= 0))
    bias = jnp.where(visible, 0.0, NEG_INF)
    lane = lax.broadcasted_iota(jnp.int32, (tq, LANES), 1)
    in_half = (lane < HEAD_DIM, lane >= HEAD_DIM)
    zero = jnp.zeros((tq, LANES), BF16)
    pairs = groups // 2
    for n in range(kv_heads):
        kt = _tile_kv_head(kc, n, 2)
        vt = _tile_kv_head(vc, n, 2)
        for j in range(pairs):
            blk = n * pairs + j
            qp = q[:, blk * LANES:(blk + 1) * LANES]
            halves = []
            for half in range(2):
                sink = sink_ref[2 * blk + half]
                qm = jnp.where(in_half[half], qp, zero)
                s = lax.dot_general(qm, kt, (((1,), (1,)), ((), ())), preferred_element_type=F32)
                s = s + bias
                m = jnp.maximum(jnp.max(s, axis=1, keepdims=True), sink)
                e = jnp.exp(s - m)
                den = jnp.sum(e, axis=1, keepdims=True) + jnp.exp(sink - m)
                halves.append(jnp.dot(e.astype(BF16), vt, preferred_element_type=F32) / den)
            o_ref[0, :, blk * LANES:(blk + 1) * LANES] = jnp.where(
                in_half[0], halves[0], halves[1]).astype(BF16)


def _window_attn_call(q, k, v, sinks, kv_heads, tq):
    b, t, nq = q.shape
    groups = nq // (kv_heads * HEAD_DIM)
    win = WIN_CHUNKS * CHUNK
    assert t % tq == 0 and tq % win == 0 and groups % 2 == 0 and win == LANES
    ratio = tq // win
    cur_spec = pl.BlockSpec((1, tq, LANES), lambda bi, i: (bi, i, 0))
    prev_spec = pl.BlockSpec((1, win, LANES), lambda bi, i: (bi, jnp.maximum(i * ratio - 1, 0), 0))
    return pl.pallas_call(
        functools.partial(_window_attn_kernel, tq=tq, groups=groups, kv_heads=kv_heads),
        out_shape=jax.ShapeDtypeStruct((b, t, nq), BF16),
        grid=(b, t // tq),
        in_specs=[pl.BlockSpec((1, tq, nq), lambda bi, i: (bi, i, 0)),
                  prev_spec, cur_spec, prev_spec, cur_spec,
                  pl.BlockSpec(memory_space=pltpu.SMEM)],
        out_specs=pl.BlockSpec((1, tq, nq), lambda bi, i: (bi, i, 0)),
        compiler_params=_cparams("parallel", "parallel"),
        name="mixer_b_window_attention",
    )(q, k, k, v, v, sinks)


def _window_decode_kernel(q_ref, kp_ref, kn_ref, vp_ref, vn_ref, sink_ref, o_ref,
                          *, past_len, groups, kv_heads):
    q = q_ref[0]
    tn = q.shape[0]
    win = kp_ref.shape[1]
    keys = 2 * LANES
    kc = _pad_rows(jnp.concatenate([kp_ref[0], kn_ref[0]], axis=0), keys)
    vc = _pad_rows(jnp.concatenate([vp_ref[0], vn_ref[0]], axis=0), keys)
    width = groups * HEAD_DIM
    rows = groups * tn
    shape = (rows, keys)
    col = lax.broadcasted_iota(jnp.int32, shape, 1)
    q_chunk = (past_len + lax.broadcasted_iota(jnp.int32, shape, 0) % tn) // CHUNK
    k_pos = past_len - win + col
    k_chunk = k_pos // CHUNK
    mask = ((k_chunk <= q_chunk) & (k_chunk >= q_chunk - WIN_CHUNKS) & (k_pos >= 0)
            & (col < win + tn))
    outs = []
    for n in range(kv_heads):
        outs.append(_sink_group_attention(q[:, n * width:(n + 1) * width],
                                          _tile_kv_head(kc, n, groups), _tile_kv_head(vc, n, groups),
                                          mask, sink_ref[n], groups))
    o_ref[0] = jnp.concatenate(outs, axis=1).astype(BF16)


def _window_decode_call(q, k_new, v_new, past_k, past_v, sink_cols, past_len):
    b, tn, nq = q.shape
    win = past_k.shape[1]
    kv_heads = sink_cols.shape[0]
    groups = nq // (kv_heads * HEAD_DIM)
    assert win + tn <= 2 * LANES
    new_spec = pl.BlockSpec((1, tn, LANES), lambda bi: (bi, 0, 0))
    past_spec = pl.BlockSpec((1, win, LANES), lambda bi: (bi, 0, 0))
    return pl.pallas_call(
        functools.partial(_window_decode_kernel, past_len=past_len, groups=groups, kv_heads=kv_heads),
        out_shape=jax.ShapeDtypeStruct((b, tn, nq), BF16),
        grid=(b,),
        in_specs=[pl.BlockSpec((1, tn, nq), lambda bi: (bi, 0, 0)),
                  past_spec, new_spec, past_spec, new_spec,
                  pl.BlockSpec(sink_cols.shape, lambda bi: (0, 0, 0))],
        out_specs=pl.BlockSpec((1, tn, nq), lambda bi: (bi, 0, 0)),
        compiler_params=_cparams("parallel"),
        name="mixer_b_decode_attention",
    )(q, past_k, k_new, past_v, v_new, sink_cols)


def _route_top2(logits, experts):
    lane = lax.broadcasted_iota(jnp.int32, logits.shape, 1).astype(F32)
    big = float(LANES)
    lg = jnp.where(lane < experts, logits, NEG_INF)
    m1 = jnp.max(lg, axis=1, keepdims=True)
    i1 = jnp.min(jnp.where(lg == m1, lane, big), axis=1, keepdims=True)
    lg2 = jnp.where(lane == i1, NEG_INF, lg)
    m2 = jnp.max(lg2, axis=1, keepdims=True)
    i2 = jnp.min(jnp.where(lg2 == m2, lane, big), axis=1, keepdims=True)
    e2 = jnp.exp(m2 - m1)
    den = 1.0 + e2
    return jnp.where(lane == i1, 1.0 / den, 0.0) + jnp.where(lane == i2, e2 / den, 0.0)


def _moe_kernel(o_ref, x_ref, ogate_ref, wo_ref, sh_ref, sc_ref, gate_ref, g_ref, rhi_ref, rlo_ref,
                wg_ref, wu_ref, wd_ref, out_ref, u_s, rnk_s, gat_s, rnkt_s, cnt_s, xg_s, ye_s,
                *, nb, tb, experts):
    e = pl.program_id(1)
    k = pl.program_id(2)
    last_k = pl.num_programs(2) - 1
    d = x_ref.shape[-1]
    tm = nb * tb
    r2 = min(2 * MXU_DIM, tm)
    sizes = [r2] + [r for r in (r2 // 2, r2 // 4) if r >= LANES]
    kc = min(2 * MXU_DIM, tm)
    rc = min(MXU_DIM, tm)
    assert nb == 1 or rc == tm

    @pl.when((e == 0) & (k == 0))
    def _route():
        row_i = lax.broadcasted_iota(jnp.int32, (rc, rc), 0)
        col_i = lax.broadcasted_iota(jnp.int32, (rc, rc), 1)
        below = jnp.where(col_i < row_i, 1.0, 0.0).astype(BF16)

        def chunk(r, seen):
            rows = pl.ds(pl.multiple_of(r * rc, rc), rc)
            ox, xx = (o_ref[:, rows, :], x_ref[:, rows, :]) if nb == 1 else (o_ref[...], x_ref[...])
            y = jnp.dot(ox.reshape(rc, ox.shape[-1]), wo_ref[...], preferred_element_type=F32)
            hx = xx + ogate_ref[...] * y.reshape(xx.shape)
            if nb == 1:
                out_ref[:, rows, :] = hx
            else:
                out_ref[...] = hx
            u = _modulate(hx, g_ref[...], sh_ref[...], sc_ref[...]).reshape(rc, d)
            ub = u.astype(BF16)
            ulo = (u - ub.astype(F32)).astype(BF16)
            rhi = rhi_ref[...]
            logits = (jnp.dot(ub, rhi, preferred_element_type=F32)
                      + jnp.dot(ub, rlo_ref[...], preferred_element_type=F32)
                      + jnp.dot(ulo, rhi, preferred_element_type=F32))
            dg = _route_top2(logits, experts)
            sel = dg > 0.0
            sel_f = jnp.where(sel, 1.0, 0.0)
            rank = jnp.dot(below, sel_f.astype(BF16), preferred_element_type=F32) + seen
            u_s[rows, :] = ub
            gat_s[rows, :] = dg
            rnk_s[rows, :] = jnp.where(sel, rank, -1.0)
            return seen + jnp.sum(sel_f, axis=0, keepdims=True)

        counts = lax.fori_loop(0, tm // rc, chunk, jnp.zeros((1, LANES), F32))
        for x in range(experts):
            cnt_s[x] = counts[0, x].astype(jnp.int32)
        rnkt_s[...] = rnk_s[...].T[:rnkt_s.shape[0]]

    count = cnt_s[e]
    whole = count // r2
    rest = count - whole * r2
    n_full = whole + (rest > sizes[1]).astype(jnp.int32) if len(sizes) > 1 else (count + r2 - 1) // r2

    def for_each_tile(fn):
        def full_tile(t, carry):
            fn(pl.multiple_of(t * r2, r2), r2)
            return carry

        lax.fori_loop(0, n_full, full_tile, 0)
        for i in range(1, len(sizes)):
            lo = sizes[i + 1] if i + 1 < len(sizes) else 0

            @pl.when((rest > lo) & (rest <= sizes[i]))
            def _small_tile(rows=sizes[i]):
                fn(pl.multiple_of(whole * r2, r2), rows)

    def gather_rows(base, rows):
        want = (base + lax.broadcasted_iota(jnp.int32, (rows, kc), 0)).astype(F32)
        acc = jnp.zeros((rows, d), F32)
        for c in range(tm // kc):
            pos = rnkt_s[pl.ds(e, 1), c * kc:(c + 1) * kc]
            pick = jnp.where(pos == want, 1.0, 0.0).astype(BF16)
            acc = acc + jnp.dot(pick, u_s[c * kc:(c + 1) * kc, :], preferred_element_type=F32)
        xg_s[pl.ds(base, rows), :] = acc.astype(BF16)
        ye_s[pl.ds(base, rows), :] = jnp.zeros((rows, d), F32)

    def expert_rows(base, rows):
        x = xg_s[pl.ds(base, rows), :]
        g = jnp.dot(x, wg_ref[0], preferred_element_type=F32)
        up = jnp.dot(x, wu_ref[0], preferred_element_type=F32)
        a = ((g * _sigmoid(g)) * up).astype(BF16)
        ye_s[pl.ds(base, rows), :] += jnp.dot(a, wd_ref[0], preferred_element_type=F32)

    @pl.when(k == 0)
    def _gather():
        for_each_tile(gather_rows)

    for_each_tile(expert_rows)

    @pl.when(k == last_k)
    def _scatter():
        lane = lax.broadcasted_iota(jnp.int32, (tm, LANES), 1)
        pos = jnp.sum(jnp.where(lane == e, rnk_s[...], 0.0), axis=1, keepdims=True)
        gate = jnp.sum(jnp.where(lane == e, gat_s[...], 0.0), axis=1, keepdims=True)

        def scatter_rows(base, rows):
            want = (base + lax.broadcasted_iota(jnp.int32, (tm, rows), 1)).astype(F32)
            place = jnp.where(pos == want, gate, 0.0).astype(BF16)
            y = ye_s[pl.ds(base, rows), :].astype(BF16)
            out_ref[...] += gate_ref[...] * jnp.dot(
                place, y, preferred_element_type=F32).reshape(nb, tb, d)

        for_each_tile(scatter_rows)


def _moe_call(o, x, o_gate, w_o, shift, scale, gate, g, r_hi, r_lo, w_gu, w_d, experts, rows):
    b, t, d = x.shape
    ko = o.shape[-1]
    ff = w_d.shape[1]
    fc = MOE_FF_CHUNK
    assert ff % fc == 0
    nk = ff // fc
    nb, tb = _row_tiling(b, t, rows)
    nt = t // tb
    tm = nb * tb
    xmap = lambda i, e, k: (i // nt, i % nt, 0)
    mmap = lambda i, e, k: (i // nt, 0, 0)
    mspec = pl.BlockSpec((nb, 1, d), mmap)
    sub = 8
    assert experts <= sub and tm % min(2 * MXU_DIM, tm) == 0
    return pl.pallas_call(
        functools.partial(_moe_kernel, nb=nb, tb=tb, experts=experts),
        out_shape=jax.ShapeDtypeStruct(x.shape, F32),
        grid=(b // nb * nt, experts, nk),
        in_specs=[pl.BlockSpec((nb, tb, ko), xmap, pipeline_mode=pl.Buffered(1)),
                  pl.BlockSpec((nb, tb, d), xmap, pipeline_mode=pl.Buffered(1)),
                  mspec,
                  _const_spec((ko, d)),
                  mspec, mspec, mspec,
                  _const_spec((1, d)),
                  _const_spec((d, LANES)),
                  _const_spec((d, LANES)),
                  pl.BlockSpec((1, d, fc), lambda i, e, k: (e, 0, k)),
                  pl.BlockSpec((1, d, fc), lambda i, e, k: (e, 0, k + nk)),
                  pl.BlockSpec((1, fc, d), lambda i, e, k: (e, k, 0))],
        out_specs=pl.BlockSpec((nb, tb, d), xmap, pipeline_mode=pl.Buffered(1)),
        scratch_shapes=[pltpu.VMEM((tm, d), BF16),
                        pltpu.VMEM((tm, LANES), F32),
                        pltpu.VMEM((tm, LANES), F32),
                        pltpu.VMEM((sub, tm), F32),
                        pltpu.SMEM((sub,), jnp.int32),
                        pltpu.VMEM((tm, d), BF16),
                        pltpu.VMEM((tm, d), F32)],
        compiler_params=_cparams("parallel", "arbitrary", "arbitrary"),
        name="expert_swiglu",
    )(o, x, o_gate, w_o, shift, scale, gate, g, r_hi, r_lo, w_gu, w_gu, w_d)


def _rope_tables(pos):
    half = ROT_DIM // 2
    inv = ROPE_THETA ** (-(jnp.arange(half, dtype=F32) * 2.0 / ROT_DIM))
    ang = pos.astype(F32)[:, None] * inv[None, :]
    cos, sin = jnp.cos(ang), jnp.sin(ang)
    t = pos.shape[0]
    zeros = lambda n: jnp.zeros((t, n), F32)
    c = jnp.concatenate([cos, cos, jnp.ones((t, HEAD_DIM - ROT_DIM), F32)], axis=1)
    s1 = jnp.concatenate([-sin, zeros(HEAD_DIM - half)], axis=1)
    s2 = jnp.concatenate([zeros(half), sin, zeros(HEAD_DIM - ROT_DIM)], axis=1)
    rep = LANES // HEAD_DIM
    return tuple(jnp.tile(a, (1, rep)) for a in (c, s1, s2))


def _mod_rows(mod, lo, hi):
    d = mod.shape[1]
    return mod[lo:hi].reshape(hi - lo, 1, d)


def _trunk(x, mods, kv_mods, w, past, rows):
    b, t, d = x.shape
    past_len = 0 if past is None else past[0].shape[2]
    pos = past_len + jnp.arange(t, dtype=jnp.int32)
    tables = _rope_tables(pos)
    heads_a = w["a_heads"]
    big = t >= rows

    m0 = mods[0]
    lam_init = 0.8 - 0.6 * math.exp(-0.3 * 0)
    qh, kh, vh, a_k, a_v = _qkv_a_call(x, m0[0], m0[1], w["norm_g"][0][0], w["a_w_qkv"], w["gsum"],
                                       w["a_q_gain"], w["a_k_gain"], tables, heads_a, rows)
    if past is None:
        o = _diff_attn_call(qh, kh, vh, w["a_lambda"], w["a_subln"], lam_init, 256, 512, 8)
    else:
        pk = past[0][0].reshape(b, past_len, heads_a * LANES)
        pv = past[1][0].reshape(b, past_len, heads_a * LANES)
        o = _diff_decode_call(qh, kh, vh, pk, pv, w["a_lambda"], w["a_subln"], lam_init)
    h = _ffn_call(o, x, m0[2], w["a_w_o"], m0[3], m0[4], m0[5], w["norm_g"][0][1],
                  w["f_w_gu"], w["f_w_d"], rows)

    m1 = mods[1]
    q, k_sh, v_sh = _qkv_b_call(h, (m1[0], m1[1], w["norm_g"][1][0]),
                                (kv_mods[0], kv_mods[1], w["kv_norm_g"]),
                                w["b_w_q"], w["kv_w"], w["gsum"], w["b_q_gain"], w["kv_k_gain"],
                                tables, rows)
    if past is None:
        o = _window_attn_call(q, k_sh, v_sh, w["sinks"], w["kv_heads"], 256)
    else:
        win = past[2].shape[1]
        o = _window_decode_call(q, k_sh, v_sh, past[2].reshape(b, win, LANES),
                                past[3].reshape(b, win, LANES), w["sink_cols_decode"], past_len)
    h = _moe_call(o, h, m1[2], w["b_w_o"], m1[3], m1[4], m1[5], w["norm_g"][1][1], w["r_hi"], w["r_lo"],
                  w["m_w_gu"], w["m_w_d"], w["experts"], min(2048, t) if big else rows)
    return h, a_k, a_v, k_sh, v_sh


def kernel(x_prompt, x_sample, c_prompt, c_sample, cache_a_k, cache_a_v, cache_b_k, cache_b_v,
           w_mod, b_mod, norm_g, a_w_qkv, a_q_norm, a_k_norm, a_lambda, a_subln, a_w_o,
           kv_norm_g, kv_w_mod, kv_b_mod, kv_w, kv_k_norm, b_w_q, b_q_norm, b_sinks, b_w_o,
           f_w_gu, f_w_d, m_router, m_w_gu, m_w_d):
    bp, t, d = x_prompt.shape
    bs, tn, _ = x_sample.shape
    assert w_mod.shape[0] == 2 and a_w_qkv.shape[0] == 1 and b_w_q.shape[0] == 1
    heads_a = cache_a_k.shape[3]
    kv_heads = cache_b_k.shape[2]
    heads_b = b_sinks.shape[1]
    groups = heads_b // kv_heads
    experts = m_router.shape[2]

    c_all = jnp.concatenate([c_prompt, c_sample], axis=0)
    mods = [_mod_call(c_all, w_mod[l], b_mod[l]) for l in range(2)]
    kv_mod = _mod_call(c_all, kv_w_mod, kv_b_mod)

    def split(mod, n, lo, hi):
        m = mod[lo:hi].reshape(hi - lo, n, 1, d)
        return [m[:, j] for j in range(n)]

    group = jnp.arange(MXU_DIM) // HEAD_DIM
    router = jnp.pad(m_router[0], ((0, 0), (0, LANES - experts)))
    r_hi = router.astype(BF16)

    def sink_cols(rows_per_head):
        s = b_sinks[0].astype(F32).reshape(kv_heads, groups, 1)
        return jnp.broadcast_to(s, (kv_heads, groups, rows_per_head)).reshape(
            kv_heads, groups * rows_per_head, 1)

    w = dict(
        a_heads=heads_a, experts=experts,
        norm_g=[[norm_g[l, j].reshape(1, d) for j in range(2)] for l in range(2)],
        gsum=(group[:, None] == group[None, :]).astype(BF16),
        a_w_qkv=a_w_qkv[0].astype(BF16),
        a_q_gain=jnp.tile(a_q_norm[0].reshape(1, LANES), (1, heads_a)),
        a_k_gain=jnp.tile(a_k_norm[0].reshape(1, LANES), (1, heads_a)),
        a_lambda=a_lambda[0].astype(F32),
        a_subln=a_subln[0].reshape(1, LANES),
        a_w_o=a_w_o[0].astype(BF16),
        f_w_gu=f_w_gu[0].astype(BF16), f_w_d=f_w_d[0].astype(BF16),
        kv_norm_g=kv_norm_g.reshape(1, d),
        kv_w=kv_w.astype(BF16),
        kv_k_gain=jnp.tile(kv_k_norm.reshape(1, HEAD_DIM), (1, LANES // HEAD_DIM)),
        b_w_q=b_w_q[0].astype(BF16),
        b_q_gain=jnp.tile(b_q_norm[0].reshape(1, HEAD_DIM), (1, heads_b)),
        b_w_o=b_w_o[0].astype(BF16),
        sinks=b_sinks[0].astype(F32), kv_heads=kv_heads, sink_cols_decode=sink_cols(tn),
        r_hi=r_hi, r_lo=(router - r_hi.astype(F32)).astype(BF16),
        m_w_gu=m_w_gu[0].astype(BF16), m_w_d=m_w_d[0].astype(BF16),
    )

    rows = 512
    y_p, ak_p, av_p, bk_p, bv_p = _trunk(
        x_prompt, [split(m, 6, 0, bp) for m in mods], split(kv_mod, 2, 0, bp), w, None, rows)
    y_s, ak_s, av_s, bk_s, bv_s = _trunk(
        x_sample, [split(m, 6, bp, bp + bs) for m in mods], split(kv_mod, 2, bp, bp + bs), w,
        (cache_a_k, cache_a_v, cache_b_k, cache_b_v), rows)

    win = min(WIN_CHUNKS * CHUNK, t)
    a_k_shape = lambda b_, t_: (1, b_, t_, heads_a, 2, HEAD_DIM)
    a_v_shape = lambda b_, t_: (1, b_, t_, heads_a, 2 * HEAD_DIM)
    b_shape = lambda b_, t_: (b_, t_, kv_heads, HEAD_DIM)
    return (y_p, y_s,
            ak_p.reshape(a_k_shape(bp, t)), av_p.reshape(a_v_shape(bp, t)),
            bk_p[:, -win:].reshape(b_shape(bp, win)), bv_p[:, -win:].reshape(b_shape(bp, win)),
            ak_s.reshape(a_k_shape(bs, tn)), av_s.reshape(a_v_shape(bs, tn)),
            bk_s.reshape(b_shape(bs, tn)), bv_s.reshape(b_shape(bs, tn)))
```

```python
import functools
import math

import jax
import jax.numpy as jnp
from jax import lax
from jax.experimental import pallas as pl
from jax.experimental.pallas import tpu as pltpu

CHUNK = 64
WIN_CHUNKS = 2
ROT_DIM = 16
ROPE_THETA = 500000.0
EPS = 1e-6
HEAD_DIM = 64
N_ROUTED = 2
MOE_FF_CHUNK = 512

LANES = 128
MXU_DIM = 256
VMEM_LIMIT_BYTES = 56 * 1024 * 1024

F32 = jnp.float32
BF16 = jnp.bfloat16
NEG_INF = float("-inf")
LOG2_E = math.log2(math.e)


def _cparams(*sem):
    return pltpu.CompilerParams(dimension_semantics=sem, vmem_limit_bytes=VMEM_LIMIT_BYTES)


def _row_tiling(batch, seq, rows):
    if seq >= rows:
        assert seq % rows == 0
        return 1, rows
    nb = max(1, min(batch, rows // seq))
    while batch % nb:
        nb -= 1
    return nb, seq


def _const_spec(shape):
    zeros = (0,) * len(shape)
    return pl.BlockSpec(shape, lambda *_: zeros, pipeline_mode=pl.Buffered(1))


def _sigmoid(x):
    return 1.0 / (1.0 + jnp.exp(-x))


def _rms_scale(x):
    return x * lax.rsqrt(jnp.mean(x * x, axis=-1, keepdims=True) + EPS)


def _modulate(x, g, shift, scale):
    return (_rms_scale(x) * g) * (1.0 + scale) + shift


def _group_norm_rope(x, gsum_ref, gain, cos, s1, s2):
    width = x.shape[1]
    slab = MXU_DIM if width % MXU_DIM == 0 else LANES
    gsum = gsum_ref[:slab, :slab]
    pieces = []
    for j in range(width // slab):
        xs = x[:, j * slab:(j + 1) * slab]
        ss = jnp.dot((xs * xs).astype(BF16), gsum, preferred_element_type=F32)
        y = xs * lax.rsqrt(ss * (1.0 / HEAD_DIM) + EPS) * gain[:, j * slab:(j + 1) * slab]
        for t in range(slab // LANES):
            yt = y[:, t * LANES:(t + 1) * LANES]
            pieces.append(yt * cos + pltpu.roll(yt, LANES - ROT_DIM // 2, 1) * s1
                          + pltpu.roll(yt, ROT_DIM // 2, 1) * s2)
    return pieces


def _tile_rows(t, nb):
    return t if nb == 1 else jnp.concatenate([t] * nb, axis=0)


def _mod_kernel(c_ref, w_ref, b_ref, o_ref):
    c = c_ref[...]
    sc = (c * _sigmoid(c)).astype(BF16)
    o_ref[...] = jnp.dot(sc, w_ref[...].astype(BF16), preferred_element_type=F32) + b_ref[...]


def _mod_call(c, w, b):
    rows, d = c.shape
    n = w.shape[1]
    tn = 2048
    assert n % tn == 0
    return pl.pallas_call(
        _mod_kernel,
        out_shape=jax.ShapeDtypeStruct((rows, n), F32),
        grid=(n // tn,),
        in_specs=[pl.BlockSpec((rows, d), lambda j: (0, 0)),
                  pl.BlockSpec((d, tn), lambda j: (0, j)),
                  pl.BlockSpec((1, tn), lambda j: (0, j))],
        out_specs=pl.BlockSpec((rows, tn), lambda j: (0, j)),
        compiler_params=_cparams("parallel"),
        name="adaln_mod",
    )(c, w, b.reshape(1, n))


def _qkv_a_kernel(x_ref, sh_ref, sc_ref, g_ref, w_ref, gsum_ref, qg_ref, kg_ref,
                  cos_ref, s1_ref, s2_ref, qh_ref, kh_ref, vh_ref, ak_ref, av_ref,
                  *, nb, tb, heads):
    d = x_ref.shape[-1]
    nqk = heads * 2 * HEAD_DIM
    qk_scale = HEAD_DIM ** -0.5 * LOG2_E
    parts = 4 if (nb == 1 and tb % (4 * LANES) == 0) else 1
    tr = tb // parts
    for p in range(parts):
        rows = slice(p * tr, (p + 1) * tr)
        x = x_ref[:, rows, :]
        u = _modulate(x, g_ref[...], sh_ref[...], sc_ref[...]).reshape(nb * tr, d).astype(BF16)
        qkv = jnp.dot(u, w_ref[...], preferred_element_type=F32)
        cos, s1, s2 = (_tile_rows(r[rows, :], nb) for r in (cos_ref, s1_ref, s2_ref))
        q_p = _group_norm_rope(qkv[:, :nqk], gsum_ref, qg_ref[...], cos, s1, s2)
        k_p = _group_norm_rope(qkv[:, nqk:2 * nqk], gsum_ref, kg_ref[...], cos, s1, s2)
        for h in range(heads):
            cols = slice(h * LANES, (h + 1) * LANES)
            v_h = qkv[:, 2 * nqk + h * LANES:2 * nqk + (h + 1) * LANES].reshape(nb, tr, LANES)
            k_h = k_p[h].reshape(nb, tr, LANES)
            qh_ref[:, h, rows, :] = (q_p[h] * qk_scale).reshape(nb, tr, LANES).astype(BF16)
            kh_ref[:, h, rows, :] = k_h.astype(BF16)
            vh_ref[:, h, rows, :] = v_h.astype(BF16)
            ak_ref[:, rows, cols] = k_h
            av_ref[:, rows, cols] = v_h


def _qkv_a_call(x, shift, scale, g, w, gsum, qg, kg, tables, heads, rows):
    b, t, d = x.shape
    nb, tb = _row_tiling(b, t, rows)
    nt = t // tb
    n = w.shape[1]
    xmap = lambda i: (i // nt, i % nt, 0)
    mmap = lambda i: (i // nt, 0, 0)
    tmap = lambda i: (i % nt, 0)
    hmap = lambda i: (i // nt, 0, i % nt, 0)
    head_shape = jax.ShapeDtypeStruct((b, heads, t, LANES), BF16)
    flat_shape = jax.ShapeDtypeStruct((b, t, heads * LANES), F32)
    return pl.pallas_call(
        functools.partial(_qkv_a_kernel, nb=nb, tb=tb, heads=heads),
        out_shape=(head_shape, head_shape, head_shape, flat_shape, flat_shape),
        grid=(b // nb * nt,),
        in_specs=[pl.BlockSpec((nb, tb, d), xmap),
                  pl.BlockSpec((nb, 1, d), mmap),
                  pl.BlockSpec((nb, 1, d), mmap),
                  _const_spec((1, d)),
                  _const_spec((d, n)),
                  _const_spec((MXU_DIM, MXU_DIM)),
                  _const_spec((1, heads * LANES)),
                  _const_spec((1, heads * LANES)),
                  pl.BlockSpec((tb, LANES), tmap),
                  pl.BlockSpec((tb, LANES), tmap),
                  pl.BlockSpec((tb, LANES), tmap)],
        out_specs=(pl.BlockSpec((nb, heads, tb, LANES), hmap),) * 3
                  + (pl.BlockSpec((nb, tb, heads * LANES), xmap),) * 2,
        compiler_params=_cparams("parallel"),
        name="mixer_a_qkv",
    )(x, shift, scale, g, w, gsum, qg, kg, *tables)


def _split_halves(q):
    lane = lax.broadcasted_iota(jnp.int32, q.shape, 1)
    zero = jnp.zeros_like(q)
    return jnp.concatenate([jnp.where(lane < HEAD_DIM, q, zero),
                            jnp.where(lane >= HEAD_DIM, q, zero)], axis=0)


def _diff_lambda(lam_ref, lam_init):
    lv = lam_ref[...]
    t1 = jnp.sum(lv[0:1] * lv[1:2], axis=1, keepdims=True)
    t2 = jnp.sum(lv[2:3] * lv[3:4], axis=1, keepdims=True)
    return jnp.exp(t1) - jnp.exp(t2) + lam_init


def _diff_finish(o0, o1, lam_ref, g_ref, lam_init):
    o = o0 - _diff_lambda(lam_ref, lam_init) * o1
    return _rms_scale(o) * g_ref[...] * (1.0 - lam_init)


def _diff_attn_kernel(q_ref, k_ref, v_ref, lam_ref, g_ref, o_ref, s_ref, m_ref, l_ref, acc_ref,
                      *, tq, tk, hp, lam_init):
    qi = pl.program_id(2)
    n_full = (qi * tq) // tk
    qqs = [_split_halves(q_ref[0, h]) for h in range(hp)]
    m_ref[...] = jnp.full(m_ref.shape, NEG_INF, F32)
    l_ref[...] = jnp.zeros(l_ref.shape, F32)
    acc_ref[...] = jnp.zeros(acc_ref.shape, F32)

    def scores(h, j):
        kj = k_ref[0, h, pl.ds(pl.multiple_of(j * tk, tk), tk), :]
        return lax.dot_general(qqs[h], kj, (((1,), (1,)), ((), ())), preferred_element_type=F32)

    def absorb(h, j, s):
        vj = v_ref[0, h, pl.ds(pl.multiple_of(j * tk, tk), tk), :]
        m_prev = m_ref[h]
        m_new = jnp.maximum(m_prev, jnp.max(s, axis=1, keepdims=True))
        alpha = jnp.exp2(m_prev - m_new)
        chunks = [jnp.exp2(s[:, c * LANES:(c + 1) * LANES] - m_new) for c in range(tk // LANES)]
        l_ref[h] = alpha * l_ref[h] + functools.reduce(lambda a, b: a + b, chunks)
        p = jnp.concatenate(chunks, axis=1).astype(BF16)
        acc_ref[h] = alpha * acc_ref[h] + jnp.dot(p, vj, preferred_element_type=F32)
        m_ref[h] = m_new

    for h in range(hp):
        s_ref[h] = scores(h, 0)

    def body(j, carry):
        for h in range(hp):
            s = s_ref[h]
            s_ref[h] = scores(h, j + 1)
            absorb(h, j, s)
        return carry

    lax.fori_loop(0, n_full, body, 0)
    q_chunk = (qi * tq + lax.broadcasted_iota(jnp.int32, (2 * tq, tk), 0) % tq) // CHUNK
    k_chunk = (n_full * tk + lax.broadcasted_iota(jnp.int32, (2 * tq, tk), 1)) // CHUNK
    bias = jnp.where(k_chunk <= q_chunk, 0.0, NEG_INF)
    for h in range(hp):
        absorb(h, n_full, s_ref[h] + bias)

    for h in range(hp):
        o = acc_ref[h] / jnp.sum(l_ref[h], axis=1, keepdims=True)
        o_ref[0, :, h * LANES:(h + 1) * LANES] = _diff_finish(
            o[:tq], o[tq:], lam_ref, g_ref, lam_init).astype(BF16)


def _diff_attn_call(qh, kh, vh, lam_vec, subln, lam_init, tq, tk, hp):
    b, heads, t, _ = qh.shape
    assert t % tk == 0 and tk % tq == 0 and tq % CHUNK == 0 and heads % hp == 0
    return pl.pallas_call(
        functools.partial(_diff_attn_kernel, tq=tq, tk=tk, hp=hp, lam_init=lam_init),
        out_shape=jax.ShapeDtypeStruct((b, t, heads * LANES), BF16),
        grid=(b, heads // hp, t // tq),
        in_specs=[pl.BlockSpec((1, hp, tq, LANES), lambda bi, h, i: (bi, h, i, 0)),
                  pl.BlockSpec((1, hp, t, LANES), lambda bi, h, i: (bi, h, 0, 0), pipeline_mode=pl.Buffered(1)),
                  pl.BlockSpec((1, hp, t, LANES), lambda bi, h, i: (bi, h, 0, 0), pipeline_mode=pl.Buffered(1)),
                  pl.BlockSpec(lam_vec.shape, lambda bi, h, i: (0, 0)),
                  pl.BlockSpec((1, LANES), lambda bi, h, i: (0, 0))],
        out_specs=pl.BlockSpec((1, tq, hp * LANES), lambda bi, h, i: (bi, i, h)),
        scratch_shapes=[pltpu.VMEM((hp, 2 * tq, tk), F32)] + [pltpu.VMEM((hp, 2 * tq, LANES), F32)] * 3,
        compiler_params=_cparams("parallel", "parallel", "arbitrary"),
        name="mixer_a_attention",
    )(qh, kh, vh, lam_vec, subln)


def _pad_rows(a, rows):
    return jnp.concatenate([a, jnp.zeros((rows - a.shape[0], a.shape[1]), a.dtype)], axis=0)


def _diff_decode_kernel(q_ref, kp_ref, vp_ref, kn_ref, vn_ref, lam_ref, g_ref, o_ref,
                        *, past, lam_init):
    tn = q_ref.shape[2]
    qq = _split_halves(q_ref[0, 0])
    kp = kp_ref[0].astype(BF16)
    vp = vp_ref[0].astype(BF16)
    kn = _pad_rows(kn_ref[0, 0], LANES)
    vn = _pad_rows(vn_ref[0, 0], LANES)
    dims = (((1,), (1,)), ((), ()))
    s_p = lax.dot_general(qq, kp, dims, preferred_element_type=F32)
    s_n = lax.dot_general(qq, kn, dims, preferred_element_type=F32)

    def visible(shape, k_off, k_count):
        q_pos = past + lax.broadcasted_iota(jnp.int32, shape, 0) % tn
        col = lax.broadcasted_iota(jnp.int32, shape, 1)
        return ((k_off + col) // CHUNK <= q_pos // CHUNK) & (col < k_count)

    s_p = jnp.where(visible(s_p.shape, 0, past), s_p, NEG_INF)
    s_n = jnp.where(visible(s_n.shape, past, tn), s_n, NEG_INF)
    m = jnp.maximum(jnp.max(s_p, axis=1, keepdims=True), jnp.max(s_n, axis=1, keepdims=True))
    e_p = jnp.exp2(s_p - m)
    e_n = jnp.exp2(s_n - m)
    l = jnp.sum(e_p, axis=1, keepdims=True) + jnp.sum(e_n, axis=1, keepdims=True)
    acc = (jnp.dot(e_p.astype(BF16), vp, preferred_element_type=F32)
           + jnp.dot(e_n.astype(BF16), vn, preferred_element_type=F32))
    o = acc / l
    o_ref[0] = _diff_finish(o[:tn], o[tn:], lam_ref, g_ref, lam_init).astype(BF16)


def _diff_decode_call(qh, kh, vh, past_k, past_v, lam_vec, subln, lam_init):
    b, heads, tn, _ = qh.shape
    past = past_k.shape[1]
    new_spec = pl.BlockSpec((1, 1, tn, LANES), lambda bi, h: (bi, h, 0, 0))
    past_spec = pl.BlockSpec((1, past, LANES), lambda bi, h: (bi, 0, h))
    return pl.pallas_call(
        functools.partial(_diff_decode_kernel, past=past, lam_init=lam_init),
        out_shape=jax.ShapeDtypeStruct((b, tn, heads * LANES), BF16),
        grid=(b, heads),
        in_specs=[new_spec, past_spec, past_spec, new_spec, new_spec,
                  pl.BlockSpec(lam_vec.shape, lambda bi, h: (0, 0)),
                  pl.BlockSpec((1, LANES), lambda bi, h: (0, 0))],
        out_specs=pl.BlockSpec((1, tn, LANES), lambda bi, h: (bi, 0, h)),
        compiler_params=_cparams("parallel", "parallel"),
        name="mixer_a_decode_attention",
    )(qh, past_k, past_v, kh, vh, lam_vec, subln)


def _ffn_kernel(o_ref, x_ref, ogate_ref, wo_ref, sh_ref, sc_ref, gate_ref, g_ref, wgu_ref, wd_ref,
                out_ref, *, nb, tb, ff, fc):
    d = x_ref.shape[-1]
    y = jnp.dot(o_ref[...].reshape(nb * tb, o_ref.shape[-1]), wo_ref[...],
                preferred_element_type=F32)
    h = x_ref[...] + ogate_ref[...] * y.reshape(nb, tb, d)
    u = _modulate(h, g_ref[...], sh_ref[...], sc_ref[...]).reshape(nb * tb, d).astype(BF16)
    acc = jnp.zeros((nb * tb, d), F32)
    for c in range(ff // fc):
        g = jnp.dot(u, wgu_ref[:, c * fc:(c + 1) * fc], preferred_element_type=F32)
        up = jnp.dot(u, wgu_ref[:, ff + c * fc:ff + (c + 1) * fc], preferred_element_type=F32)
        a = ((g * _sigmoid(g)) * up).astype(BF16)
        acc = acc + jnp.dot(a, wd_ref[c * fc:(c + 1) * fc, :], preferred_element_type=F32)
    out_ref[...] = h + gate_ref[...] * acc.reshape(nb, tb, d)


def _ffn_call(o, x, o_gate, w_o, shift, scale, gate, g, w_gu, w_d, rows):
    b, t, d = x.shape
    k = o.shape[-1]
    ff = w_d.shape[0]
    fc = ff // 2 if (ff // 2) % LANES == 0 else ff
    nb, tb = _row_tiling(b, t, rows)
    nt = t // tb
    xmap = lambda i: (i // nt, i % nt, 0)
    mspec = pl.BlockSpec((nb, 1, d), lambda i: (i // nt, 0, 0))
    return pl.pallas_call(
        functools.partial(_ffn_kernel, nb=nb, tb=tb, ff=ff, fc=fc),
        out_shape=jax.ShapeDtypeStruct(x.shape, F32),
        grid=(b // nb * nt,),
        in_specs=[pl.BlockSpec((nb, tb, k), xmap),
                  pl.BlockSpec((nb, tb, d), xmap),
                  mspec,
                  _const_spec((k, d)),
                  mspec, mspec, mspec,
                  _const_spec((1, d)),
                  _const_spec((d, 2 * ff)),
                  _const_spec((ff, d))],
        out_specs=pl.BlockSpec((nb, tb, d), xmap),
        compiler_params=_cparams("parallel"),
        name="out_proj_dense_swiglu",
    )(o, x, o_gate, w_o, shift, scale, gate, g, w_gu, w_d)


def _qkv_b_kernel(x_ref, shq_ref, scq_ref, gq_ref, shkv_ref, sckv_ref, gkv_ref, wq_ref, wkv_ref,
                  gsum_ref, qn_ref, kn_ref, cos_ref, s1_ref, s2_ref, q_ref, k_ref, v_ref,
                  *, nb, tb):
    d = x_ref.shape[-1]
    qk_scale = HEAD_DIM ** -0.5 * LOG2_E
    parts = 1
    tr = tb // parts
    for p in range(parts):
        rows = slice(p * tr, (p + 1) * tr)
        y = _rms_scale(x_ref[:, rows, :])
        uq = ((y * gq_ref[...]) * (1.0 + scq_ref[...]) + shq_ref[...]).reshape(nb * tr, d)
        ukv = ((y * gkv_ref[...]) * (1.0 + sckv_ref[...]) + shkv_ref[...]).reshape(nb * tr, d)
        q = jnp.dot(uq.astype(BF16), wq_ref[...], preferred_element_type=F32)
        kv = jnp.dot(ukv.astype(BF16), wkv_ref[...], preferred_element_type=F32)
        cos, s1, s2 = (_tile_rows(r[rows, :], nb) for r in (cos_ref, s1_ref, s2_ref))
        q_p = _group_norm_rope(q, gsum_ref, qn_ref[...], cos, s1, s2)
        k_p = _group_norm_rope(kv[:, :LANES], gsum_ref, kn_ref[...], cos, s1, s2)
        for j, piece in enumerate(q_p):
            q_ref[:, rows, j * LANES:(j + 1) * LANES] = (
                piece * qk_scale).reshape(nb, tr, LANES).astype(BF16)
        k_ref[:, rows, :] = k_p[0].reshape(nb, tr, LANES)
        v_ref[:, rows, :] = kv[:, LANES:].reshape(nb, tr, LANES)


def _qkv_b_call(x, mod_q, mod_kv, wq, wkv, gsum, qn, kn, tables, rows):
    b, t, d = x.shape
    nq = wq.shape[1]
    assert wkv.shape[1] == 2 * LANES
    nb, tb = _row_tiling(b, t, rows)
    nt = t // tb
    xmap = lambda i: (i // nt, i % nt, 0)
    mmap = lambda i: (i // nt, 0, 0)
    tmap = lambda i: (i % nt, 0)
    mspec = pl.BlockSpec((nb, 1, d), mmap)
    return pl.pallas_call(
        functools.partial(_qkv_b_kernel, nb=nb, tb=tb),
        out_shape=(jax.ShapeDtypeStruct((b, t, nq), BF16),
                   jax.ShapeDtypeStruct((b, t, LANES), F32),
                   jax.ShapeDtypeStruct((b, t, LANES), F32)),
        grid=(b // nb * nt,),
        in_specs=[pl.BlockSpec((nb, tb, d), xmap),
                  mspec, mspec, _const_spec((1, d)),
                  mspec, mspec, _const_spec((1, d)),
                  _const_spec((d, nq)),
                  _const_spec((d, 2 * LANES)),
                  _const_spec((MXU_DIM, MXU_DIM)),
                  _const_spec((1, nq)),
                  _const_spec((1, LANES)),
                  pl.BlockSpec((tb, LANES), tmap),
                  pl.BlockSpec((tb, LANES), tmap),
                  pl.BlockSpec((tb, LANES), tmap)],
        out_specs=(pl.BlockSpec((nb, tb, nq), xmap),
                   pl.BlockSpec((nb, tb, LANES), xmap),
                   pl.BlockSpec((nb, tb, LANES), xmap)),
        compiler_params=_cparams("parallel"),
        name="mixer_b_qkv",
    )(x, mod_q[0], mod_q[1], mod_q[2], mod_kv[0], mod_kv[1], mod_kv[2], wq, wkv, gsum, qn, kn,
      *tables)


def _tile_kv_head(a, n, groups):
    lane = lax.broadcasted_iota(jnp.int32, a.shape, 1)
    swapped = pltpu.roll(a, HEAD_DIM, 1)
    keep = (lane < HEAD_DIM) if n == 0 else (lane >= HEAD_DIM)
    pair = jnp.where(keep, a, swapped).astype(BF16)
    return jnp.concatenate([pair] * (groups // 2), axis=1)


def _sink_group_attention(qs, kt, vt, mask, sink_col, groups):
    rows, width = qs.shape
    head_of_lane = lax.broadcasted_iota(jnp.int32, (rows, width), 1) // HEAD_DIM
    zero = jnp.zeros_like(qs)
    lhs = jnp.concatenate([jnp.where(head_of_lane == h, qs, zero) for h in range(groups)], axis=0)
    s = lax.dot_general(lhs, kt, (((1,), (1,)), ((), ())), preferred_element_type=F32)
    s = jnp.where(mask, s, NEG_INF)
    sink = sink_col * LOG2_E
    m = jnp.maximum(jnp.max(s, axis=1, keepdims=True), sink)
    e = jnp.exp2(s - m)
    den = jnp.sum(e, axis=1, keepdims=True) + jnp.exp2(sink - m)
    ob = jnp.dot((e / den).astype(BF16), vt, preferred_element_type=F32)
    out = jnp.zeros((rows, width), F32)
    for h in range(groups):
        out = out + jnp.where(head_of_lane == h, ob[h * rows:(h + 1) * rows], 0.0)
    return out


def _window_attn_kernel(q_ref, kprev_ref, kcur_ref, vprev_ref, vcur_ref, sink_ref, o_ref,
                        *, tq, groups, kv_heads, stack):
    q = q_ref[0]
    kc = jnp.concatenate([kprev_ref[0], kcur_ref[0]], axis=0)
    vc = jnp.concatenate([vprev_ref[0], vcur_ref[0]], axis=0)
    win = kprev_ref.shape[1]
    keys = win + tq
    first_chunk = pl.program_id(1) * (tq // CHUNK)
    q_chunk = lax.broadcasted_iota(jnp.int32, (tq, keys), 0) // CHUNK
    k_chunk = lax.broadcasted_iota(jnp.int32, (tq, keys), 1) // CHUNK - win // CHUNK
    visible = ((k_chunk <= q_chunk) & (k_chunk >= q_chunk - WIN_CHUNKS)
               & (k_chunk + first_chunk >= 0))
    bias = jnp.where(visible, 0.0, NEG_INF)
    lane = lax.broadcasted_iota(jnp.int32, (tq, LANES), 1)
    in_half = (lane < HEAD_DIM, lane >= HEAD_DIM)
    zero = jnp.zeros((tq, LANES), BF16)
    nh = 2 * stack
    bias_n = jnp.concatenate([bias] * nh, axis=0)
    head_of_row = lax.broadcasted_iota(jnp.int32, (nh * tq, 1), 0) // tq
    pairs = groups // 2
    for n in range(kv_heads):
        kt = _tile_kv_head(kc, n, 2)
        vt = _tile_kv_head(vc, n, 2)
        for j0 in range(0, pairs, stack):
            blk0 = n * pairs + j0
            qm = jnp.concatenate(
                [jnp.where(in_half[i % 2], q[:, (blk0 + i // 2) * LANES:(blk0 + i // 2 + 1) * LANES], zero)
                 for i in range(nh)], axis=0)
            sink = jnp.zeros((nh * tq, 1), F32)
            for i in range(nh):
                sink = jnp.where(head_of_row == i, sink_ref[2 * blk0 + i] * LOG2_E, sink)
            s = lax.dot_general(qm, kt, (((1,), (1,)), ((), ())), preferred_element_type=F32)
            s = s + bias_n
            m = jnp.maximum(jnp.max(s, axis=1, keepdims=True), sink)
            e = jnp.exp2(s - m)
            den = jnp.sum(e, axis=1, keepdims=True) + jnp.exp2(sink - m)
            o = jnp.dot(e.astype(BF16), vt, preferred_element_type=F32) / den
            for i in range(stack):
                o_ref[0, :, (blk0 + i) * LANES:(blk0 + i + 1) * LANES] = jnp.where(
                    in_half[0], o[2 * i * tq:(2 * i + 1) * tq],
                    o[(2 * i + 1) * tq:(2 * i + 2) * tq]).astype(BF16)


def _window_attn_call(q, k, v, sinks, kv_heads, tq):
    b, t, nq = q.shape
    groups = nq // (kv_heads * HEAD_DIM)
    win = WIN_CHUNKS * CHUNK
    assert t % tq == 0 and tq % win == 0 and groups % 2 == 0 and win == LANES
    ratio = tq // win
    cur_spec = pl.BlockSpec((1, tq, LANES), lambda bi, i: (bi, i, 0))
    prev_spec = pl.BlockSpec((1, win, LANES), lambda bi, i: (bi, jnp.maximum(i * ratio - 1, 0), 0))
    return pl.pallas_call(
        functools.partial(_window_attn_kernel, tq=tq, groups=groups, kv_heads=kv_heads, stack=4),
        out_shape=jax.ShapeDtypeStruct((b, t, nq), BF16),
        grid=(b, t // tq),
        in_specs=[pl.BlockSpec((1, tq, nq), lambda bi, i: (bi, i, 0)),
                  prev_spec, cur_spec, prev_spec, cur_spec,
                  pl.BlockSpec(memory_space=pltpu.SMEM)],
        out_specs=pl.BlockSpec((1, tq, nq), lambda bi, i: (bi, i, 0)),
        compiler_params=_cparams("parallel", "parallel"),
        name="mixer_b_window_attention",
    )(q, k, k, v, v, sinks)


def _window_decode_kernel(q_ref, kp_ref, kn_ref, vp_ref, vn_ref, sink_ref, o_ref,
                          *, past_len, groups, kv_heads):
    q = q_ref[0]
    tn = q.shape[0]
    win = kp_ref.shape[1]
    keys = 2 * LANES
    kc = _pad_rows(jnp.concatenate([kp_ref[0], kn_ref[0]], axis=0), keys)
    vc = _pad_rows(jnp.concatenate([vp_ref[0], vn_ref[0]], axis=0), keys)
    width = groups * HEAD_DIM
    rows = groups * tn
    shape = (rows, keys)
    col = lax.broadcasted_iota(jnp.int32, shape, 1)
    q_chunk = (past_len + lax.broadcasted_iota(jnp.int32, shape, 0) % tn) // CHUNK
    k_pos = past_len - win + col
    k_chunk = k_pos // CHUNK
    mask = ((k_chunk <= q_chunk) & (k_chunk >= q_chunk - WIN_CHUNKS) & (k_pos >= 0)
            & (col < win + tn))
    outs = []
    for n in range(kv_heads):
        outs.append(_sink_group_attention(q[:, n * width:(n + 1) * width],
                                          _tile_kv_head(kc, n, groups), _tile_kv_head(vc, n, groups),
                                          mask, sink_ref[n], groups))
    o_ref[0] = jnp.concatenate(outs, axis=1).astype(BF16)


def _window_decode_call(q, k_new, v_new, past_k, past_v, sink_cols, past_len):
    b, tn, nq = q.shape
    win = past_k.shape[1]
    kv_heads = sink_cols.shape[0]
    groups = nq // (kv_heads * HEAD_DIM)
    assert win + tn <= 2 * LANES
    new_spec = pl.BlockSpec((1, tn, LANES), lambda bi: (bi, 0, 0))
    past_spec = pl.BlockSpec((1, win, LANES), lambda bi: (bi, 0, 0))
    return pl.pallas_call(
        functools.partial(_window_decode_kernel, past_len=past_len, groups=groups, kv_heads=kv_heads),
        out_shape=jax.ShapeDtypeStruct((b, tn, nq), BF16),
        grid=(b,),
        in_specs=[pl.BlockSpec((1, tn, nq), lambda bi: (bi, 0, 0)),
                  past_spec, new_spec, past_spec, new_spec,
                  pl.BlockSpec(sink_cols.shape, lambda bi: (0, 0, 0))],
        out_specs=pl.BlockSpec((1, tn, nq), lambda bi: (bi, 0, 0)),
        compiler_params=_cparams("parallel"),
        name="mixer_b_decode_attention",
    )(q, past_k, k_new, past_v, v_new, sink_cols)


def _route_top2(logits, experts):
    lane = lax.broadcasted_iota(jnp.int32, logits.shape, 1).astype(F32)
    big = float(LANES)
    lg = jnp.where(lane < experts, logits, NEG_INF)
    m1 = jnp.max(lg, axis=1, keepdims=True)
    i1 = jnp.min(jnp.where(lg == m1, lane, big), axis=1, keepdims=True)
    lg2 = jnp.where(lane == i1, NEG_INF, lg)
    m2 = jnp.max(lg2, axis=1, keepdims=True)
    i2 = jnp.min(jnp.where(lg2 == m2, lane, big), axis=1, keepdims=True)
    e2 = jnp.exp(m2 - m1)
    den = 1.0 + e2
    return jnp.where(lane == i1, 1.0 / den, 0.0) + jnp.where(lane == i2, e2 / den, 0.0)


def _moe_kernel(o_ref, x_ref, ogate_ref, wo_ref, sh_ref, sc_ref, gate_ref, g_ref, rhi_ref, rlo_ref,
                wg_ref, wu_ref, wd_ref, out_ref, u_s, rnk_s, gat_s, rnkt_s, cnt_s, xg_s, ye_s,
                *, nb, tb, experts):
    e = pl.program_id(1)
    k = pl.program_id(2)
    last_k = pl.num_programs(2) - 1
    d = x_ref.shape[-1]
    tm = nb * tb
    r2 = min(2 * MXU_DIM, tm)
    sizes = [r2] + [r for r in (r2 // 2, r2 // 4) if r >= LANES]
    kc = min(2 * MXU_DIM, tm)
    rc = min(MXU_DIM, tm)
    assert nb == 1 or rc == tm

    @pl.when((e == 0) & (k == 0))
    def _route():
        row_i = lax.broadcasted_iota(jnp.int32, (rc, rc), 0)
        col_i = lax.broadcasted_iota(jnp.int32, (rc, rc), 1)
        below = jnp.where(col_i < row_i, 1.0, 0.0).astype(BF16)

        def chunk(r, seen):
            rows = pl.ds(pl.multiple_of(r * rc, rc), rc)
            ox, xx = (o_ref[:, rows, :], x_ref[:, rows, :]) if nb == 1 else (o_ref[...], x_ref[...])
            y = jnp.dot(ox.reshape(rc, ox.shape[-1]), wo_ref[...], preferred_element_type=F32)
            hx = xx + ogate_ref[...] * y.reshape(xx.shape)
            if nb == 1:
                out_ref[:, rows, :] = hx
            else:
                out_ref[...] = hx
            u = _modulate(hx, g_ref[...], sh_ref[...], sc_ref[...]).reshape(rc, d)
            ub = u.astype(BF16)
            ulo = (u - ub.astype(F32)).astype(BF16)
            rhi = rhi_ref[...]
            logits = (jnp.dot(ub, rhi, preferred_element_type=F32)
                      + jnp.dot(ub, rlo_ref[...], preferred_element_type=F32)
                      + jnp.dot(ulo, rhi, preferred_element_type=F32))
            dg = _route_top2(logits, experts)
            sel = dg > 0.0
            sel_f = jnp.where(sel, 1.0, 0.0)
            rank = jnp.dot(below, sel_f.astype(BF16), preferred_element_type=F32) + seen
            u_s[rows, :] = ub
            gat_s[rows, :] = dg
            rnk_s[rows, :] = jnp.where(sel, rank, -1.0)
            return seen + jnp.sum(sel_f, axis=0, keepdims=True)

        counts = lax.fori_loop(0, tm // rc, chunk, jnp.zeros((1, LANES), F32))
        for x in range(experts):
            cnt_s[x] = counts[0, x].astype(jnp.int32)
        rnkt_s[...] = rnk_s[...].T[:rnkt_s.shape[0]]

    count = cnt_s[e]
    whole = count // r2
    rest = count - whole * r2
    n_full = whole + (rest > sizes[1]).astype(jnp.int32) if len(sizes) > 1 else (count + r2 - 1) // r2

    def for_each_tile(fn):
        def full_tile(t, carry):
            fn(pl.multiple_of(t * r2, r2), r2)
            return carry

        lax.fori_loop(0, n_full, full_tile, 0)
        for i in range(1, len(sizes)):
            lo = sizes[i + 1] if i + 1 < len(sizes) else 0

            @pl.when((rest > lo) & (rest <= sizes[i]))
            def _small_tile(rows=sizes[i]):
                fn(pl.multiple_of(whole * r2, r2), rows)

    def gather_rows(base, rows):
        want = (base + lax.broadcasted_iota(jnp.int32, (rows, kc), 0)).astype(F32)
        acc = jnp.zeros((rows, d), F32)
        for c in range(tm // kc):
            pos = rnkt_s[pl.ds(e, 1), c * kc:(c + 1) * kc]
            pick = jnp.where(pos == want, 1.0, 0.0).astype(BF16)
            acc = acc + jnp.dot(pick, u_s[c * kc:(c + 1) * kc, :], preferred_element_type=F32)
        xg_s[pl.ds(base, rows), :] = acc.astype(BF16)
        ye_s[pl.ds(base, rows), :] = jnp.zeros((rows, d), F32)

    def expert_rows(base, rows):
        x = xg_s[pl.ds(base, rows), :]
        g = jnp.dot(x, wg_ref[0], preferred_element_type=F32)
        up = jnp.dot(x, wu_ref[0], preferred_element_type=F32)
        a = ((g * _sigmoid(g)) * up).astype(BF16)
        ye_s[pl.ds(base, rows), :] += jnp.dot(a, wd_ref[0], preferred_element_type=F32)

    @pl.when(k == 0)
    def _gather():
        for_each_tile(gather_rows)

    for_each_tile(expert_rows)

    @pl.when(k == last_k)
    def _scatter():
        lane = lax.broadcasted_iota(jnp.int32, (tm, LANES), 1)
        pos = jnp.sum(jnp.where(lane == e, rnk_s[...], 0.0), axis=1, keepdims=True)
        gate = jnp.sum(jnp.where(lane == e, gat_s[...], 0.0), axis=1, keepdims=True)

        def scatter_rows(base, rows):
            want = (base + lax.broadcasted_iota(jnp.int32, (tm, rows), 1)).astype(F32)
            place = jnp.where(pos == want, gate, 0.0).astype(BF16)
            y = ye_s[pl.ds(base, rows), :].astype(BF16)
            out_ref[...] += gate_ref[...] * jnp.dot(
                place, y, preferred_element_type=F32).reshape(nb, tb, d)

        for_each_tile(scatter_rows)


def _moe_call(o, x, o_gate, w_o, shift, scale, gate, g, r_hi, r_lo, w_gu, w_d, experts, rows):
    b, t, d = x.shape
    ko = o.shape[-1]
    ff = w_d.shape[1]
    fc = MOE_FF_CHUNK
    assert ff % fc == 0
    nk = ff // fc
    nb, tb = _row_tiling(b, t, rows)
    nt = t // tb
    tm = nb * tb
    xmap = lambda i, e, k: (i // nt, i % nt, 0)
    mmap = lambda i, e, k: (i // nt, 0, 0)
    mspec = pl.BlockSpec((nb, 1, d), mmap)
    sub = 8
    assert experts <= sub and tm % min(2 * MXU_DIM, tm) == 0
    return pl.pallas_call(
        functools.partial(_moe_kernel, nb=nb, tb=tb, experts=experts),
        out_shape=jax.ShapeDtypeStruct(x.shape, F32),
        grid=(b // nb * nt, experts, nk),
        in_specs=[pl.BlockSpec((nb, tb, ko), xmap, pipeline_mode=pl.Buffered(1)),
                  pl.BlockSpec((nb, tb, d), xmap, pipeline_mode=pl.Buffered(1)),
                  mspec,
                  _const_spec((ko, d)),
                  mspec, mspec, mspec,
                  _const_spec((1, d)),
                  _const_spec((d, LANES)),
                  _const_spec((d, LANES)),
                  pl.BlockSpec((1, d, fc), lambda i, e, k: (e, 0, k)),
                  pl.BlockSpec((1, d, fc), lambda i, e, k: (e, 0, k + nk)),
                  pl.BlockSpec((1, fc, d), lambda i, e, k: (e, k, 0))],
        out_specs=pl.BlockSpec((nb, tb, d), xmap, pipeline_mode=pl.Buffered(1)),
        scratch_shapes=[pltpu.VMEM((tm, d), BF16),
                        pltpu.VMEM((tm, LANES), F32),
                        pltpu.VMEM((tm, LANES), F32),
                        pltpu.VMEM((sub, tm), F32),
                        pltpu.SMEM((sub,), jnp.int32),
                        pltpu.VMEM((tm, d), BF16),
                        pltpu.VMEM((tm, d), F32)],
        compiler_params=_cparams("parallel", "arbitrary", "arbitrary"),
        name="expert_swiglu",
    )(o, x, o_gate, w_o, shift, scale, gate, g, r_hi, r_lo, w_gu, w_gu, w_d)


def _rope_tables(pos):
    half = ROT_DIM // 2
    inv = ROPE_THETA ** (-(jnp.arange(half, dtype=F32) * 2.0 / ROT_DIM))
    ang = pos.astype(F32)[:, None] * inv[None, :]
    cos, sin = jnp.cos(ang), jnp.sin(ang)
    t = pos.shape[0]
    zeros = lambda n: jnp.zeros((t, n), F32)
    c = jnp.concatenate([cos, cos, jnp.ones((t, HEAD_DIM - ROT_DIM), F32)], axis=1)
    s1 = jnp.concatenate([-sin, zeros(HEAD_DIM - half)], axis=1)
    s2 = jnp.concatenate([zeros(half), sin, zeros(HEAD_DIM - ROT_DIM)], axis=1)
    rep = LANES // HEAD_DIM
    return tuple(jnp.tile(a, (1, rep)) for a in (c, s1, s2))


def _mod_rows(mod, lo, hi):
    d = mod.shape[1]
    return mod[lo:hi].reshape(hi - lo, 1, d)


def _trunk(x, mods, kv_mods, w, past, rows):
    b, t, d = x.shape
    past_len = 0 if past is None else past[0].shape[2]
    pos = past_len + jnp.arange(t, dtype=jnp.int32)
    tables = _rope_tables(pos)
    heads_a = w["a_heads"]
    big = t >= rows

    m0 = mods[0]
    lam_init = 0.8 - 0.6 * math.exp(-0.3 * 0)
    qh, kh, vh, a_k, a_v = _qkv_a_call(x, m0[0], m0[1], w["norm_g"][0][0], w["a_w_qkv"], w["gsum"],
                                       w["a_q_gain"], w["a_k_gain"], tables, heads_a, rows)
    if past is None:
        o = _diff_attn_call(qh, kh, vh, w["a_lambda"], w["a_subln"], lam_init, 256, 512, 8)
    else:
        pk = past[0][0].reshape(b, past_len, heads_a * LANES)
        pv = past[1][0].reshape(b, past_len, heads_a * LANES)
        o = _diff_decode_call(qh, kh, vh, pk, pv, w["a_lambda"], w["a_subln"], lam_init)
    h = _ffn_call(o, x, m0[2], w["a_w_o"], m0[3], m0[4], m0[5], w["norm_g"][0][1],
                  w["f_w_gu"], w["f_w_d"], rows)

    m1 = mods[1]
    q, k_sh, v_sh = _qkv_b_call(h, (m1[0], m1[1], w["norm_g"][1][0]),
                                (kv_mods[0], kv_mods[1], w["kv_norm_g"]),
                                w["b_w_q"], w["kv_w"], w["gsum"], w["b_q_gain"], w["kv_k_gain"],
                                tables, rows)
    if past is None:
        o = _window_attn_call(q, k_sh, v_sh, w["sinks"], w["kv_heads"], 128)
    else:
        win = past[2].shape[1]
        o = _window_decode_call(q, k_sh, v_sh, past[2].reshape(b, win, LANES),
                                past[3].reshape(b, win, LANES), w["sink_cols_decode"], past_len)
    h = _moe_call(o, h, m1[2], w["b_w_o"], m1[3], m1[4], m1[5], w["norm_g"][1][1], w["r_hi"], w["r_lo"],
                  w["m_w_gu"], w["m_w_d"], w["experts"], min(2048, t) if big else rows)
    return h, a_k, a_v, k_sh, v_sh


def kernel(x_prompt, x_sample, c_prompt, c_sample, cache_a_k, cache_a_v, cache_b_k, cache_b_v,
           w_mod, b_mod, norm_g, a_w_qkv, a_q_norm, a_k_norm, a_lambda, a_subln, a_w_o,
           kv_norm_g, kv_w_mod, kv_b_mod, kv_w, kv_k_norm, b_w_q, b_q_norm, b_sinks, b_w_o,
           f_w_gu, f_w_d, m_router, m_w_gu, m_w_d):
    bp, t, d = x_prompt.shape
    bs, tn, _ = x_sample.shape
    assert w_mod.shape[0] == 2 and a_w_qkv.shape[0] == 1 and b_w_q.shape[0] == 1
    heads_a = cache_a_k.shape[3]
    kv_heads = cache_b_k.shape[2]
    heads_b = b_sinks.shape[1]
    groups = heads_b // kv_heads
    experts = m_router.shape[2]

    c_all = jnp.concatenate([c_prompt, c_sample], axis=0)
    mods = [_mod_call(c_all, w_mod[l], b_mod[l]) for l in range(2)]
    kv_mod = _mod_call(c_all, kv_w_mod, kv_b_mod)

    def split(mod, n, lo, hi):
        m = mod[lo:hi].reshape(hi - lo, n, 1, d)
        return [m[:, j] for j in range(n)]

    group = jnp.arange(MXU_DIM) // HEAD_DIM
    router = jnp.pad(m_router[0], ((0, 0), (0, LANES - experts)))
    r_hi = router.astype(BF16)

    def sink_cols(rows_per_head):
        s = b_sinks[0].astype(F32).reshape(kv_heads, groups, 1)
        return jnp.broadcast_to(s, (kv_heads, groups, rows_per_head)).reshape(
            kv_heads, groups * rows_per_head, 1)

    w = dict(
        a_heads=heads_a, experts=experts,
        norm_g=[[norm_g[l, j].reshape(1, d) for j in range(2)] for l in range(2)],
        gsum=(group[:, None] == group[None, :]).astype(BF16),
        a_w_qkv=a_w_qkv[0].astype(BF16),
        a_q_gain=jnp.tile(a_q_norm[0].reshape(1, LANES), (1, heads_a)),
        a_k_gain=jnp.tile(a_k_norm[0].reshape(1, LANES), (1, heads_a)),
        a_lambda=a_lambda[0].astype(F32),
        a_subln=a_subln[0].reshape(1, LANES),
        a_w_o=a_w_o[0].astype(BF16),
        f_w_gu=f_w_gu[0].astype(BF16), f_w_d=f_w_d[0].astype(BF16),
        kv_norm_g=kv_norm_g.reshape(1, d),
        kv_w=kv_w.astype(BF16),
        kv_k_gain=jnp.tile(kv_k_norm.reshape(1, HEAD_DIM), (1, LANES // HEAD_DIM)),
        b_w_q=b_w_q[0].astype(BF16),
        b_q_gain=jnp.tile(b_q_norm[0].reshape(1, HEAD_DIM), (1, heads_b)),
        b_w_o=b_w_o[0].astype(BF16),
        sinks=b_sinks[0].astype(F32), kv_heads=kv_heads, sink_cols_decode=sink_cols(tn),
        r_hi=r_hi, r_lo=(router - r_hi.astype(F32)).astype(BF16),
        m_w_gu=m_w_gu[0].astype(BF16), m_w_d=m_w_d[0].astype(BF16),
    )

    rows = 512
    y_p, ak_p, av_p, bk_p, bv_p = _trunk(
        x_prompt, [split(m, 6, 0, bp) for m in mods], split(kv_mod, 2, 0, bp), w, None, rows)
    y_s, ak_s, av_s, bk_s, bv_s = _trunk(
        x_sample, [split(m, 6, bp, bp + bs) for m in mods], split(kv_mod, 2, bp, bp + bs), w,
        (cache_a_k, cache_a_v, cache_b_k, cache_b_v), rows)

    win = min(WIN_CHUNKS * CHUNK, t)
    a_k_shape = lambda b_, t_: (1, b_, t_, heads_a, 2, HEAD_DIM)
    a_v_shape = lambda b_, t_: (1, b_, t_, heads_a, 2 * HEAD_DIM)
    b_shape = lambda b_, t_: (b_, t_, kv_heads, HEAD_DIM)
    return (y_p, y_s,
            ak_p.reshape(a_k_shape(bp, t)), av_p.reshape(a_v_shape(bp, t)),
            bk_p[:, -win:].reshape(b_shape(bp, win)), bv_p[:, -win:].reshape(b_shape(bp, win)),
            ak_s.reshape(a_k_shape(bs, tn)), av_s.reshape(a_v_shape(bs, tn)),
            bk_s.reshape(b_shape(bs, tn)), bv_s.reshape(b_shape(bs, tn)))
```

```python
import functools
import math

import jax
import jax.numpy as jnp
from jax import lax
from jax.experimental import pallas as pl
from jax.experimental.pallas import tpu as pltpu

CHUNK = 64
WIN_CHUNKS = 2
ROT_DIM = 16
ROPE_THETA = 500000.0
EPS = 1e-6
HEAD_DIM = 64
N_ROUTED = 2
MOE_FF_CHUNK = 512

LANES = 128
MXU_DIM = 256
VMEM_LIMIT_BYTES = 56 * 1024 * 1024

F32 = jnp.float32
BF16 = jnp.bfloat16
NEG_INF = float("-inf")
LOG2_E = math.log2(math.e)


def _cparams(*sem):
    return pltpu.CompilerParams(dimension_semantics=sem, vmem_limit_bytes=VMEM_LIMIT_BYTES)


def _row_tiling(batch, seq, rows):
    if seq >= rows:
        assert seq % rows == 0
        return 1, rows
    nb = max(1, min(batch, rows // seq))
    while batch % nb:
        nb -= 1
    return nb, seq


def _const_spec(shape):
    zeros = (0,) * len(shape)
    return pl.BlockSpec(shape, lambda *_: zeros, pipeline_mode=pl.Buffered(1))


def _sigmoid(x):
    return 1.0 / (1.0 + jnp.exp(-x))


def _rms_scale(x):
    return x * lax.rsqrt(jnp.mean(x * x, axis=-1, keepdims=True) + EPS)


def _modulate(x, g, shift, scale):
    return (_rms_scale(x) * g) * (1.0 + scale) + shift


def _group_norm_rope(x, gsum_ref, gain, cos, s1, s2):
    width = x.shape[1]
    slab = MXU_DIM if width % MXU_DIM == 0 else LANES
    gsum = gsum_ref[:slab, :slab]
    pieces = []
    for j in range(width // slab):
        xs = x[:, j * slab:(j + 1) * slab]
        ss = jnp.dot((xs * xs).astype(BF16), gsum, preferred_element_type=F32)
        y = xs * lax.rsqrt(ss * (1.0 / HEAD_DIM) + EPS) * gain[:, j * slab:(j + 1) * slab]
        for t in range(slab // LANES):
            yt = y[:, t * LANES:(t + 1) * LANES]
            pieces.append(yt * cos + pltpu.roll(yt, LANES - ROT_DIM // 2, 1) * s1
                          + pltpu.roll(yt, ROT_DIM // 2, 1) * s2)
    return pieces


def _tile_rows(t, nb):
    return t if nb == 1 else jnp.concatenate([t] * nb, axis=0)


def _mod_kernel(c_ref, w_ref, b_ref, o_ref):
    c = c_ref[...]
    sc = (c * _sigmoid(c)).astype(BF16)
    o_ref[...] = jnp.dot(sc, w_ref[...].astype(BF16), preferred_element_type=F32) + b_ref[...]


def _mod_call(c, w, b):
    rows, d = c.shape
    n = w.shape[1]
    tn = 2048
    assert n % tn == 0
    return pl.pallas_call(
        _mod_kernel,
        out_shape=jax.ShapeDtypeStruct((rows, n), F32),
        grid=(n // tn,),
        in_specs=[pl.BlockSpec((rows, d), lambda j: (0, 0)),
                  pl.BlockSpec((d, tn), lambda j: (0, j)),
                  pl.BlockSpec((1, tn), lambda j: (0, j))],
        out_specs=pl.BlockSpec((rows, tn), lambda j: (0, j)),
        compiler_params=_cparams("parallel"),
        name="adaln_mod",
    )(c, w, b.reshape(1, n))


def _qkv_a_kernel(x_ref, sh_ref, sc_ref, g_ref, w_ref, gsum_ref, qg_ref, kg_ref,
                  cos_ref, s1_ref, s2_ref, qh_ref, kh_ref, vh_ref, ak_ref, av_ref,
                  *, nb, tb, heads):
    d = x_ref.shape[-1]
    nqk = heads * 2 * HEAD_DIM
    qk_scale = HEAD_DIM ** -0.5 * LOG2_E
    parts = 4 if (nb == 1 and tb % (4 * LANES) == 0) else 1
    tr = tb // parts
    for p in range(parts):
        rows = slice(p * tr, (p + 1) * tr)
        x = x_ref[:, rows, :]
        u = _modulate(x, g_ref[...], sh_ref[...], sc_ref[...]).reshape(nb * tr, d).astype(BF16)
        qkv = jnp.dot(u, w_ref[...], preferred_element_type=F32)
        cos, s1, s2 = (_tile_rows(r[rows, :], nb) for r in (cos_ref, s1_ref, s2_ref))
        q_p = _group_norm_rope(qkv[:, :nqk], gsum_ref, qg_ref[...], cos, s1, s2)
        k_p = _group_norm_rope(qkv[:, nqk:2 * nqk], gsum_ref, kg_ref[...], cos, s1, s2)
        for h in range(heads):
            cols = slice(h * LANES, (h + 1) * LANES)
            v_h = qkv[:, 2 * nqk + h * LANES:2 * nqk + (h + 1) * LANES].reshape(nb, tr, LANES)
            k_h = k_p[h].reshape(nb, tr, LANES)
            qh_ref[:, h, rows, :] = (q_p[h] * qk_scale).reshape(nb, tr, LANES).astype(BF16)
            kh_ref[:, h, rows, :] = k_h.astype(BF16)
            vh_ref[:, h, rows, :] = v_h.astype(BF16)
            ak_ref[:, rows, cols] = k_h
            av_ref[:, rows, cols] = v_h


def _qkv_a_call(x, shift, scale, g, w, gsum, qg, kg, tables, heads, rows):
    b, t, d = x.shape
    nb, tb = _row_tiling(b, t, rows)
    nt = t // tb
    n = w.shape[1]
    xmap = lambda i: (i // nt, i % nt, 0)
    mmap = lambda i: (i // nt, 0, 0)
    tmap = lambda i: (i % nt, 0)
    hmap = lambda i: (i // nt, 0, i % nt, 0)
    head_shape = jax.ShapeDtypeStruct((b, heads, t, LANES), BF16)
    flat_shape = jax.ShapeDtypeStruct((b, t, heads * LANES), F32)
    return pl.pallas_call(
        functools.partial(_qkv_a_kernel, nb=nb, tb=tb, heads=heads),
        out_shape=(head_shape, head_shape, head_shape, flat_shape, flat_shape),
        grid=(b // nb * nt,),
        in_specs=[pl.BlockSpec((nb, tb, d), xmap),
                  pl.BlockSpec((nb, 1, d), mmap),
                  pl.BlockSpec((nb, 1, d), mmap),
                  _const_spec((1, d)),
                  _const_spec((d, n)),
                  _const_spec((MXU_DIM, MXU_DIM)),
                  _const_spec((1, heads * LANES)),
                  _const_spec((1, heads * LANES)),
                  pl.BlockSpec((tb, LANES), tmap),
                  pl.BlockSpec((tb, LANES), tmap),
                  pl.BlockSpec((tb, LANES), tmap)],
        out_specs=(pl.BlockSpec((nb, heads, tb, LANES), hmap),) * 3
                  + (pl.BlockSpec((nb, tb, heads * LANES), xmap),) * 2,
        compiler_params=_cparams("parallel"),
        name="mixer_a_qkv",
    )(x, shift, scale, g, w, gsum, qg, kg, *tables)


def _split_halves(q):
    lane = lax.broadcasted_iota(jnp.int32, q.shape, 1)
    zero = jnp.zeros_like(q)
    return jnp.concatenate([jnp.where(lane < HEAD_DIM, q, zero),
                            jnp.where(lane >= HEAD_DIM, q, zero)], axis=0)


def _diff_lambda(lam_ref, lam_init):
    lv = lam_ref[...]
    t1 = jnp.sum(lv[0:1] * lv[1:2], axis=1, keepdims=True)
    t2 = jnp.sum(lv[2:3] * lv[3:4], axis=1, keepdims=True)
    return jnp.exp(t1) - jnp.exp(t2) + lam_init


def _diff_finish(o0, o1, lam_ref, g_ref, lam_init):
    o = o0 - _diff_lambda(lam_ref, lam_init) * o1
    return _rms_scale(o) * g_ref[...] * (1.0 - lam_init)


def _diff_attn_kernel(q_ref, k_ref, v_ref, lam_ref, g_ref, o_ref, s_ref, m_ref, l_ref, acc_ref,
                      *, tq, tk, hp, lam_init):
    qi = pl.program_id(2)
    n_full = (qi * tq) // tk
    qqs = [_split_halves(q_ref[0, h]) for h in range(hp)]
    m_ref[...] = jnp.full(m_ref.shape, NEG_INF, F32)
    l_ref[...] = jnp.zeros(l_ref.shape, F32)
    acc_ref[...] = jnp.zeros(acc_ref.shape, F32)

    def scores(h, j):
        kj = k_ref[0, h, pl.ds(pl.multiple_of(j * tk, tk), tk), :]
        return lax.dot_general(qqs[h], kj, (((1,), (1,)), ((), ())), preferred_element_type=F32)

    def absorb(h, j, s):
        vj = v_ref[0, h, pl.ds(pl.multiple_of(j * tk, tk), tk), :]
        m_prev = m_ref[h]
        m_new = jnp.maximum(m_prev, jnp.max(s, axis=1, keepdims=True))
        alpha = jnp.exp2(m_prev - m_new)
        chunks = [jnp.exp2(s[:, c * LANES:(c + 1) * LANES] - m_new) for c in range(tk // LANES)]
        l_ref[h] = alpha * l_ref[h] + functools.reduce(lambda a, b: a + b, chunks)
        p = jnp.concatenate(chunks, axis=1).astype(BF16)
        acc_ref[h] = alpha * acc_ref[h] + jnp.dot(p, vj, preferred_element_type=F32)
        m_ref[h] = m_new

    for h in range(hp):
        s_ref[h] = scores(h, 0)

    def body(j, carry):
        for h in range(hp):
            s = s_ref[h]
            s_ref[h] = scores(h, j + 1)
            absorb(h, j, s)
        return carry

    lax.fori_loop(0, n_full, body, 0)
    q_chunk = (qi * tq + lax.broadcasted_iota(jnp.int32, (2 * tq, tk), 0) % tq) // CHUNK
    k_chunk = (n_full * tk + lax.broadcasted_iota(jnp.int32, (2 * tq, tk), 1)) // CHUNK
    bias = jnp.where(k_chunk <= q_chunk, 0.0, NEG_INF)
    for h in range(hp):
        absorb(h, n_full, s_ref[h] + bias)

    for h in range(hp):
        o = acc_ref[h] / jnp.sum(l_ref[h], axis=1, keepdims=True)
        o_ref[0, :, h * LANES:(h + 1) * LANES] = _diff_finish(
            o[:tq], o[tq:], lam_ref, g_ref, lam_init).astype(BF16)


def _diff_attn_call(qh, kh, vh, lam_vec, subln, lam_init, tq, tk, hp):
    b, heads, t, _ = qh.shape
    assert t % tk == 0 and tk % tq == 0 and tq % CHUNK == 0 and heads % hp == 0
    return pl.pallas_call(
        functools.partial(_diff_attn_kernel, tq=tq, tk=tk, hp=hp, lam_init=lam_init),
        out_shape=jax.ShapeDtypeStruct((b, t, heads * LANES), BF16),
        grid=(b, heads // hp, t // tq),
        in_specs=[pl.BlockSpec((1, hp, tq, LANES), lambda bi, h, i: (bi, h, i, 0)),
                  pl.BlockSpec((1, hp, t, LANES), lambda bi, h, i: (bi, h, 0, 0), pipeline_mode=pl.Buffered(1)),
                  pl.BlockSpec((1, hp, t, LANES), lambda bi, h, i: (bi, h, 0, 0), pipeline_mode=pl.Buffered(1)),
                  pl.BlockSpec(lam_vec.shape, lambda bi, h, i: (0, 0)),
                  pl.BlockSpec((1, LANES), lambda bi, h, i: (0, 0))],
        out_specs=pl.BlockSpec((1, tq, hp * LANES), lambda bi, h, i: (bi, i, h)),
        scratch_shapes=[pltpu.VMEM((hp, 2 * tq, tk), F32)] + [pltpu.VMEM((hp, 2 * tq, LANES), F32)] * 3,
        compiler_params=_cparams("parallel", "parallel", "arbitrary"),
        name="mixer_a_attention",
    )(qh, kh, vh, lam_vec, subln)


def _pad_rows(a, rows):
    return jnp.concatenate([a, jnp.zeros((rows - a.shape[0], a.shape[1]), a.dtype)], axis=0)


def _diff_decode_kernel(q_ref, kp_ref, vp_ref, kn_ref, vn_ref, lam_ref, g_ref, o_ref,
                        *, past, lam_init):
    tn = q_ref.shape[2]
    qq = _split_halves(q_ref[0, 0])
    kp = kp_ref[0].astype(BF16)
    vp = vp_ref[0].astype(BF16)
    kn = _pad_rows(kn_ref[0, 0], LANES)
    vn = _pad_rows(vn_ref[0, 0], LANES)
    dims = (((1,), (1,)), ((), ()))
    s_p = lax.dot_general(qq, kp, dims, preferred_element_type=F32)
    s_n = lax.dot_general(qq, kn, dims, preferred_element_type=F32)

    def visible(shape, k_off, k_count):
        q_pos = past + lax.broadcasted_iota(jnp.int32, shape, 0) % tn
        col = lax.broadcasted_iota(jnp.int32, shape, 1)
        return ((k_off + col) // CHUNK <= q_pos // CHUNK) & (col < k_count)

    s_p = jnp.where(visible(s_p.shape, 0, past), s_p, NEG_INF)
    s_n = jnp.where(visible(s_n.shape, past, tn), s_n, NEG_INF)
    m = jnp.maximum(jnp.max(s_p, axis=1, keepdims=True), jnp.max(s_n, axis=1, keepdims=True))
    e_p = jnp.exp2(s_p - m)
    e_n = jnp.exp2(s_n - m)
    l = jnp.sum(e_p, axis=1, keepdims=True) + jnp.sum(e_n, axis=1, keepdims=True)
    acc = (jnp.dot(e_p.astype(BF16), vp, preferred_element_type=F32)
           + jnp.dot(e_n.astype(BF16), vn, preferred_element_type=F32))
    o = acc / l
    o_ref[0] = _diff_finish(o[:tn], o[tn:], lam_ref, g_ref, lam_init).astype(BF16)


def _diff_decode_call(qh, kh, vh, past_k, past_v, lam_vec, subln, lam_init):
    b, heads, tn, _ = qh.shape
    past = past_k.shape[1]
    new_spec = pl.BlockSpec((1, 1, tn, LANES), lambda bi, h: (bi, h, 0, 0))
    past_spec = pl.BlockSpec((1, past, LANES), lambda bi, h: (bi, 0, h))
    return pl.pallas_call(
        functools.partial(_diff_decode_kernel, past=past, lam_init=lam_init),
        out_shape=jax.ShapeDtypeStruct((b, tn, heads * LANES), BF16),
        grid=(b, heads),
        in_specs=[new_spec, past_spec, past_spec, new_spec, new_spec,
                  pl.BlockSpec(lam_vec.shape, lambda bi, h: (0, 0)),
                  pl.BlockSpec((1, LANES), lambda bi, h: (0, 0))],
        out_specs=pl.BlockSpec((1, tn, LANES), lambda bi, h: (bi, 0, h)),
        compiler_params=_cparams("parallel", "parallel"),
        name="mixer_a_decode_attention",
    )(qh, past_k, past_v, kh, vh, lam_vec, subln)


def _ffn_kernel(o_ref, x_ref, ogate_ref, wo_ref, sh_ref, sc_ref, gate_ref, g_ref, wgu_ref, wd_ref,
                out_ref, *, nb, tb, ff, fc):
    d = x_ref.shape[-1]
    y = jnp.dot(o_ref[...].reshape(nb * tb, o_ref.shape[-1]), wo_ref[...],
                preferred_element_type=F32)
    h = x_ref[...] + ogate_ref[...] * y.reshape(nb, tb, d)
    u = _modulate(h, g_ref[...], sh_ref[...], sc_ref[...]).reshape(nb * tb, d).astype(BF16)
    acc = jnp.zeros((nb * tb, d), F32)
    for c in range(ff // fc):
        g = jnp.dot(u, wgu_ref[:, c * fc:(c + 1) * fc], preferred_element_type=F32)
        up = jnp.dot(u, wgu_ref[:, ff + c * fc:ff + (c + 1) * fc], preferred_element_type=F32)
        a = ((g * _sigmoid(g)) * up).astype(BF16)
        acc = acc + jnp.dot(a, wd_ref[c * fc:(c + 1) * fc, :], preferred_element_type=F32)
    out_ref[...] = h + gate_ref[...] * acc.reshape(nb, tb, d)


def _ffn_call(o, x, o_gate, w_o, shift, scale, gate, g, w_gu, w_d, rows):
    b, t, d = x.shape
    k = o.shape[-1]
    ff = w_d.shape[0]
    fc = ff // 2 if (ff // 2) % LANES == 0 else ff
    nb, tb = _row_tiling(b, t, rows)
    nt = t // tb
    xmap = lambda i: (i // nt, i % nt, 0)
    mspec = pl.BlockSpec((nb, 1, d), lambda i: (i // nt, 0, 0))
    return pl.pallas_call(
        functools.partial(_ffn_kernel, nb=nb, tb=tb, ff=ff, fc=fc),
        out_shape=jax.ShapeDtypeStruct(x.shape, F32),
        grid=(b // nb * nt,),
        in_specs=[pl.BlockSpec((nb, tb, k), xmap),
                  pl.BlockSpec((nb, tb, d), xmap),
                  mspec,
                  _const_spec((k, d)),
                  mspec, mspec, mspec,
                  _const_spec((1, d)),
                  _const_spec((d, 2 * ff)),
                  _const_spec((ff, d))],
        out_specs=pl.BlockSpec((nb, tb, d), xmap),
        compiler_params=_cparams("parallel"),
        name="out_proj_dense_swiglu",
    )(o, x, o_gate, w_o, shift, scale, gate, g, w_gu, w_d)


def _qkv_b_kernel(x_ref, shq_ref, scq_ref, gq_ref, shkv_ref, sckv_ref, gkv_ref, wq_ref, wkv_ref,
                  gsum_ref, qn_ref, kn_ref, cos_ref, s1_ref, s2_ref, q_ref, k_ref, v_ref,
                  *, nb, tb):
    d = x_ref.shape[-1]
    qk_scale = HEAD_DIM ** -0.5 * LOG2_E
    parts = 1
    tr = tb // parts
    for p in range(parts):
        rows = slice(p * tr, (p + 1) * tr)
        y = _rms_scale(x_ref[:, rows, :])
        uq = ((y * gq_ref[...]) * (1.0 + scq_ref[...]) + shq_ref[...]).reshape(nb * tr, d)
        ukv = ((y * gkv_ref[...]) * (1.0 + sckv_ref[...]) + shkv_ref[...]).reshape(nb * tr, d)
        q = jnp.dot(uq.astype(BF16), wq_ref[...], preferred_element_type=F32)
        kv = jnp.dot(ukv.astype(BF16), wkv_ref[...], preferred_element_type=F32)
        cos, s1, s2 = (_tile_rows(r[rows, :], nb) for r in (cos_ref, s1_ref, s2_ref))
        q_p = _group_norm_rope(q, gsum_ref, qn_ref[...], cos, s1, s2)
        k_p = _group_norm_rope(kv[:, :LANES], gsum_ref, kn_ref[...], cos, s1, s2)
        for j, piece in enumerate(q_p):
            q_ref[:, rows, j * LANES:(j + 1) * LANES] = (
                piece * qk_scale).reshape(nb, tr, LANES).astype(BF16)
        k_ref[:, rows, :] = k_p[0].reshape(nb, tr, LANES)
        v_ref[:, rows, :] = kv[:, LANES:].reshape(nb, tr, LANES)


def _qkv_b_call(x, mod_q, mod_kv, wq, wkv, gsum, qn, kn, tables, rows):
    b, t, d = x.shape
    nq = wq.shape[1]
    assert wkv.shape[1] == 2 * LANES
    nb, tb = _row_tiling(b, t, rows)
    nt = t // tb
    xmap = lambda i: (i // nt, i % nt, 0)
    mmap = lambda i: (i // nt, 0, 0)
    tmap = lambda i: (i % nt, 0)
    mspec = pl.BlockSpec((nb, 1, d), mmap)
    return pl.pallas_call(
        functools.partial(_qkv_b_kernel, nb=nb, tb=tb),
        out_shape=(jax.ShapeDtypeStruct((b, t, nq), BF16),
                   jax.ShapeDtypeStruct((b, t, LANES), F32),
                   jax.ShapeDtypeStruct((b, t, LANES), F32)),
        grid=(b // nb * nt,),
        in_specs=[pl.BlockSpec((nb, tb, d), xmap),
                  mspec, mspec, _const_spec((1, d)),
                  mspec, mspec, _const_spec((1, d)),
                  _const_spec((d, nq)),
                  _const_spec((d, 2 * LANES)),
                  _const_spec((MXU_DIM, MXU_DIM)),
                  _const_spec((1, nq)),
                  _const_spec((1, LANES)),
                  pl.BlockSpec((tb, LANES), tmap),
                  pl.BlockSpec((tb, LANES), tmap),
                  pl.BlockSpec((tb, LANES), tmap)],
        out_specs=(pl.BlockSpec((nb, tb, nq), xmap),
                   pl.BlockSpec((nb, tb, LANES), xmap),
                   pl.BlockSpec((nb, tb, LANES), xmap)),
        compiler_params=_cparams("parallel"),
        name="mixer_b_qkv",
    )(x, mod_q[0], mod_q[1], mod_q[2], mod_kv[0], mod_kv[1], mod_kv[2], wq, wkv, gsum, qn, kn,
      *tables)


def _tile_kv_head(a, n, groups):
    lane = lax.broadcasted_iota(jnp.int32, a.shape, 1)
    swapped = pltpu.roll(a, HEAD_DIM, 1)
    keep = (lane < HEAD_DIM) if n == 0 else (lane >= HEAD_DIM)
    pair = jnp.where(keep, a, swapped).astype(BF16)
    return jnp.concatenate([pair] * (groups // 2), axis=1)


def _sink_group_attention(qs, kt, vt, mask, sink_col, groups):
    rows, width = qs.shape
    head_of_lane = lax.broadcasted_iota(jnp.int32, (rows, width), 1) // HEAD_DIM
    zero = jnp.zeros_like(qs)
    lhs = jnp.concatenate([jnp.where(head_of_lane == h, qs, zero) for h in range(groups)], axis=0)
    s = lax.dot_general(lhs, kt, (((1,), (1,)), ((), ())), preferred_element_type=F32)
    s = jnp.where(mask, s, NEG_INF)
    sink = sink_col * LOG2_E
    m = jnp.maximum(jnp.max(s, axis=1, keepdims=True), sink)
    e = jnp.exp2(s - m)
    den = jnp.sum(e, axis=1, keepdims=True) + jnp.exp2(sink - m)
    ob = jnp.dot((e / den).astype(BF16), vt, preferred_element_type=F32)
    out = jnp.zeros((rows, width), F32)
    for h in range(groups):
        out = out + jnp.where(head_of_lane == h, ob[h * rows:(h + 1) * rows], 0.0)
    return out


def _window_attn_kernel(q_ref, kprev_ref, kcur_ref, vprev_ref, vcur_ref, sink_ref, o_ref,
                        *, tq, groups, kv_heads, stack):
    q = q_ref[0]
    kc = jnp.concatenate([kprev_ref[0], kcur_ref[0]], axis=0)
    vc = jnp.concatenate([vprev_ref[0], vcur_ref[0]], axis=0)
    win = kprev_ref.shape[1]
    keys = win + tq
    first_chunk = pl.program_id(1) * (tq // CHUNK)
    q_chunk = lax.broadcasted_iota(jnp.int32, (tq, keys), 0) // CHUNK
    k_chunk = lax.broadcasted_iota(jnp.int32, (tq, keys), 1) // CHUNK - win // CHUNK
    visible = ((k_chunk <= q_chunk) & (k_chunk >= q_chunk - WIN_CHUNKS)
               & (k_chunk + first_chunk >= 0))
    bias = jnp.where(visible, 0.0, NEG_INF)
    lane = lax.broadcasted_iota(jnp.int32, (tq, LANES), 1)
    in_half = (lane < HEAD_DIM, lane >= HEAD_DIM)
    zero = jnp.zeros((tq, LANES), BF16)
    nh = 2 * stack
    bias_n = jnp.concatenate([bias] * nh, axis=0)
    head_of_row = lax.broadcasted_iota(jnp.int32, (nh * tq, 1), 0) // tq
    pairs = groups // 2
    for n in range(kv_heads):
        kt = _tile_kv_head(kc, n, 2)
        vt = _tile_kv_head(vc, n, 2)
        for j0 in range(0, pairs, stack):
            blk0 = n * pairs + j0
            qm = jnp.concatenate(
                [jnp.where(in_half[i % 2], q[:, (blk0 + i // 2) * LANES:(blk0 + i // 2 + 1) * LANES], zero)
                 for i in range(nh)], axis=0)
            sink = jnp.zeros((nh * tq, 1), F32)
            for i in range(nh):
                sink = jnp.where(head_of_row == i, sink_ref[2 * blk0 + i] * LOG2_E, sink)
            s = lax.dot_general(qm, kt, (((1,), (1,)), ((), ())), preferred_element_type=F32)
            s = s + bias_n
            m = jnp.maximum(jnp.max(s, axis=1, keepdims=True), sink)
            e = jnp.exp2(s - m)
            den = jnp.sum(e, axis=1, keepdims=True) + jnp.exp2(sink - m)
            o = jnp.dot(e.astype(BF16), vt, preferred_element_type=F32) / den
            for i in range(stack):
                o_ref[0, :, (blk0 + i) * LANES:(blk0 + i + 1) * LANES] = jnp.where(
                    in_half[0], o[2 * i * tq:(2 * i + 1) * tq],
                    o[(2 * i + 1) * tq:(2 * i + 2) * tq]).astype(BF16)


def _window_attn_call(q, k, v, sinks, kv_heads, tq):
    b, t, nq = q.shape
    groups = nq // (kv_heads * HEAD_DIM)
    win = WIN_CHUNKS * CHUNK
    assert t % tq == 0 and tq % win == 0 and groups % 2 == 0 and win == LANES
    ratio = tq // win
    cur_spec = pl.BlockSpec((1, tq, LANES), lambda bi, i: (bi, i, 0))
    prev_spec = pl.BlockSpec((1, win, LANES), lambda bi, i: (bi, jnp.maximum(i * ratio - 1, 0), 0))
    return pl.pallas_call(
        functools.partial(_window_attn_kernel, tq=tq, groups=groups, kv_heads=kv_heads, stack=4),
        out_shape=jax.ShapeDtypeStruct((b, t, nq), BF16),
        grid=(b, t // tq),
        in_specs=[pl.BlockSpec((1, tq, nq), lambda bi, i: (bi, i, 0)),
                  prev_spec, cur_spec, prev_spec, cur_spec,
                  pl.BlockSpec(memory_space=pltpu.SMEM)],
        out_specs=pl.BlockSpec((1, tq, nq), lambda bi, i: (bi, i, 0)),
        compiler_params=_cparams("parallel", "parallel"),
        name="mixer_b_window_attention",
    )(q, k, k, v, v, sinks)


def _window_decode_kernel(q_ref, kp_ref, kn_ref, vp_ref, vn_ref, sink_ref, o_ref,
                          *, past_len, groups, kv_heads):
    q = q_ref[0]
    tn = q.shape[0]
    win = kp_ref.shape[1]
    keys = 2 * LANES
    kc = _pad_rows(jnp.concatenate([kp_ref[0], kn_ref[0]], axis=0), keys)
    vc = _pad_rows(jnp.concatenate([vp_ref[0], vn_ref[0]], axis=0), keys)
    width = groups * HEAD_DIM
    rows = groups * tn
    shape = (rows, keys)
    col = lax.broadcasted_iota(jnp.int32, shape, 1)
    q_chunk = (past_len + lax.broadcasted_iota(jnp.int32, shape, 0) % tn) // CHUNK
    k_pos = past_len - win + col
    k_chunk = k_pos // CHUNK
    mask = ((k_chunk <= q_chunk) & (k_chunk >= q_chunk - WIN_CHUNKS) & (k_pos >= 0)
            & (col < win + tn))
    outs = []
    for n in range(kv_heads):
        outs.append(_sink_group_attention(q[:, n * width:(n + 1) * width],
                                          _tile_kv_head(kc, n, groups), _tile_kv_head(vc, n, groups),
                                          mask, sink_ref[n], groups))
    o_ref[0] = jnp.concatenate(outs, axis=1).astype(BF16)


def _window_decode_call(q, k_new, v_new, past_k, past_v, sink_cols, past_len):
    b, tn, nq = q.shape
    win = past_k.shape[1]
    kv_heads = sink_cols.shape[0]
    groups = nq // (kv_heads * HEAD_DIM)
    assert win + tn <= 2 * LANES
    new_spec = pl.BlockSpec((1, tn, LANES), lambda bi: (bi, 0, 0))
    past_spec = pl.BlockSpec((1, win, LANES), lambda bi: (bi, 0, 0))
    return pl.pallas_call(
        functools.partial(_window_decode_kernel, past_len=past_len, groups=groups, kv_heads=kv_heads),
        out_shape=jax.ShapeDtypeStruct((b, tn, nq), BF16),
        grid=(b,),
        in_specs=[pl.BlockSpec((1, tn, nq), lambda bi: (bi, 0, 0)),
                  past_spec, new_spec, past_spec, new_spec,
                  pl.BlockSpec(sink_cols.shape, lambda bi: (0, 0, 0))],
        out_specs=pl.BlockSpec((1, tn, nq), lambda bi: (bi, 0, 0)),
        compiler_params=_cparams("parallel"),
        name="mixer_b_decode_attention",
    )(q, past_k, k_new, past_v, v_new, sink_cols)


def _route_top2(logits, experts):
    lane = lax.broadcasted_iota(jnp.int32, logits.shape, 1).astype(F32)
    big = float(LANES)
    lg = jnp.where(lane < experts, logits, NEG_INF)
    m1 = jnp.max(lg, axis=1, keepdims=True)
    i1 = jnp.min(jnp.where(lg == m1, lane, big), axis=1, keepdims=True)
    lg2 = jnp.where(lane == i1, NEG_INF, lg)
    m2 = jnp.max(lg2, axis=1, keepdims=True)
    i2 = jnp.min(jnp.where(lg2 == m2, lane, big), axis=1, keepdims=True)
    e2 = jnp.exp(m2 - m1)
    den = 1.0 + e2
    return jnp.where(lane == i1, 1.0 / den, 0.0) + jnp.where(lane == i2, e2 / den, 0.0)


def _moe_kernel(o_ref, x_ref, ogate_ref, wo_ref, sh_ref, sc_ref, gate_ref, g_ref, rhi_ref, rlo_ref,
                wg_ref, wu_ref, wd_ref, out_ref, u_s, rnk_s, gat_s, rnkt_s, off_s, xg_s, ye_s,
                *, nb, tb, experts, plan):
    e = pl.program_id(1)
    k = pl.program_id(2)
    last_k = pl.num_programs(2) - 1
    d = x_ref.shape[-1]
    tm = nb * tb
    tc, rc, align, sizes, gsub, ssub = plan
    r2 = sizes[0]
    n_chunks = tm // tc
    slots = rnkt_s.shape[0]
    assert nb == 1 or (rc == tm and tc == tm)

    @pl.when((e == 0) & (k == 0))
    def _route():
        row_i = lax.broadcasted_iota(jnp.int32, (rc, rc), 0)
        col_i = lax.broadcasted_iota(jnp.int32, (rc, rc), 1)
        below = jnp.where(col_i < row_i, 1.0, 0.0).astype(BF16)

        def token_chunk(c, used):
            used = jnp.floor((used + (align - 1)) * (1.0 / align)) * align
            for x in range(experts):
                off_s[c * slots + x] = used[0, x].astype(jnp.int32)
            for i in range(tc // rc):
                rows = pl.ds(pl.multiple_of(c * tc + i * rc, rc), rc)
                ox, xx = (o_ref[:, rows, :], x_ref[:, rows, :]) if nb == 1 else (o_ref[...], x_ref[...])
                y = jnp.dot(ox.reshape(rc, ox.shape[-1]), wo_ref[...], preferred_element_type=F32)
                hx = xx + ogate_ref[...] * y.reshape(xx.shape)
                if nb == 1:
                    out_ref[:, rows, :] = hx
                else:
                    out_ref[...] = hx
                u = _modulate(hx, g_ref[...], sh_ref[...], sc_ref[...]).reshape(rc, d)
                ub = u.astype(BF16)
                ulo = (u - ub.astype(F32)).astype(BF16)
                rhi = rhi_ref[...]
                logits = (jnp.dot(ub, rhi, preferred_element_type=F32)
                          + jnp.dot(ub, rlo_ref[...], preferred_element_type=F32)
                          + jnp.dot(ulo, rhi, preferred_element_type=F32))
                dg = _route_top2(logits, experts)
                sel = dg > 0.0
                sel_f = jnp.where(sel, 1.0, 0.0)
                rank = jnp.dot(below, sel_f.astype(BF16), preferred_element_type=F32) + used
                u_s[rows, :] = ub
                gat_s[rows, :] = dg
                rnk_s[rows, :] = jnp.where(sel, rank, -1.0)
                used = used + jnp.sum(sel_f, axis=0, keepdims=True)
            return used

        used = lax.fori_loop(0, n_chunks, token_chunk, jnp.zeros((1, LANES), F32))
        for x in range(experts):
            off_s[n_chunks * slots + x] = used[0, x].astype(jnp.int32)
        rnkt_s[...] = rnk_s[...].T[:slots]

    total = off_s[n_chunks * slots + e]
    whole = total // r2
    rest = total - whole * r2
    n_full = whole + (rest > sizes[1]).astype(jnp.int32) if len(sizes) > 1 else (total + r2 - 1) // r2
    cover_end = n_full * r2
    for i in range(1, len(sizes)):
        lo = sizes[i + 1] if i + 1 < len(sizes) else 0
        cover_end = cover_end + jnp.where((rest > lo) & (rest <= sizes[i]), sizes[i], 0)

    def for_each_tile(fn):
        def full_tile(t, carry):
            fn(pl.multiple_of(t * r2, align), r2)
            return carry

        lax.fori_loop(0, n_full, full_tile, 0)
        for i in range(1, len(sizes)):
            lo = sizes[i + 1] if i + 1 < len(sizes) else 0

            @pl.when((rest > lo) & (rest <= sizes[i]))
            def _small_tile(rows=sizes[i]):
                fn(pl.multiple_of(whole * r2, align), rows)

    def segment(c):
        start = off_s[c * slots + e]
        return start, off_s[(c + 1) * slots + e] - start

    def clear_rows(base, rows):
        xg_s[pl.ds(base, rows), :] = jnp.zeros((rows, d), BF16)
        ye_s[pl.ds(base, rows), :] = jnp.zeros((rows, d), F32)

    def expert_rows(base, rows):
        x = xg_s[pl.ds(base, rows), :]
        g = jnp.dot(x, wg_ref[0], preferred_element_type=F32)
        up = jnp.dot(x, wu_ref[0], preferred_element_type=F32)
        a = ((g * _sigmoid(g)) * up).astype(BF16)
        ye_s[pl.ds(base, rows), :] += jnp.dot(a, wd_ref[0], preferred_element_type=F32)

    @pl.when(k == 0)
    def _gather():
        for_each_tile(clear_rows)
        for c in range(n_chunks):
            start, length = segment(c)
            pos = rnkt_s[pl.ds(e, 1), c * tc:(c + 1) * tc]

            def sub_tile(t, carry, c=c, start=start, pos=pos):
                base = pl.multiple_of(start + t * gsub, align)
                want = (base + lax.broadcasted_iota(jnp.int32, (gsub, tc), 0)).astype(F32)
                pick = jnp.where(pos == want, 1.0, 0.0).astype(BF16)
                xg_s[pl.ds(base, gsub), :] = jnp.dot(
                    pick, u_s[c * tc:(c + 1) * tc, :], preferred_element_type=F32).astype(BF16)
                return carry

            lax.fori_loop(0, (length + gsub - 1) // gsub, sub_tile, 0)

    for_each_tile(expert_rows)

    @pl.when(k == last_k)
    def _scatter():
        lane = lax.broadcasted_iota(jnp.int32, (tm, LANES), 1)
        pos_all = jnp.sum(jnp.where(lane == e, rnk_s[...], 0.0), axis=1, keepdims=True)
        gate_all = jnp.sum(jnp.where(lane == e, gat_s[...], 0.0), axis=1, keepdims=True)
        for c in range(n_chunks):
            start, length = segment(c)
            pos = pos_all[c * tc:(c + 1) * tc]
            gate = gate_all[c * tc:(c + 1) * tc]

            def sub_tile(t, carry, c=c, start=start, pos=pos, gate=gate):
                first = start + t * ssub
                base = pl.multiple_of(jnp.minimum(first, cover_end - ssub), align)
                want = base + lax.broadcasted_iota(jnp.int32, (tc, ssub), 1)
                hit = (pos == want.astype(F32)) & (want >= first)
                place = jnp.where(hit, gate, 0.0).astype(BF16)
                y = ye_s[pl.ds(base, ssub), :].astype(BF16)
                add = jnp.dot(place, y, preferred_element_type=F32)
                if nb == 1:
                    out_ref[:, c * tc:(c + 1) * tc, :] += gate_ref[...] * add.reshape(1, tc, d)
                else:
                    out_ref[...] += gate_ref[...] * add.reshape(nb, tb, d)
                return carry

            lax.fori_loop(0, (length + ssub - 1) // ssub, sub_tile, 0)


def _moe_plan(tm):
    align = 16
    tc = min(2 * MXU_DIM, tm)
    rc = min(MXU_DIM, tm)
    r2 = -(-(tc + tc // 8) // align) * align
    gsub = min(tc, tc // 4 + 2 * align)
    ssub = min(MXU_DIM, tc)
    sizes = [r2] + [r for r in (r2 // 2, r2 // 4) if r % align == 0 and r >= ssub]
    most = tm + (tm // tc) * (align - 1)
    cover = -(-most // r2) * r2
    cap = -(-max(cover, most + gsub) // align) * align
    return (tc, rc, align, sizes, gsub, ssub), cap


def _moe_call(o, x, o_gate, w_o, shift, scale, gate, g, r_hi, r_lo, w_gu, w_d, experts, rows):
    b, t, d = x.shape
    ko = o.shape[-1]
    ff = w_d.shape[1]
    fc = MOE_FF_CHUNK
    assert ff % fc == 0
    nk = ff // fc
    nb, tb = _row_tiling(b, t, rows)
    nt = t // tb
    tm = nb * tb
    xmap = lambda i, e, k: (i // nt, i % nt, 0)
    mmap = lambda i, e, k: (i // nt, 0, 0)
    mspec = pl.BlockSpec((nb, 1, d), mmap)
    sub = 8
    plan, cap = _moe_plan(tm)
    assert experts <= sub and tm % plan[0] == 0
    return pl.pallas_call(
        functools.partial(_moe_kernel, nb=nb, tb=tb, experts=experts, plan=plan),
        out_shape=jax.ShapeDtypeStruct(x.shape, F32),
        grid=(b // nb * nt, experts, nk),
        in_specs=[pl.BlockSpec((nb, tb, ko), xmap, pipeline_mode=pl.Buffered(1)),
                  pl.BlockSpec((nb, tb, d), xmap, pipeline_mode=pl.Buffered(1)),
                  mspec,
                  _const_spec((ko, d)),
                  mspec, mspec, mspec,
                  _const_spec((1, d)),
                  _const_spec((d, LANES)),
                  _const_spec((d, LANES)),
                  pl.BlockSpec((1, d, fc), lambda i, e, k: (e, 0, k)),
                  pl.BlockSpec((1, d, fc), lambda i, e, k: (e, 0, k + nk)),
                  pl.BlockSpec((1, fc, d), lambda i, e, k: (e, k, 0))],
        out_specs=pl.BlockSpec((nb, tb, d), xmap, pipeline_mode=pl.Buffered(1)),
        scratch_shapes=[pltpu.VMEM((tm, d), BF16),
                        pltpu.VMEM((tm, LANES), F32),
                        pltpu.VMEM((tm, LANES), F32),
                        pltpu.VMEM((sub, tm), F32),
                        pltpu.SMEM(((tm // plan[0] + 1) * sub,), jnp.int32),
                        pltpu.VMEM((cap, d), BF16),
                        pltpu.VMEM((cap, d), F32)],
        compiler_params=_cparams("parallel", "arbitrary", "arbitrary"),
        name="expert_swiglu",
    )(o, x, o_gate, w_o, shift, scale, gate, g, r_hi, r_lo, w_gu, w_gu, w_d)


def _rope_tables(pos):
    half = ROT_DIM // 2
    inv = ROPE_THETA ** (-(jnp.arange(half, dtype=F32) * 2.0 / ROT_DIM))
    ang = pos.astype(F32)[:, None] * inv[None, :]
    cos, sin = jnp.cos(ang), jnp.sin(ang)
    t = pos.shape[0]
    zeros = lambda n: jnp.zeros((t, n), F32)
    c = jnp.concatenate([cos, cos, jnp.ones((t, HEAD_DIM - ROT_DIM), F32)], axis=1)
    s1 = jnp.concatenate([-sin, zeros(HEAD_DIM - half)], axis=1)
    s2 = jnp.concatenate([zeros(half), sin, zeros(HEAD_DIM - ROT_DIM)], axis=1)
    rep = LANES // HEAD_DIM
    return tuple(jnp.tile(a, (1, rep)) for a in (c, s1, s2))


def _mod_rows(mod, lo, hi):
    d = mod.shape[1]
    return mod[lo:hi].reshape(hi - lo, 1, d)


def _trunk(x, mods, kv_mods, w, past, rows):
    b, t, d = x.shape
    past_len = 0 if past is None else past[0].shape[2]
    pos = past_len + jnp.arange(t, dtype=jnp.int32)
    tables = _rope_tables(pos)
    heads_a = w["a_heads"]
    big = t >= rows

    m0 = mods[0]
    lam_init = 0.8 - 0.6 * math.exp(-0.3 * 0)
    qh, kh, vh, a_k, a_v = _qkv_a_call(x, m0[0], m0[1], w["norm_g"][0][0], w["a_w_qkv"], w["gsum"],
                                       w["a_q_gain"], w["a_k_gain"], tables, heads_a, rows)
    if past is None:
        o = _diff_attn_call(qh, kh, vh, w["a_lambda"], w["a_subln"], lam_init, 256, 512, 8)
    else:
        pk = past[0][0].reshape(b, past_len, heads_a * LANES)
        pv = past[1][0].reshape(b, past_len, heads_a * LANES)
        o = _diff_decode_call(qh, kh, vh, pk, pv, w["a_lambda"], w["a_subln"], lam_init)
    h = _ffn_call(o, x, m0[2], w["a_w_o"], m0[3], m0[4], m0[5], w["norm_g"][0][1],
                  w["f_w_gu"], w["f_w_d"], rows)

    m1 = mods[1]
    q, k_sh, v_sh = _qkv_b_call(h, (m1[0], m1[1], w["norm_g"][1][0]),
                                (kv_mods[0], kv_mods[1], w["kv_norm_g"]),
                                w["b_w_q"], w["kv_w"], w["gsum"], w["b_q_gain"], w["kv_k_gain"],
                                tables, rows)
    if past is None:
        o = _window_attn_call(q, k_sh, v_sh, w["sinks"], w["kv_heads"], 128)
    else:
        win = past[2].shape[1]
        o = _window_decode_call(q, k_sh, v_sh, past[2].reshape(b, win, LANES),
                                past[3].reshape(b, win, LANES), w["sink_cols_decode"], past_len)
    h = _moe_call(o, h, m1[2], w["b_w_o"], m1[3], m1[4], m1[5], w["norm_g"][1][1], w["r_hi"], w["r_lo"],
                  w["m_w_gu"], w["m_w_d"], w["experts"], min(2048, t) if big else rows)
    return h, a_k, a_v, k_sh, v_sh


def kernel(x_prompt, x_sample, c_prompt, c_sample, cache_a_k, cache_a_v, cache_b_k, cache_b_v,
           w_mod, b_mod, norm_g, a_w_qkv, a_q_norm, a_k_norm, a_lambda, a_subln, a_w_o,
           kv_norm_g, kv_w_mod, kv_b_mod, kv_w, kv_k_norm, b_w_q, b_q_norm, b_sinks, b_w_o,
           f_w_gu, f_w_d, m_router, m_w_gu, m_w_d):
    bp, t, d = x_prompt.shape
    bs, tn, _ = x_sample.shape
    assert w_mod.shape[0] == 2 and a_w_qkv.shape[0] == 1 and b_w_q.shape[0] == 1
    heads_a = cache_a_k.shape[3]
    kv_heads = cache_b_k.shape[2]
    heads_b = b_sinks.shape[1]
    groups = heads_b // kv_heads
    experts = m_router.shape[2]

    c_all = jnp.concatenate([c_prompt, c_sample], axis=0)
    mods = [_mod_call(c_all, w_mod[l], b_mod[l]) for l in range(2)]
    kv_mod = _mod_call(c_all, kv_w_mod, kv_b_mod)

    def split(mod, n, lo, hi):
        m = mod[lo:hi].reshape(hi - lo, n, 1, d)
        return [m[:, j] for j in range(n)]

    group = jnp.arange(MXU_DIM) // HEAD_DIM
    router = jnp.pad(m_router[0], ((0, 0), (0, LANES - experts)))
    r_hi = router.astype(BF16)

    def sink_cols(rows_per_head):
        s = b_sinks[0].astype(F32).reshape(kv_heads, groups, 1)
        return jnp.broadcast_to(s, (kv_heads, groups, rows_per_head)).reshape(
            kv_heads, groups * rows_per_head, 1)

    w = dict(
        a_heads=heads_a, experts=experts,
        norm_g=[[norm_g[l, j].reshape(1, d) for j in range(2)] for l in range(2)],
        gsum=(group[:, None] == group[None, :]).astype(BF16),
        a_w_qkv=a_w_qkv[0].astype(BF16),
        a_q_gain=jnp.tile(a_q_norm[0].reshape(1, LANES), (1, heads_a)),
        a_k_gain=jnp.tile(a_k_norm[0].reshape(1, LANES), (1, heads_a)),
        a_lambda=a_lambda[0].astype(F32),
        a_subln=a_subln[0].reshape(1, LANES),
        a_w_o=a_w_o[0].astype(BF16),
        f_w_gu=f_w_gu[0].astype(BF16), f_w_d=f_w_d[0].astype(BF16),
        kv_norm_g=kv_norm_g.reshape(1, d),
        kv_w=kv_w.astype(BF16),
        kv_k_gain=jnp.tile(kv_k_norm.reshape(1, HEAD_DIM), (1, LANES // HEAD_DIM)),
        b_w_q=b_w_q[0].astype(BF16),
        b_q_gain=jnp.tile(b_q_norm[0].reshape(1, HEAD_DIM), (1, heads_b)),
        b_w_o=b_w_o[0].astype(BF16),
        sinks=b_sinks[0].astype(F32), kv_heads=kv_heads, sink_cols_decode=sink_cols(tn),
        r_hi=r_hi, r_lo=(router - r_hi.astype(F32)).astype(BF16),
        m_w_gu=m_w_gu[0].astype(BF16), m_w_d=m_w_d[0].astype(BF16),
    )

    rows = 512
    y_p, ak_p, av_p, bk_p, bv_p = _trunk(
        x_prompt, [split(m, 6, 0, bp) for m in mods], split(kv_mod, 2, 0, bp), w, None, rows)
    y_s, ak_s, av_s, bk_s, bv_s = _trunk(
        x_sample, [split(m, 6, bp, bp + bs) for m in mods], split(kv_mod, 2, bp, bp + bs), w,
        (cache_a_k, cache_a_v, cache_b_k, cache_b_v), rows)

    win = min(WIN_CHUNKS * CHUNK, t)
    a_k_shape = lambda b_, t_: (1, b_, t_, heads_a, 2, HEAD_DIM)
    a_v_shape = lambda b_, t_: (1, b_, t_, heads_a, 2 * HEAD_DIM)
    b_shape = lambda b_, t_: (b_, t_, kv_heads, HEAD_DIM)
    return (y_p, y_s,
            ak_p.reshape(a_k_shape(bp, t)), av_p.reshape(a_v_shape(bp, t)),
            bk_p[:, -win:].reshape(b_shape(bp, win)), bv_p[:, -win:].reshape(b_shape(bp, win)),
            ak_s.reshape(a_k_shape(bs, tn)), av_s.reshape(a_v_shape(bs, tn)),
            bk_s.reshape(b_shape(bs, tn)), bv_s.reshape(b_shape(bs, tn)))
```

```python
import functools
import math

import jax
import jax.numpy as jnp
from jax import lax
from jax.experimental import pallas as pl
from jax.experimental.pallas import tpu as pltpu

CHUNK = 64
WIN_CHUNKS = 2
ROT_DIM = 16
ROPE_THETA = 500000.0
EPS = 1e-6
HEAD_DIM = 64
N_ROUTED = 2
MOE_FF_CHUNK = 512
WEIGHT_RING = 3

LANES = 128
MXU_DIM = 256
VMEM_LIMIT_BYTES = 56 * 1024 * 1024

F32 = jnp.float32
BF16 = jnp.bfloat16
NEG_INF = float("-inf")
LOG2_E = math.log2(math.e)


def _cparams(*sem):
    return pltpu.CompilerParams(dimension_semantics=sem, vmem_limit_bytes=VMEM_LIMIT_BYTES)


def _row_tiling(batch, seq, rows):
    if seq >= rows:
        assert seq % rows == 0
        return 1, rows
    nb = max(1, min(batch, rows // seq))
    while batch % nb:
        nb -= 1
    return nb, seq


def _const_spec(shape):
    zeros = (0,) * len(shape)
    return pl.BlockSpec(shape, lambda *_: zeros, pipeline_mode=pl.Buffered(1))


def _sigmoid(x):
    return 1.0 / (1.0 + jnp.exp(-x))


def _rms_scale(x):
    return x * lax.rsqrt(jnp.mean(x * x, axis=-1, keepdims=True) + EPS)


def _modulate(x, g, shift, scale):
    return (_rms_scale(x) * g) * (1.0 + scale) + shift


def _group_norm_rope(x, gsum_ref, gain, cos, s1, s2):
    width = x.shape[1]
    slab = MXU_DIM if width % MXU_DIM == 0 else LANES
    gsum = gsum_ref[:slab, :slab]
    pieces = []
    for j in range(width // slab):
        xs = x[:, j * slab:(j + 1) * slab]
        ss = jnp.dot((xs * xs).astype(BF16), gsum, preferred_element_type=F32)
        y = xs * lax.rsqrt(ss * (1.0 / HEAD_DIM) + EPS) * gain[:, j * slab:(j + 1) * slab]
        for t in range(slab // LANES):
            yt = y[:, t * LANES:(t + 1) * LANES]
            pieces.append(yt * cos + pltpu.roll(yt, LANES - ROT_DIM // 2, 1) * s1
                          + pltpu.roll(yt, ROT_DIM // 2, 1) * s2)
    return pieces


def _tile_rows(t, nb):
    return t if nb == 1 else jnp.concatenate([t] * nb, axis=0)


def _mod_kernel(c_ref, w_ref, b_ref, o_ref):
    c = c_ref[...]
    sc = (c * _sigmoid(c)).astype(BF16)
    o_ref[...] = jnp.dot(sc, w_ref[...].astype(BF16), preferred_element_type=F32) + b_ref[...]


def _mod_call(c, w, b):
    rows, d = c.shape
    n = w.shape[1]
    tn = 2048
    assert n % tn == 0
    return pl.pallas_call(
        _mod_kernel,
        out_shape=jax.ShapeDtypeStruct((rows, n), F32),
        grid=(n // tn,),
        in_specs=[pl.BlockSpec((rows, d), lambda j: (0, 0)),
                  pl.BlockSpec((d, tn), lambda j: (0, j)),
                  pl.BlockSpec((1, tn), lambda j: (0, j))],
        out_specs=pl.BlockSpec((rows, tn), lambda j: (0, j)),
        compiler_params=_cparams("parallel"),
        name="adaln_mod",
    )(c, w, b.reshape(1, n))


def _qkv_a_kernel(x_ref, sh_ref, sc_ref, g_ref, w_ref, gsum_ref, qg_ref, kg_ref,
                  cos_ref, s1_ref, s2_ref, qh_ref, kh_ref, vh_ref, ak_ref, av_ref,
                  *, nb, tb, heads):
    d = x_ref.shape[-1]
    nqk = heads * 2 * HEAD_DIM
    qk_scale = HEAD_DIM ** -0.5 * LOG2_E
    parts = 4 if (nb == 1 and tb % (4 * LANES) == 0) else 1
    tr = tb // parts
    for p in range(parts):
        rows = slice(p * tr, (p + 1) * tr)
        x = x_ref[:, rows, :]
        u = _modulate(x, g_ref[...], sh_ref[...], sc_ref[...]).reshape(nb * tr, d).astype(BF16)
        qkv = jnp.dot(u, w_ref[...], preferred_element_type=F32)
        cos, s1, s2 = (_tile_rows(r[rows, :], nb) for r in (cos_ref, s1_ref, s2_ref))
        q_p = _group_norm_rope(qkv[:, :nqk], gsum_ref, qg_ref[...], cos, s1, s2)
        k_p = _group_norm_rope(qkv[:, nqk:2 * nqk], gsum_ref, kg_ref[...], cos, s1, s2)
        for h in range(heads):
            cols = slice(h * LANES, (h + 1) * LANES)
            v_h = qkv[:, 2 * nqk + h * LANES:2 * nqk + (h + 1) * LANES].reshape(nb, tr, LANES)
            k_h = k_p[h].reshape(nb, tr, LANES)
            qh_ref[:, h, rows, :] = (q_p[h] * qk_scale).reshape(nb, tr, LANES).astype(BF16)
            kh_ref[:, h, rows, :] = k_h.astype(BF16)
            vh_ref[:, h, rows, :] = v_h.astype(BF16)
            ak_ref[:, rows, cols] = k_h
            av_ref[:, rows, cols] = v_h


def _qkv_a_call(x, shift, scale, g, w, gsum, qg, kg, tables, heads, rows):
    b, t, d = x.shape
    nb, tb = _row_tiling(b, t, rows)
    nt = t // tb
    n = w.shape[1]
    xmap = lambda i: (i // nt, i % nt, 0)
    mmap = lambda i: (i // nt, 0, 0)
    tmap = lambda i: (i % nt, 0)
    hmap = lambda i: (i // nt, 0, i % nt, 0)
    head_shape = jax.ShapeDtypeStruct((b, heads, t, LANES), BF16)
    flat_shape = jax.ShapeDtypeStruct((b, t, heads * LANES), F32)
    return pl.pallas_call(
        functools.partial(_qkv_a_kernel, nb=nb, tb=tb, heads=heads),
        out_shape=(head_shape, head_shape, head_shape, flat_shape, flat_shape),
        grid=(b // nb * nt,),
        in_specs=[pl.BlockSpec((nb, tb, d), xmap),
                  pl.BlockSpec((nb, 1, d), mmap),
                  pl.BlockSpec((nb, 1, d), mmap),
                  _const_spec((1, d)),
                  _const_spec((d, n)),
                  _const_spec((MXU_DIM, MXU_DIM)),
                  _const_spec((1, heads * LANES)),
                  _const_spec((1, heads * LANES)),
                  pl.BlockSpec((tb, LANES), tmap),
                  pl.BlockSpec((tb, LANES), tmap),
                  pl.BlockSpec((tb, LANES), tmap)],
        out_specs=(pl.BlockSpec((nb, heads, tb, LANES), hmap),) * 3
                  + (pl.BlockSpec((nb, tb, heads * LANES), xmap),) * 2,
        compiler_params=_cparams("parallel"),
        name="mixer_a_qkv",
    )(x, shift, scale, g, w, gsum, qg, kg, *tables)


def _split_halves(q):
    lane = lax.broadcasted_iota(jnp.int32, q.shape, 1)
    zero = jnp.zeros_like(q)
    return jnp.concatenate([jnp.where(lane < HEAD_DIM, q, zero),
                            jnp.where(lane >= HEAD_DIM, q, zero)], axis=0)


def _diff_lambda(lam_ref, lam_init):
    lv = lam_ref[...]
    t1 = jnp.sum(lv[0:1] * lv[1:2], axis=1, keepdims=True)
    t2 = jnp.sum(lv[2:3] * lv[3:4], axis=1, keepdims=True)
    return jnp.exp(t1) - jnp.exp(t2) + lam_init


def _diff_finish(o0, o1, lam_ref, g_ref, lam_init):
    o = o0 - _diff_lambda(lam_ref, lam_init) * o1
    return _rms_scale(o) * g_ref[...] * (1.0 - lam_init)


def _diff_attn_kernel(q_ref, k_ref, v_ref, lam_ref, g_ref, o_ref, s_ref, m_ref, l_ref, acc_ref,
                      *, tq, tk, hp, lam_init):
    qi = pl.program_id(2)
    n_full = (qi * tq) // tk
    qqs = [_split_halves(q_ref[0, h]) for h in range(hp)]
    m_ref[...] = jnp.full(m_ref.shape, NEG_INF, F32)
    l_ref[...] = jnp.zeros(l_ref.shape, F32)
    acc_ref[...] = jnp.zeros(acc_ref.shape, F32)

    def scores(h, j):
        kj = k_ref[0, h, pl.ds(pl.multiple_of(j * tk, tk), tk), :]
        return lax.dot_general(qqs[h], kj, (((1,), (1,)), ((), ())), preferred_element_type=F32)

    def absorb(h, j, s):
        vj = v_ref[0, h, pl.ds(pl.multiple_of(j * tk, tk), tk), :]
        m_prev = m_ref[h]
        m_new = jnp.maximum(m_prev, jnp.max(s, axis=1, keepdims=True))
        alpha = jnp.exp2(m_prev - m_new)
        chunks = [jnp.exp2(s[:, c * LANES:(c + 1) * LANES] - m_new) for c in range(tk // LANES)]
        l_ref[h] = alpha * l_ref[h] + functools.reduce(lambda a, b: a + b, chunks)
        p = jnp.concatenate(chunks, axis=1).astype(BF16)
        acc_ref[h] = alpha * acc_ref[h] + jnp.dot(p, vj, preferred_element_type=F32)
        m_ref[h] = m_new

    for h in range(hp):
        s_ref[h] = scores(h, 0)

    def body(j, carry):
        for h in range(hp):
            s = s_ref[h]
            s_ref[h] = scores(h, j + 1)
            absorb(h, j, s)
        return carry

    lax.fori_loop(0, n_full, body, 0)
    q_chunk = (qi * tq + lax.broadcasted_iota(jnp.int32, (2 * tq, tk), 0) % tq) // CHUNK
    k_chunk = (n_full * tk + lax.broadcasted_iota(jnp.int32, (2 * tq, tk), 1)) // CHUNK
    bias = jnp.where(k_chunk <= q_chunk, 0.0, NEG_INF)
    for h in range(hp):
        absorb(h, n_full, s_ref[h] + bias)

    for h in range(hp):
        o = acc_ref[h] / jnp.sum(l_ref[h], axis=1, keepdims=True)
        o_ref[0, :, h * LANES:(h + 1) * LANES] = _diff_finish(
            o[:tq], o[tq:], lam_ref, g_ref, lam_init).astype(BF16)


def _diff_attn_call(qh, kh, vh, lam_vec, subln, lam_init, tq, tk, hp):
    b, heads, t, _ = qh.shape
    assert t % tk == 0 and tk % tq == 0 and tq % CHUNK == 0 and heads % hp == 0
    return pl.pallas_call(
        functools.partial(_diff_attn_kernel, tq=tq, tk=tk, hp=hp, lam_init=lam_init),
        out_shape=jax.ShapeDtypeStruct((b, t, heads * LANES), BF16),
        grid=(b, heads // hp, t // tq),
        in_specs=[pl.BlockSpec((1, hp, tq, LANES), lambda bi, h, i: (bi, h, i, 0)),
                  pl.BlockSpec((1, hp, t, LANES), lambda bi, h, i: (bi, h, 0, 0), pipeline_mode=pl.Buffered(1)),
                  pl.BlockSpec((1, hp, t, LANES), lambda bi, h, i: (bi, h, 0, 0), pipeline_mode=pl.Buffered(1)),
                  pl.BlockSpec(lam_vec.shape, lambda bi, h, i: (0, 0)),
                  pl.BlockSpec((1, LANES), lambda bi, h, i: (0, 0))],
        out_specs=pl.BlockSpec((1, tq, hp * LANES), lambda bi, h, i: (bi, i, h)),
        scratch_shapes=[pltpu.VMEM((hp, 2 * tq, tk), F32)] + [pltpu.VMEM((hp, 2 * tq, LANES), F32)] * 3,
        compiler_params=_cparams("parallel", "parallel", "arbitrary"),
        name="mixer_a_attention",
    )(qh, kh, vh, lam_vec, subln)


def _pad_rows(a, rows):
    return jnp.concatenate([a, jnp.zeros((rows - a.shape[0], a.shape[1]), a.dtype)], axis=0)


def _diff_decode_kernel(q_ref, kp_ref, vp_ref, kn_ref, vn_ref, lam_ref, g_ref, o_ref,
                        *, past, lam_init):
    tn = q_ref.shape[2]
    qq = _split_halves(q_ref[0, 0])
    kp = kp_ref[0].astype(BF16)
    vp = vp_ref[0].astype(BF16)
    kn = _pad_rows(kn_ref[0, 0], LANES)
    vn = _pad_rows(vn_ref[0, 0], LANES)
    dims = (((1,), (1,)), ((), ()))
    s_p = lax.dot_general(qq, kp, dims, preferred_element_type=F32)
    s_n = lax.dot_general(qq, kn, dims, preferred_element_type=F32)

    def visible(shape, k_off, k_count):
        q_pos = past + lax.broadcasted_iota(jnp.int32, shape, 0) % tn
        col = lax.broadcasted_iota(jnp.int32, shape, 1)
        return ((k_off + col) // CHUNK <= q_pos // CHUNK) & (col < k_count)

    s_p = jnp.where(visible(s_p.shape, 0, past), s_p, NEG_INF)
    s_n = jnp.where(visible(s_n.shape, past, tn), s_n, NEG_INF)
    m = jnp.maximum(jnp.max(s_p, axis=1, keepdims=True), jnp.max(s_n, axis=1, keepdims=True))
    e_p = jnp.exp2(s_p - m)
    e_n = jnp.exp2(s_n - m)
    l = jnp.sum(e_p, axis=1, keepdims=True) + jnp.sum(e_n, axis=1, keepdims=True)
    acc = (jnp.dot(e_p.astype(BF16), vp, preferred_element_type=F32)
           + jnp.dot(e_n.astype(BF16), vn, preferred_element_type=F32))
    o = acc / l
    o_ref[0] = _diff_finish(o[:tn], o[tn:], lam_ref, g_ref, lam_init).astype(BF16)


def _diff_decode_call(qh, kh, vh, past_k, past_v, lam_vec, subln, lam_init):
    b, heads, tn, _ = qh.shape
    past = past_k.shape[1]
    new_spec = pl.BlockSpec((1, 1, tn, LANES), lambda bi, h: (bi, h, 0, 0))
    past_spec = pl.BlockSpec((1, past, LANES), lambda bi, h: (bi, 0, h))
    return pl.pallas_call(
        functools.partial(_diff_decode_kernel, past=past, lam_init=lam_init),
        out_shape=jax.ShapeDtypeStruct((b, tn, heads * LANES), BF16),
        grid=(b, heads),
        in_specs=[new_spec, past_spec, past_spec, new_spec, new_spec,
                  pl.BlockSpec(lam_vec.shape, lambda bi, h: (0, 0)),
                  pl.BlockSpec((1, LANES), lambda bi, h: (0, 0))],
        out_specs=pl.BlockSpec((1, tn, LANES), lambda bi, h: (bi, 0, h)),
        compiler_params=_cparams("parallel", "parallel"),
        name="mixer_a_decode_attention",
    )(qh, past_k, past_v, kh, vh, lam_vec, subln)


def _ffn_kernel(o_ref, x_ref, ogate_ref, wo_ref, sh_ref, sc_ref, gate_ref, g_ref, wgu_ref, wd_ref,
                out_ref, *, nb, tb, ff, fc):
    d = x_ref.shape[-1]
    y = jnp.dot(o_ref[...].reshape(nb * tb, o_ref.shape[-1]), wo_ref[...],
                preferred_element_type=F32)
    h = x_ref[...] + ogate_ref[...] * y.reshape(nb, tb, d)
    u = _modulate(h, g_ref[...], sh_ref[...], sc_ref[...]).reshape(nb * tb, d).astype(BF16)
    acc = jnp.zeros((nb * tb, d), F32)
    for c in range(ff // fc):
        g = jnp.dot(u, wgu_ref[:, c * fc:(c + 1) * fc], preferred_element_type=F32)
        up = jnp.dot(u, wgu_ref[:, ff + c * fc:ff + (c + 1) * fc], preferred_element_type=F32)
        a = ((g * _sigmoid(g)) * up).astype(BF16)
        acc = acc + jnp.dot(a, wd_ref[c * fc:(c + 1) * fc, :], preferred_element_type=F32)
    out_ref[...] = h + gate_ref[...] * acc.reshape(nb, tb, d)


def _ffn_call(o, x, o_gate, w_o, shift, scale, gate, g, w_gu, w_d, rows):
    b, t, d = x.shape
    k = o.shape[-1]
    ff = w_d.shape[0]
    fc = ff // 2 if (ff // 2) % LANES == 0 else ff
    nb, tb = _row_tiling(b, t, rows)
    nt = t // tb
    xmap = lambda i: (i // nt, i % nt, 0)
    mspec = pl.BlockSpec((nb, 1, d), lambda i: (i // nt, 0, 0))
    return pl.pallas_call(
        functools.partial(_ffn_kernel, nb=nb, tb=tb, ff=ff, fc=fc),
        out_shape=jax.ShapeDtypeStruct(x.shape, F32),
        grid=(b // nb * nt,),
        in_specs=[pl.BlockSpec((nb, tb, k), xmap),
                  pl.BlockSpec((nb, tb, d), xmap),
                  mspec,
                  _const_spec((k, d)),
                  mspec, mspec, mspec,
                  _const_spec((1, d)),
                  _const_spec((d, 2 * ff)),
                  _const_spec((ff, d))],
        out_specs=pl.BlockSpec((nb, tb, d), xmap),
        compiler_params=_cparams("parallel"),
        name="out_proj_dense_swiglu",
    )(o, x, o_gate, w_o, shift, scale, gate, g, w_gu, w_d)


def _qkv_b_kernel(x_ref, shq_ref, scq_ref, gq_ref, shkv_ref, sckv_ref, gkv_ref, wq_ref, wkv_ref,
                  gsum_ref, qn_ref, kn_ref, cos_ref, s1_ref, s2_ref, q_ref, k_ref, v_ref,
                  *, nb, tb):
    d = x_ref.shape[-1]
    qk_scale = HEAD_DIM ** -0.5 * LOG2_E
    parts = 1
    tr = tb // parts
    for p in range(parts):
        rows = slice(p * tr, (p + 1) * tr)
        y = _rms_scale(x_ref[:, rows, :])
        uq = ((y * gq_ref[...]) * (1.0 + scq_ref[...]) + shq_ref[...]).reshape(nb * tr, d)
        ukv = ((y * gkv_ref[...]) * (1.0 + sckv_ref[...]) + shkv_ref[...]).reshape(nb * tr, d)
        q = jnp.dot(uq.astype(BF16), wq_ref[...], preferred_element_type=F32)
        kv = jnp.dot(ukv.astype(BF16), wkv_ref[...], preferred_element_type=F32)
        cos, s1, s2 = (_tile_rows(r[rows, :], nb) for r in (cos_ref, s1_ref, s2_ref))
        q_p = _group_norm_rope(q, gsum_ref, qn_ref[...], cos, s1, s2)
        k_p = _group_norm_rope(kv[:, :LANES], gsum_ref, kn_ref[...], cos, s1, s2)
        for j, piece in enumerate(q_p):
            q_ref[:, rows, j * LANES:(j + 1) * LANES] = (
                piece * qk_scale).reshape(nb, tr, LANES).astype(BF16)
        k_ref[:, rows, :] = k_p[0].reshape(nb, tr, LANES)
        v_ref[:, rows, :] = kv[:, LANES:].reshape(nb, tr, LANES)


def _qkv_b_call(x, mod_q, mod_kv, wq, wkv, gsum, qn, kn, tables, rows):
    b, t, d = x.shape
    nq = wq.shape[1]
    assert wkv.shape[1] == 2 * LANES
    nb, tb = _row_tiling(b, t, rows)
    nt = t // tb
    xmap = lambda i: (i // nt, i % nt, 0)
    mmap = lambda i: (i // nt, 0, 0)
    tmap = lambda i: (i % nt, 0)
    mspec = pl.BlockSpec((nb, 1, d), mmap)
    return pl.pallas_call(
        functools.partial(_qkv_b_kernel, nb=nb, tb=tb),
        out_shape=(jax.ShapeDtypeStruct((b, t, nq), BF16),
                   jax.ShapeDtypeStruct((b, t, LANES), F32),
                   jax.ShapeDtypeStruct((b, t, LANES), F32)),
        grid=(b // nb * nt,),
        in_specs=[pl.BlockSpec((nb, tb, d), xmap),
                  mspec, mspec, _const_spec((1, d)),
                  mspec, mspec, _const_spec((1, d)),
                  _const_spec((d, nq)),
                  _const_spec((d, 2 * LANES)),
                  _const_spec((MXU_DIM, MXU_DIM)),
                  _const_spec((1, nq)),
                  _const_spec((1, LANES)),
                  pl.BlockSpec((tb, LANES), tmap),
                  pl.BlockSpec((tb, LANES), tmap),
                  pl.BlockSpec((tb, LANES), tmap)],
        out_specs=(pl.BlockSpec((nb, tb, nq), xmap),
                   pl.BlockSpec((nb, tb, LANES), xmap),
                   pl.BlockSpec((nb, tb, LANES), xmap)),
        compiler_params=_cparams("parallel"),
        name="mixer_b_qkv",
    )(x, mod_q[0], mod_q[1], mod_q[2], mod_kv[0], mod_kv[1], mod_kv[2], wq, wkv, gsum, qn, kn,
      *tables)


def _tile_kv_head(a, n, groups):
    lane = lax.broadcasted_iota(jnp.int32, a.shape, 1)
    swapped = pltpu.roll(a, HEAD_DIM, 1)
    keep = (lane < HEAD_DIM) if n == 0 else (lane >= HEAD_DIM)
    pair = jnp.where(keep, a, swapped).astype(BF16)
    return jnp.concatenate([pair] * (groups // 2), axis=1)


def _sink_group_attention(qs, kt, vt, mask, sink_col, groups):
    rows, width = qs.shape
    head_of_lane = lax.broadcasted_iota(jnp.int32, (rows, width), 1) // HEAD_DIM
    zero = jnp.zeros_like(qs)
    lhs = jnp.concatenate([jnp.where(head_of_lane == h, qs, zero) for h in range(groups)], axis=0)
    s = lax.dot_general(lhs, kt, (((1,), (1,)), ((), ())), preferred_element_type=F32)
    s = jnp.where(mask, s, NEG_INF)
    sink = sink_col * LOG2_E
    m = jnp.maximum(jnp.max(s, axis=1, keepdims=True), sink)
    e = jnp.exp2(s - m)
    den = jnp.sum(e, axis=1, keepdims=True) + jnp.exp2(sink - m)
    ob = jnp.dot((e / den).astype(BF16), vt, preferred_element_type=F32)
    out = jnp.zeros((rows, width), F32)
    for h in range(groups):
        out = out + jnp.where(head_of_lane == h, ob[h * rows:(h + 1) * rows], 0.0)
    return out


def _window_attn_kernel(q_ref, kprev_ref, kcur_ref, vprev_ref, vcur_ref, sink_ref, o_ref,
                        *, tq, groups, kv_heads, stack):
    q = q_ref[0]
    kc = jnp.concatenate([kprev_ref[0], kcur_ref[0]], axis=0)
    vc = jnp.concatenate([vprev_ref[0], vcur_ref[0]], axis=0)
    win = kprev_ref.shape[1]
    keys = win + tq
    first_chunk = pl.program_id(1) * (tq // CHUNK)
    q_chunk = lax.broadcasted_iota(jnp.int32, (tq, keys), 0) // CHUNK
    k_chunk = lax.broadcasted_iota(jnp.int32, (tq, keys), 1) // CHUNK - win // CHUNK
    visible = ((k_chunk <= q_chunk) & (k_chunk >= q_chunk - WIN_CHUNKS)
               & (k_chunk + first_chunk >= 0))
    bias = jnp.where(visible, 0.0, NEG_INF)
    lane = lax.broadcasted_iota(jnp.int32, (tq, LANES), 1)
    in_half = (lane < HEAD_DIM, lane >= HEAD_DIM)
    zero = jnp.zeros((tq, LANES), BF16)
    nh = 2 * stack
    bias_n = jnp.concatenate([bias] * nh, axis=0)
    head_of_row = lax.broadcasted_iota(jnp.int32, (nh * tq, 1), 0) // tq
    pairs = groups // 2
    for n in range(kv_heads):
        kt = _tile_kv_head(kc, n, 2)
        vt = _tile_kv_head(vc, n, 2)
        for j0 in range(0, pairs, stack):
            blk0 = n * pairs + j0
            qm = jnp.concatenate(
                [jnp.where(in_half[i % 2], q[:, (blk0 + i // 2) * LANES:(blk0 + i // 2 + 1) * LANES], zero)
                 for i in range(nh)], axis=0)
            sink = jnp.zeros((nh * tq, 1), F32)
            for i in range(nh):
                sink = jnp.where(head_of_row == i, sink_ref[2 * blk0 + i] * LOG2_E, sink)
            s = lax.dot_general(qm, kt, (((1,), (1,)), ((), ())), preferred_element_type=F32)
            s = s + bias_n
            m = jnp.maximum(jnp.max(s, axis=1, keepdims=True), sink)
            e = jnp.exp2(s - m)
            den = jnp.sum(e, axis=1, keepdims=True) + jnp.exp2(sink - m)
            o = jnp.dot(e.astype(BF16), vt, preferred_element_type=F32) / den
            for i in range(stack):
                o_ref[0, :, (blk0 + i) * LANES:(blk0 + i + 1) * LANES] = jnp.where(
                    in_half[0], o[2 * i * tq:(2 * i + 1) * tq],
                    o[(2 * i + 1) * tq:(2 * i + 2) * tq]).astype(BF16)


def _window_attn_call(q, k, v, sinks, kv_heads, tq):
    b, t, nq = q.shape
    groups = nq // (kv_heads * HEAD_DIM)
    win = WIN_CHUNKS * CHUNK
    assert t % tq == 0 and tq % win == 0 and groups % 2 == 0 and win == LANES
    ratio = tq // win
    cur_spec = pl.BlockSpec((1, tq, LANES), lambda bi, i: (bi, i, 0))
    prev_spec = pl.BlockSpec((1, win, LANES), lambda bi, i: (bi, jnp.maximum(i * ratio - 1, 0), 0))
    return pl.pallas_call(
        functools.partial(_window_attn_kernel, tq=tq, groups=groups, kv_heads=kv_heads, stack=4),
        out_shape=jax.ShapeDtypeStruct((b, t, nq), BF16),
        grid=(b, t // tq),
        in_specs=[pl.BlockSpec((1, tq, nq), lambda bi, i: (bi, i, 0)),
                  prev_spec, cur_spec, prev_spec, cur_spec,
                  pl.BlockSpec(memory_space=pltpu.SMEM)],
        out_specs=pl.BlockSpec((1, tq, nq), lambda bi, i: (bi, i, 0)),
        compiler_params=_cparams("parallel", "parallel"),
        name="mixer_b_window_attention",
    )(q, k, k, v, v, sinks)


def _window_decode_kernel(q_ref, kp_ref, kn_ref, vp_ref, vn_ref, sink_ref, o_ref,
                          *, past_len, groups, kv_heads):
    q = q_ref[0]
    tn = q.shape[0]
    win = kp_ref.shape[1]
    keys = 2 * LANES
    kc = _pad_rows(jnp.concatenate([kp_ref[0], kn_ref[0]], axis=0), keys)
    vc = _pad_rows(jnp.concatenate([vp_ref[0], vn_ref[0]], axis=0), keys)
    width = groups * HEAD_DIM
    rows = groups * tn
    shape = (rows, keys)
    col = lax.broadcasted_iota(jnp.int32, shape, 1)
    q_chunk = (past_len + lax.broadcasted_iota(jnp.int32, shape, 0) % tn) // CHUNK
    k_pos = past_len - win + col
    k_chunk = k_pos // CHUNK
    mask = ((k_chunk <= q_chunk) & (k_chunk >= q_chunk - WIN_CHUNKS) & (k_pos >= 0)
            & (col < win + tn))
    outs = []
    for n in range(kv_heads):
        outs.append(_sink_group_attention(q[:, n * width:(n + 1) * width],
                                          _tile_kv_head(kc, n, groups), _tile_kv_head(vc, n, groups),
                                          mask, sink_ref[n], groups))
    o_ref[0] = jnp.concatenate(outs, axis=1).astype(BF16)


def _window_decode_call(q, k_new, v_new, past_k, past_v, sink_cols, past_len):
    b, tn, nq = q.shape
    win = past_k.shape[1]
    kv_heads = sink_cols.shape[0]
    groups = nq // (kv_heads * HEAD_DIM)
    assert win + tn <= 2 * LANES
    new_spec = pl.BlockSpec((1, tn, LANES), lambda bi: (bi, 0, 0))
    past_spec = pl.BlockSpec((1, win, LANES), lambda bi: (bi, 0, 0))
    return pl.pallas_call(
        functools.partial(_window_decode_kernel, past_len=past_len, groups=groups, kv_heads=kv_heads),
        out_shape=jax.ShapeDtypeStruct((b, tn, nq), BF16),
        grid=(b,),
        in_specs=[pl.BlockSpec((1, tn, nq), lambda bi: (bi, 0, 0)),
                  past_spec, new_spec, past_spec, new_spec,
                  pl.BlockSpec(sink_cols.shape, lambda bi: (0, 0, 0))],
        out_specs=pl.BlockSpec((1, tn, nq), lambda bi: (bi, 0, 0)),
        compiler_params=_cparams("parallel"),
        name="mixer_b_decode_attention",
    )(q, past_k, k_new, past_v, v_new, sink_cols)


def _route_top2(logits, experts):
    lane = lax.broadcasted_iota(jnp.int32, logits.shape, 1).astype(F32)
    big = float(LANES)
    lg = jnp.where(lane < experts, logits, NEG_INF)
    m1 = jnp.max(lg, axis=1, keepdims=True)
    i1 = jnp.min(jnp.where(lg == m1, lane, big), axis=1, keepdims=True)
    lg2 = jnp.where(lane == i1, NEG_INF, lg)
    m2 = jnp.max(lg2, axis=1, keepdims=True)
    i2 = jnp.min(jnp.where(lg2 == m2, lane, big), axis=1, keepdims=True)
    e2 = jnp.exp(m2 - m1)
    den = 1.0 + e2
    return jnp.where(lane == i1, 1.0 / den, 0.0) + jnp.where(lane == i2, e2 / den, 0.0)


def _moe_kernel(o_ref, x_ref, ogate_ref, wo_ref, sh_ref, sc_ref, gate_ref, g_ref, rhi_ref, rlo_ref,
                wgu_hbm, wd_hbm, out_ref, u_s, rnk_s, gat_s, rnkt_s, off_s, xg_s, ye_s,
                wg_buf, wu_buf, wd_buf, w_sem, *, nb, tb, experts, plan):
    e = pl.program_id(1)
    k = pl.program_id(2)
    last_k = pl.num_programs(2) - 1
    d = x_ref.shape[-1]
    tm = nb * tb
    tc, rc, align, sizes, gsub, ssub = plan
    r2 = sizes[0]
    n_chunks = tm // tc
    slots = rnkt_s.shape[0]
    assert nb == 1 or (rc == tm and tc == tm)

    n_e = pl.num_programs(1)
    n_k = pl.num_programs(2)
    step = (pl.program_id(0) * n_e + e) * n_k + k
    n_steps = pl.num_programs(0) * n_e * n_k
    fc = wg_buf.shape[2]
    ff = wd_hbm.shape[1]
    ahead = WEIGHT_RING - 1

    def weight_copies(t):
        slot = t % WEIGHT_RING
        et = (t // n_k) % n_e
        col = pl.multiple_of((t % n_k) * fc, fc)
        up_col = pl.multiple_of(ff + (t % n_k) * fc, fc)
        return (pltpu.make_async_copy(wgu_hbm.at[et, :, pl.ds(col, fc)], wg_buf.at[slot], w_sem.at[0, slot]),
                pltpu.make_async_copy(wgu_hbm.at[et, :, pl.ds(up_col, fc)], wu_buf.at[slot], w_sem.at[1, slot]),
                pltpu.make_async_copy(wd_hbm.at[et, pl.ds(col, fc), :], wd_buf.at[slot], w_sem.at[2, slot]))

    @pl.when(step == 0)
    def _prime():
        for t in range(ahead):
            @pl.when(t < n_steps)
            def _start(t=t):
                for cp in weight_copies(t):
                    cp.start()

    @pl.when(step + ahead < n_steps)
    def _prefetch():
        for cp in weight_copies(step + ahead):
            cp.start()

    @pl.when((e == 0) & (k == 0))
    def _route():
        row_i = lax.broadcasted_iota(jnp.int32, (rc, rc), 0)
        col_i = lax.broadcasted_iota(jnp.int32, (rc, rc), 1)
        below = jnp.where(col_i < row_i, 1.0, 0.0).astype(BF16)

        def token_chunk(c, used):
            used = jnp.floor((used + (align - 1)) * (1.0 / align)) * align
            for x in range(experts):
                off_s[c * slots + x] = used[0, x].astype(jnp.int32)
            for i in range(tc // rc):
                rows = pl.ds(pl.multiple_of(c * tc + i * rc, rc), rc)
                ox, xx = (o_ref[:, rows, :], x_ref[:, rows, :]) if nb == 1 else (o_ref[...], x_ref[...])
                y = jnp.dot(ox.reshape(rc, ox.shape[-1]), wo_ref[...], preferred_element_type=F32)
                hx = xx + ogate_ref[...] * y.reshape(xx.shape)
                if nb == 1:
                    out_ref[:, rows, :] = hx
                else:
                    out_ref[...] = hx
                u = _modulate(hx, g_ref[...], sh_ref[...], sc_ref[...]).reshape(rc, d)
                ub = u.astype(BF16)
                ulo = (u - ub.astype(F32)).astype(BF16)
                rhi = rhi_ref[...]
                logits = (jnp.dot(ub, rhi, preferred_element_type=F32)
                          + jnp.dot(ub, rlo_ref[...], preferred_element_type=F32)
                          + jnp.dot(ulo, rhi, preferred_element_type=F32))
                dg = _route_top2(logits, experts)
                sel = dg > 0.0
                sel_f = jnp.where(sel, 1.0, 0.0)
                rank = jnp.dot(below, sel_f.astype(BF16), preferred_element_type=F32) + used
                u_s[rows, :] = ub
                gat_s[rows, :] = dg
                rnk_s[rows, :] = jnp.where(sel, rank, -1.0)
                used = used + jnp.sum(sel_f, axis=0, keepdims=True)
            return used

        used = lax.fori_loop(0, n_chunks, token_chunk, jnp.zeros((1, LANES), F32))
        for x in range(experts):
            off_s[n_chunks * slots + x] = used[0, x].astype(jnp.int32)
        rnkt_s[...] = rnk_s[...].T[:slots]

    total = off_s[n_chunks * slots + e]
    whole = total // r2
    rest = total - whole * r2
    n_full = whole + (rest > sizes[1]).astype(jnp.int32) if len(sizes) > 1 else (total + r2 - 1) // r2
    cover_end = n_full * r2
    for i in range(1, len(sizes)):
        lo = sizes[i + 1] if i + 1 < len(sizes) else 0
        cover_end = cover_end + jnp.where((rest > lo) & (rest <= sizes[i]), sizes[i], 0)

    def for_each_tile(fn):
        def full_tile(t, carry):
            fn(pl.multiple_of(t * r2, align), r2)
            return carry

        lax.fori_loop(0, n_full, full_tile, 0)
        for i in range(1, len(sizes)):
            lo = sizes[i + 1] if i + 1 < len(sizes) else 0

            @pl.when((rest > lo) & (rest <= sizes[i]))
            def _small_tile(rows=sizes[i]):
                fn(pl.multiple_of(whole * r2, align), rows)

    def segment(c):
        start = off_s[c * slots + e]
        return start, off_s[(c + 1) * slots + e] - start

    def clear_rows(base, rows):
        xg_s[pl.ds(base, rows), :] = jnp.zeros((rows, d), BF16)
        ye_s[pl.ds(base, rows), :] = jnp.zeros((rows, d), F32)

    def expert_rows(base, rows):
        x = xg_s[pl.ds(base, rows), :]
        slot = step % WEIGHT_RING
        g = jnp.dot(x, wg_buf[slot], preferred_element_type=F32)
        up = jnp.dot(x, wu_buf[slot], preferred_element_type=F32)
        a = ((g * _sigmoid(g)) * up).astype(BF16)
        ye_s[pl.ds(base, rows), :] += jnp.dot(a, wd_buf[slot], preferred_element_type=F32)

    @pl.when(k == 0)
    def _gather():
        for_each_tile(clear_rows)
        for c in range(n_chunks):
            start, length = segment(c)
            pos = rnkt_s[pl.ds(e, 1), c * tc:(c + 1) * tc]

            def sub_tile(t, carry, c=c, start=start, pos=pos):
                base = pl.multiple_of(start + t * gsub, align)
                want = (base + lax.broadcasted_iota(jnp.int32, (gsub, tc), 0)).astype(F32)
                pick = jnp.where(pos == want, 1.0, 0.0).astype(BF16)
                xg_s[pl.ds(base, gsub), :] = jnp.dot(
                    pick, u_s[c * tc:(c + 1) * tc, :], preferred_element_type=F32).astype(BF16)
                return carry

            lax.fori_loop(0, (length + gsub - 1) // gsub, sub_tile, 0)

    for cp in weight_copies(step):
        cp.wait()
    for_each_tile(expert_rows)

    @pl.when(k == last_k)
    def _scatter():
        lane = lax.broadcasted_iota(jnp.int32, (tm, LANES), 1)
        pos_all = jnp.sum(jnp.where(lane == e, rnk_s[...], 0.0), axis=1, keepdims=True)
        gate_all = jnp.sum(jnp.where(lane == e, gat_s[...], 0.0), axis=1, keepdims=True)
        for c in range(n_chunks):
            start, length = segment(c)
            pos = pos_all[c * tc:(c + 1) * tc]
            gate = gate_all[c * tc:(c + 1) * tc]

            def sub_tile(t, carry, c=c, start=start, pos=pos, gate=gate):
                first = start + t * ssub
                base = pl.multiple_of(jnp.minimum(first, cover_end - ssub), align)
                want = base + lax.broadcasted_iota(jnp.int32, (tc, ssub), 1)
                hit = (pos == want.astype(F32)) & (want >= first)
                place = jnp.where(hit, gate, 0.0).astype(BF16)
                y = ye_s[pl.ds(base, ssub), :].astype(BF16)
                add = jnp.dot(place, y, preferred_element_type=F32)
                if nb == 1:
                    out_ref[:, c * tc:(c + 1) * tc, :] += gate_ref[...] * add.reshape(1, tc, d)
                else:
                    out_ref[...] += gate_ref[...] * add.reshape(nb, tb, d)
                return carry

            lax.fori_loop(0, (length + ssub - 1) // ssub, sub_tile, 0)


def _moe_plan(tm):
    align = 16
    tc = min(2 * MXU_DIM, tm)
    rc = min(MXU_DIM, tm)
    r2 = -(-(tc + tc // 8) // align) * align
    gsub = min(tc, tc // 4 + 2 * align)
    ssub = min(MXU_DIM, tc)
    sizes = [r2] + [r for r in (r2 // 2, r2 // 4) if r % align == 0 and r >= ssub]
    most = tm + (tm // tc) * (align - 1)
    cover = -(-most // r2) * r2
    cap = -(-max(cover, most + gsub) // align) * align
    return (tc, rc, align, sizes, gsub, ssub), cap


def _moe_call(o, x, o_gate, w_o, shift, scale, gate, g, r_hi, r_lo, w_gu, w_d, experts, rows):
    b, t, d = x.shape
    ko = o.shape[-1]
    ff = w_d.shape[1]
    fc = MOE_FF_CHUNK
    assert ff % fc == 0
    nk = ff // fc
    nb, tb = _row_tiling(b, t, rows)
    nt = t // tb
    tm = nb * tb
    xmap = lambda i, e, k: (i // nt, i % nt, 0)
    mmap = lambda i, e, k: (i // nt, 0, 0)
    mspec = pl.BlockSpec((nb, 1, d), mmap)
    sub = 8
    plan, cap = _moe_plan(tm)
    assert experts <= sub and tm % plan[0] == 0
    return pl.pallas_call(
        functools.partial(_moe_kernel, nb=nb, tb=tb, experts=experts, plan=plan),
        out_shape=jax.ShapeDtypeStruct(x.shape, F32),
        grid=(b // nb * nt, experts, nk),
        in_specs=[pl.BlockSpec((nb, tb, ko), xmap, pipeline_mode=pl.Buffered(1)),
                  pl.BlockSpec((nb, tb, d), xmap, pipeline_mode=pl.Buffered(1)),
                  mspec,
                  _const_spec((ko, d)),
                  mspec, mspec, mspec,
                  _const_spec((1, d)),
                  _const_spec((d, LANES)),
                  _const_spec((d, LANES)),
                  pl.BlockSpec(memory_space=pl.ANY),
                  pl.BlockSpec(memory_space=pl.ANY)],
        out_specs=pl.BlockSpec((nb, tb, d), xmap, pipeline_mode=pl.Buffered(1)),
        scratch_shapes=[pltpu.VMEM((tm, d), BF16),
                        pltpu.VMEM((tm, LANES), F32),
                        pltpu.VMEM((tm, LANES), F32),
                        pltpu.VMEM((sub, tm), F32),
                        pltpu.SMEM(((tm // plan[0] + 1) * sub,), jnp.int32),
                        pltpu.VMEM((cap, d), BF16),
                        pltpu.VMEM((cap, d), F32),
                        pltpu.VMEM((WEIGHT_RING, d, fc), BF16),
                        pltpu.VMEM((WEIGHT_RING, d, fc), BF16),
                        pltpu.VMEM((WEIGHT_RING, fc, d), BF16),
                        pltpu.SemaphoreType.DMA((3, WEIGHT_RING))],
        compiler_params=_cparams("arbitrary", "arbitrary", "arbitrary"),
        name="expert_swiglu",
    )(o, x, o_gate, w_o, shift, scale, gate, g, r_hi, r_lo, w_gu, w_d)


def _rope_tables(pos):
    half = ROT_DIM // 2
    inv = ROPE_THETA ** (-(jnp.arange(half, dtype=F32) * 2.0 / ROT_DIM))
    ang = pos.astype(F32)[:, None] * inv[None, :]
    cos, sin = jnp.cos(ang), jnp.sin(ang)
    t = pos.shape[0]
    zeros = lambda n: jnp.zeros((t, n), F32)
    c = jnp.concatenate([cos, cos, jnp.ones((t, HEAD_DIM - ROT_DIM), F32)], axis=1)
    s1 = jnp.concatenate([-sin, zeros(HEAD_DIM - half)], axis=1)
    s2 = jnp.concatenate([zeros(half), sin, zeros(HEAD_DIM - ROT_DIM)], axis=1)
    rep = LANES // HEAD_DIM
    return tuple(jnp.tile(a, (1, rep)) for a in (c, s1, s2))


def _mod_rows(mod, lo, hi):
    d = mod.shape[1]
    return mod[lo:hi].reshape(hi - lo, 1, d)


def _trunk(x, mods, kv_mods, w, past, rows):
    b, t, d = x.shape
    past_len = 0 if past is None else past[0].shape[2]
    pos = past_len + jnp.arange(t, dtype=jnp.int32)
    tables = _rope_tables(pos)
    heads_a = w["a_heads"]
    big = t >= rows

    m0 = mods[0]
    lam_init = 0.8 - 0.6 * math.exp(-0.3 * 0)
    qh, kh, vh, a_k, a_v = _qkv_a_call(x, m0[0], m0[1], w["norm_g"][0][0], w["a_w_qkv"], w["gsum"],
                                       w["a_q_gain"], w["a_k_gain"], tables, heads_a, rows)
    if past is None:
        o = _diff_attn_call(qh, kh, vh, w["a_lambda"], w["a_subln"], lam_init, 256, 512, 8)
    else:
        pk = past[0][0].reshape(b, past_len, heads_a * LANES)
        pv = past[1][0].reshape(b, past_len, heads_a * LANES)
        o = _diff_decode_call(qh, kh, vh, pk, pv, w["a_lambda"], w["a_subln"], lam_init)
    h = _ffn_call(o, x, m0[2], w["a_w_o"], m0[3], m0[4], m0[5], w["norm_g"][0][1],
                  w["f_w_gu"], w["f_w_d"], rows)

    m1 = mods[1]
    q, k_sh, v_sh = _qkv_b_call(h, (m1[0], m1[1], w["norm_g"][1][0]),
                                (kv_mods[0], kv_mods[1], w["kv_norm_g"]),
                                w["b_w_q"], w["kv_w"], w["gsum"], w["b_q_gain"], w["kv_k_gain"],
                                tables, rows)
    if past is None:
        o = _window_attn_call(q, k_sh, v_sh, w["sinks"], w["kv_heads"], 128)
    else:
        win = past[2].shape[1]
        o = _window_decode_call(q, k_sh, v_sh, past[2].reshape(b, win, LANES),
                                past[3].reshape(b, win, LANES), w["sink_cols_decode"], past_len)
    h = _moe_call(o, h, m1[2], w["b_w_o"], m1[3], m1[4], m1[5], w["norm_g"][1][1], w["r_hi"], w["r_lo"],
                  w["m_w_gu"], w["m_w_d"], w["experts"], min(2048, t) if big else rows)
    return h, a_k, a_v, k_sh, v_sh


def kernel(x_prompt, x_sample, c_prompt, c_sample, cache_a_k, cache_a_v, cache_b_k, cache_b_v,
           w_mod, b_mod, norm_g, a_w_qkv, a_q_norm, a_k_norm, a_lambda, a_subln, a_w_o,
           kv_norm_g, kv_w_mod, kv_b_mod, kv_w, kv_k_norm, b_w_q, b_q_norm, b_sinks, b_w_o,
           f_w_gu, f_w_d, m_router, m_w_gu, m_w_d):
    bp, t, d = x_prompt.shape
    bs, tn, _ = x_sample.shape
    assert w_mod.shape[0] == 2 and a_w_qkv.shape[0] == 1 and b_w_q.shape[0] == 1
    heads_a = cache_a_k.shape[3]
    kv_heads = cache_b_k.shape[2]
    heads_b = b_sinks.shape[1]
    groups = heads_b // kv_heads
    experts = m_router.shape[2]

    c_all = jnp.concatenate([c_prompt, c_sample], axis=0)
    mods = [_mod_call(c_all, w_mod[l], b_mod[l]) for l in range(2)]
    kv_mod = _mod_call(c_all, kv_w_mod, kv_b_mod)

    def split(mod, n, lo, hi):
        m = mod[lo:hi].reshape(hi - lo, n, 1, d)
        return [m[:, j] for j in range(n)]

    group = jnp.arange(MXU_DIM) // HEAD_DIM
    router = jnp.pad(m_router[0], ((0, 0), (0, LANES - experts)))
    r_hi = router.astype(BF16)

    def sink_cols(rows_per_head):
        s = b_sinks[0].astype(F32).reshape(kv_heads, groups, 1)
        return jnp.broadcast_to(s, (kv_heads, groups, rows_per_head)).reshape(
            kv_heads, groups * rows_per_head, 1)

    w = dict(
        a_heads=heads_a, experts=experts,
        norm_g=[[norm_g[l, j].reshape(1, d) for j in range(2)] for l in range(2)],
        gsum=(group[:, None] == group[None, :]).astype(BF16),
        a_w_qkv=a_w_qkv[0].astype(BF16),
        a_q_gain=jnp.tile(a_q_norm[0].reshape(1, LANES), (1, heads_a)),
        a_k_gain=jnp.tile(a_k_norm[0].reshape(1, LANES), (1, heads_a)),
        a_lambda=a_lambda[0].astype(F32),
        a_subln=a_subln[0].reshape(1, LANES),
        a_w_o=a_w_o[0].astype(BF16),
        f_w_gu=f_w_gu[0].astype(BF16), f_w_d=f_w_d[0].astype(BF16),
        kv_norm_g=kv_norm_g.reshape(1, d),
        kv_w=kv_w.astype(BF16),
        kv_k_gain=jnp.tile(kv_k_norm.reshape(1, HEAD_DIM), (1, LANES // HEAD_DIM)),
        b_w_q=b_w_q[0].astype(BF16),
        b_q_gain=jnp.tile(b_q_norm[0].reshape(1, HEAD_DIM), (1, heads_b)),
        b_w_o=b_w_o[0].astype(BF16),
        sinks=b_sinks[0].astype(F32), kv_heads=kv_heads, sink_cols_decode=sink_cols(tn),
        r_hi=r_hi, r_lo=(router - r_hi.astype(F32)).astype(BF16),
        m_w_gu=m_w_gu[0].astype(BF16), m_w_d=m_w_d[0].astype(BF16),
    )

    rows = 512
    y_p, ak_p, av_p, bk_p, bv_p = _trunk(
        x_prompt, [split(m, 6, 0, bp) for m in mods], split(kv_mod, 2, 0, bp), w, None, rows)
    y_s, ak_s, av_s, bk_s, bv_s = _trunk(
        x_sample, [split(m, 6, bp, bp + bs) for m in mods], split(kv_mod, 2, bp, bp + bs), w,
        (cache_a_k, cache_a_v, cache_b_k, cache_b_v), rows)

    win = min(WIN_CHUNKS * CHUNK, t)
    a_k_shape = lambda b_, t_: (1, b_, t_, heads_a, 2, HEAD_DIM)
    a_v_shape = lambda b_, t_: (1, b_, t_, heads_a, 2 * HEAD_DIM)
    b_shape = lambda b_, t_: (b_, t_, kv_heads, HEAD_DIM)
    return (y_p, y_s,
            ak_p.reshape(a_k_shape(bp, t)), av_p.reshape(a_v_shape(bp, t)),
            bk_p[:, -win:].reshape(b_shape(bp, win)), bv_p[:, -win:].reshape(b_shape(bp, win)),
            ak_s.reshape(a_k_shape(bs, tn)), av_s.reshape(a_v_shape(bs, tn)),
            bk_s.reshape(b_shape(bs, tn)), bv_s.reshape(b_shape(bs, tn)))
```
